```python
import math
import jax
import jax.numpy as jnp
from jax import lax
import numpy as np

D_MODEL = 4096
BATCH = 2
SEQ = 8192
DEPTH = 2

GRID_W = 64
CTX_LEN = 256
Q_BLOCK = 128
ROPE_THETA = 10000.0
NORM_EPS = 1e-6
SUBLN_EPS = 1e-5
GN_EPS = 64e-5

N_MIXERS = 4
GROUP_W = D_MODEL // N_MIXERS
MIX_W = N_MIXERS * GROUP_W

RWKV_HEAD = 64
RWKV_HEADS = GROUP_W // RWKV_HEAD
DECAY_LORA = max(32, int(round(1.8 * D_MODEL ** 0.5 / 32)) * 32)
AAA_LORA = max(32, int(round(1.8 * D_MODEL ** 0.5 / 32)) * 32)
GATE_LORA = max(32, int(round(0.6 * D_MODEL ** 0.8 / 32)) * 32)

DIFF_HEADS = 8
DIFF_HEAD = GROUP_W // DIFF_HEADS // 2

GQA_HEADS = 8
GQA_KV_HEADS = 2
GQA_HEAD = GROUP_W // GQA_HEADS

MLA_HEADS = 8
MLA_V = GROUP_W // MLA_HEADS
MLA_NOPE = 128
MLA_ROPE = 64
Q_LORA = 768
KV_LORA = 512

D_FF = 11008
N_EXPERTS = 8
TOP_K = 2
D_FF_EXPERT = 4096
MOE_BLOCK = 512
N_DENSE = (DEPTH + 1) // 2
N_MOE = DEPTH // 2

RWKV_COLS = (GROUP_W, GROUP_W, GROUP_W, DECAY_LORA, AAA_LORA, GATE_LORA)
DIFF_COLS = (DIFF_HEADS * 2 * DIFF_HEAD, DIFF_HEADS * 2 * DIFF_HEAD, DIFF_HEADS * 2 * DIFF_HEAD)
GQA_COLS = (GQA_HEADS * GQA_HEAD, GQA_KV_HEADS * GQA_HEAD, GQA_KV_HEADS * GQA_HEAD)
MLA_COLS = (Q_LORA, KV_LORA, MLA_ROPE)
GROUP_COLS = (sum(RWKV_COLS), sum(DIFF_COLS), sum(GQA_COLS), sum(MLA_COLS))
IN_W = sum(GROUP_COLS)

kernel_name = 'hybrid_rwkv7_diff_gqa_mla_moe_dit'


def _split(z, sizes):
    idx = [int(i) for i in np.cumsum(sizes)[:-1]]
    return jnp.split(z, idx, axis=-1)


def _rmsnorm(x, g, eps=NORM_EPS):
    xf = x.astype(jnp.float32)
    y = xf * lax.rsqrt(jnp.mean(xf * xf, axis=-1, keepdims=True) + eps)
    return (y * g.astype(jnp.float32)).astype(x.dtype)


def _modulate(h, shift, scale):
    return h * (1.0 + scale) + shift


def _axial_rope(n_tokens, rot_dim):
    rows = n_tokens // GRID_W
    row = jnp.repeat(jnp.arange(rows), GRID_W).astype(jnp.float32)
    col = jnp.tile(jnp.arange(GRID_W), rows).astype(jnp.float32)
    n_freq = rot_dim // 4
    inv = ROPE_THETA ** (-jnp.arange(n_freq, dtype=jnp.float32) / n_freq)
    ang = jnp.concatenate([row[:, None] * inv, col[:, None] * inv], axis=-1)
    return jnp.cos(ang), jnp.sin(ang)


def _apply_rope(x, cos, sin):
    half = x.shape[-1] // 2
    x1 = x[..., :half].astype(jnp.float32)
    x2 = x[..., half:].astype(jnp.float32)
    cs = cos[None, :, None, :]
    sn = sin[None, :, None, :]
    return jnp.concatenate([x1 * cs - x2 * sn, x1 * sn + x2 * cs], axis=-1).astype(x.dtype)


def _sweep_queries(block_fn, q):
    b, t = q.shape[:2]
    blk = Q_BLOCK if t % Q_BLOCK == 0 else t
    nb = t // blk
    qb = jnp.moveaxis(q.reshape((b, nb, blk) + q.shape[2:]), 1, 0)
    out = jnp.moveaxis(lax.map(block_fn, qb), 0, 1)
    return out.reshape((b, t) + out.shape[3:])


def _gqa_block(qb, k, v, scale):
    s = jnp.einsum('bqkgd,bskd->bkgqs', qb, k).astype(jnp.float32) * scale
    p = jax.nn.softmax(s, axis=-1).astype(v.dtype)
    return jnp.einsum('bkgqs,bskd->bqkgd', p, v)


def _attend(q, k, v, scale):
    out = _sweep_queries(lambda qb: _gqa_block(qb, k, v, scale), q)
    return out.reshape(out.shape[0], out.shape[1], -1)


def _diff_block(qb, k, v, lam):
    s = jnp.einsum('bqhmd,bshmd->bhmqs', qb, k).astype(jnp.float32) * DIFF_HEAD ** -0.5
    p = jax.nn.softmax(s, axis=-1)
    p = p[:, :, 0] - lam * p[:, :, 1]
    return jnp.einsum('bhqs,bshd->bqhd', p.astype(v.dtype), v)


def _centred_shift(p, mu):
    zero = jnp.zeros_like(p[:, :1])
    prev = jnp.concatenate([zero, p[:, :-1]], axis=1)
    nxt = jnp.concatenate([p[:, 1:], zero], axis=1)
    return p + mu * (0.5 * (prev + nxt) - p)


def _rwkv7_dir_inputs(r, k, v, wd, ad, w0, w_up, a0, a_up, k_k, k_a):
    b, t = r.shape[:2]
    hd = lambda z: z.reshape(b, t, RWKV_HEADS, RWKV_HEAD)
    w_log = -jax.nn.softplus(-(w0 + jnp.tanh(wd) @ w_up).astype(jnp.float32)) - 0.5
    decay = jnp.exp(-jnp.exp(w_log))
    a = jax.nn.sigmoid((a0 + ad @ a_up).astype(jnp.float32))
    kk = hd(k * k_k).astype(jnp.float32)
    kk = kk / jnp.maximum(jnp.sqrt(jnp.sum(kk * kk, axis=-1, keepdims=True)), 1e-12)
    kd = k.astype(jnp.float32) * (1.0 + (a - 1.0) * k_a)
    a = hd(a)
    return hd(r), hd(decay), hd(kd), hd(v), -kk, kk * a


def _rwkv7_scan(state, r, decay, k, v, a, b, reverse):
    def step(s, inp):
        r_t, w_t, k_t, v_t, a_t, b_t = inp
        sa = jnp.einsum('bhvk,bhk->bhv', s, a_t)
        s = s * w_t[:, :, None, :] + sa[..., None] * b_t[:, :, None, :] + v_t[..., None] * k_t[:, :, None, :]
        return s, jnp.einsum('bhvk,bhk->bhv', s, r_t)
    xs = tuple(jnp.moveaxis(z.astype(jnp.float32), 1, 0) for z in (r, decay, k, v, a, b))
    state, ys = lax.scan(step, state, xs, reverse=reverse)
    return state, jnp.moveaxis(ys, 0, 1)


def _rwkv7_bonus(ins, r_k):
    r, _, kd, v = ins[:4]
    return jnp.sum(r.astype(jnp.float32) * kd * r_k.astype(jnp.float32), axis=-1, keepdims=True) * v.astype(jnp.float32)


def _rwkv7_out(y, bonus, gd, g_up, lnx_w, lnx_b):
    b, t = y.shape[:2]
    mean = jnp.mean(y, axis=-1, keepdims=True)
    var = jnp.mean(jnp.square(y - mean), axis=-1, keepdims=True)
    yn = ((y - mean) * lax.rsqrt(var + GN_EPS)).reshape(b, t, GROUP_W) * lnx_w + lnx_b
    g = jax.nn.sigmoid(gd) @ g_up
    return (yn + bonus.reshape(b, t, GROUP_W)) * g


def _rwkv7_mixer(p_lat, p_ctx, mu, w0, w_up, a0, a_up, g_up, k_k, k_a, r_k, lnx_w, lnx_b, need_ctx):
    parts_lat = _split(_centred_shift(p_lat, mu), RWKV_COLS)
    parts_ctx = _split(_centred_shift(p_ctx, mu), RWKV_COLS)
    zero_state = jnp.zeros((p_lat.shape[0], RWKV_HEADS, RWKV_HEAD, RWKV_HEAD), jnp.float32)
    y_lat = bonus_lat = y_ctx = bonus_ctx = 0.0
    for d, reverse in ((0, False), (1, True)):
        dir_params = (w0[d], w_up[d], a0[d], a_up[d], k_k, k_a)
        ins_ctx = _rwkv7_dir_inputs(*parts_ctx[:5], *dir_params)
        ins_lat = _rwkv7_dir_inputs(*parts_lat[:5], *dir_params)
        state_ctx, yc = _rwkv7_scan(zero_state, *ins_ctx, reverse)
        _, yl = _rwkv7_scan(state_ctx, *ins_lat, reverse)
        y_lat = y_lat + yl
        bonus_lat = bonus_lat + _rwkv7_bonus(ins_lat, r_k)
        if need_ctx:
            y_ctx = y_ctx + yc
            bonus_ctx = bonus_ctx + _rwkv7_bonus(ins_ctx, r_k)
    out_lat = _rwkv7_out(y_lat, bonus_lat, parts_lat[5], g_up, lnx_w, lnx_b).astype(p_lat.dtype)
    out_ctx = _rwkv7_out(y_ctx, bonus_ctx, parts_ctx[5], g_up, lnx_w, lnx_b).astype(p_ctx.dtype) if need_ctx else None
    return out_lat, out_ctx


def _diff_attention(p_lat, p_ctx, lq1, lk1, lq2, lk2, subln_g, lambda_init, cos, sin, need_ctx):
    lam = (jnp.exp(jnp.sum(lq1.astype(jnp.float32) * lk1.astype(jnp.float32)))
           - jnp.exp(jnp.sum(lq2.astype(jnp.float32) * lk2.astype(jnp.float32))) + lambda_init)

    def heads(p, rope):
        q, k, v = _split(p, DIFF_COLS)
        b, t = p.shape[:2]
        q = q.reshape(b, t, DIFF_HEADS * 2, DIFF_HEAD)
        k = k.reshape(b, t, DIFF_HEADS * 2, DIFF_HEAD)
        if rope:
            q = _apply_rope(q, cos, sin)
            k = _apply_rope(k, cos, sin)
        return (q.reshape(b, t, DIFF_HEADS, 2, DIFF_HEAD), k.reshape(b, t, DIFF_HEADS, 2, DIFF_HEAD),
                v.reshape(b, t, DIFF_HEADS, 2 * DIFF_HEAD))

    def run(q, k, v):
        o = _sweep_queries(lambda qb: _diff_block(qb, k, v, lam), q)
        o = _rmsnorm(o, subln_g, SUBLN_EPS) * (1.0 - lambda_init)
        return o.reshape(o.shape[0], o.shape[1], GROUP_W)

    q_l, k_l, v_l = heads(p_lat, True)
    q_c, k_c, v_c = heads(p_ctx, False)
    out_lat = run(q_l, jnp.concatenate([k_l, k_c], axis=1), jnp.concatenate([v_l, v_c], axis=1))
    out_ctx = run(q_c, k_c, v_c) if need_ctx else None
    return out_lat, out_ctx


def _gqa_attention(p_lat, p_ctx, q_norm_g, k_norm_g, cos, sin, need_ctx):
    def heads(p, rope):
        q, k, v = _split(p, GQA_COLS)
        b, t = p.shape[:2]
        q = _rmsnorm(q.reshape(b, t, GQA_HEADS, GQA_HEAD), q_norm_g)
        k = _rmsnorm(k.reshape(b, t, GQA_KV_HEADS, GQA_HEAD), k_norm_g)
        v = v.reshape(b, t, GQA_KV_HEADS, GQA_HEAD)
        if rope:
            q = _apply_rope(q, cos, sin)
            k = _apply_rope(k, cos, sin)
        return q.reshape(b, t, GQA_KV_HEADS, GQA_HEADS // GQA_KV_HEADS, GQA_HEAD), k, v

    scale = GQA_HEAD ** -0.5
    q_l, k_l, v_l = heads(p_lat, True)
    q_c, k_c, v_c = heads(p_ctx, False)
    out_lat = _attend(q_l, jnp.concatenate([k_l, k_c], axis=1), jnp.concatenate([v_l, v_c], axis=1), scale)
    out_ctx = _attend(q_c, k_c, v_c, scale) if need_ctx else None
    return out_lat, out_ctx


def _mla_attention(p_lat, p_ctx, q_norm_g, q_up, kv_norm_g, kv_up, cos, sin, need_ctx):
    def heads(p, rope):
        c_q, c_kv, k_rope = _split(p, MLA_COLS)
        b, t = p.shape[:2]
        q = (_rmsnorm(c_q, q_norm_g) @ q_up).reshape(b, t, MLA_HEADS, MLA_NOPE + MLA_ROPE)
        kv = (_rmsnorm(c_kv, kv_norm_g) @ kv_up).reshape(b, t, MLA_HEADS, MLA_NOPE + MLA_V)
        q_nope, q_rope = q[..., :MLA_NOPE], q[..., MLA_NOPE:]
        k_nope, v = kv[..., :MLA_NOPE], kv[..., MLA_NOPE:]
        k_rope = k_rope[:, :, None, :]
        if rope:
            q_rope = _apply_rope(q_rope, cos, sin)
            k_rope = _apply_rope(k_rope, cos, sin)
        q = jnp.concatenate([q_nope, q_rope], axis=-1)[:, :, :, None, :]
        k = jnp.concatenate([k_nope, jnp.broadcast_to(k_rope, (b, t, MLA_HEADS, MLA_ROPE))], axis=-1)
        return q, k, v

    scale = (MLA_NOPE + MLA_ROPE) ** -0.5
    q_l, k_l, v_l = heads(p_lat, True)
    q_c, k_c, v_c = heads(p_ctx, False)
    out_lat = _attend(q_l, jnp.concatenate([k_l, k_c], axis=1), jnp.concatenate([v_l, v_c], axis=1), scale)
    out_ctx = _attend(q_c, k_c, v_c, scale) if need_ctx else None
    return out_lat, out_ctx


def _swiglu(h, w_gate, w_up, w_down):
    return (jax.nn.silu(h @ w_gate) * (h @ w_up)) @ w_down


def _moe(h, router, w_gate, w_up, w_down):
    b, t, d = h.shape
    n_tok = b * t
    tok = h.reshape(n_tok, d)
    logits = (tok @ router).astype(jnp.float32)
    top_logit, top_idx = lax.top_k(logits, TOP_K)
    gates = jax.nn.softmax(top_logit, axis=-1)
    flat_e = top_idx.reshape(-1)
    n_assign = n_tok * TOP_K
    order = jnp.argsort(flat_e)
    sorted_e = flat_e[order]
    counts = jnp.bincount(flat_e, length=N_EXPERTS)
    padded = (counts + MOE_BLOCK - 1) // MOE_BLOCK * MOE_BLOCK
    pad_end = jnp.cumsum(padded)
    pad_start = pad_end - padded
    start = jnp.cumsum(counts) - counts
    dest = pad_start[sorted_e] + jnp.arange(n_assign) - start[sorted_e]
    n_blocks = -(-n_assign // MOE_BLOCK) + N_EXPERTS
    token_of = order // TOP_K
    buf = jnp.zeros((n_blocks * MOE_BLOCK, d), h.dtype).at[dest].set(tok[token_of])
    block_expert = jnp.minimum(jnp.searchsorted(pad_end, jnp.arange(n_blocks) * MOE_BLOCK, side='right'), N_EXPERTS - 1)

    def expert_block(args):
        rows, e = args
        return _swiglu(rows, w_gate[e], w_up[e], w_down[e])

    out = lax.map(expert_block, (buf.reshape(n_blocks, MOE_BLOCK, d), block_expert))
    y = out.reshape(-1, d)[dest] * gates.reshape(-1)[order][:, None].astype(h.dtype)
    return jax.ops.segment_sum(y, token_of, num_segments=n_tok).reshape(b, t, d)


def setup_inputs(seed: int = 0) -> dict:
    key = jax.random.key(seed)
    ks = iter(jax.random.split(key, 48))
    f32 = jnp.float32

    def nrm(shape, scale):
        return jax.random.normal(next(ks), shape, f32) * scale

    def gain(shape, base=1.0):
        return base + 0.02 * jax.random.normal(next(ks), shape, f32)

    def unif(shape, lo, hi):
        return jax.random.uniform(next(ks), shape, f32, lo, hi)

    L, D = DEPTH, D_MODEL
    return {
        'x': nrm((BATCH, SEQ, D), 1.0),
        'c': nrm((BATCH, D), 1.0),
        'ctx': nrm((BATCH, CTX_LEN, D), 1.0),
        'c_ctx': nrm((D,), 1.0),
        'ada_w': nrm((L, D, 6 * D), 0.5 * D ** -0.5),
        'ada_b': nrm((L, 6 * D), 0.02),
        'norm_mix_g': gain((L, D)),
        'norm_ffn_g': gain((L, D)),
        'w_in': nrm((L, D, IN_W), D ** -0.5),
        'rwkv_mu': unif((L, GROUP_COLS[0]), 0.0, 1.0),
        'rwkv_w0': unif((L, 2, GROUP_W), -6.0, -1.0),
        'rwkv_w_up': nrm((L, 2, DECAY_LORA, GROUP_W), 0.1),
        'rwkv_a0': nrm((L, 2, GROUP_W), 0.1),
        'rwkv_a_up': nrm((L, 2, AAA_LORA, GROUP_W), AAA_LORA ** -0.5),
        'rwkv_g_up': nrm((L, GATE_LORA, GROUP_W), GATE_LORA ** -0.5),
        'rwkv_k_k': gain((L, GROUP_W), 0.85),
        'rwkv_k_a': gain((L, GROUP_W)),
        'rwkv_r_k': nrm((L, RWKV_HEADS, RWKV_HEAD), 0.1),
        'rwkv_lnx_w': gain((L, GROUP_W)),
        'rwkv_lnx_b': nrm((L, GROUP_W), 0.02),
        'diff_lq1': nrm((L, DIFF_HEAD), 0.1),
        'diff_lk1': nrm((L, DIFF_HEAD), 0.1),
        'diff_lq2': nrm((L, DIFF_HEAD), 0.1),
        'diff_lk2': nrm((L, DIFF_HEAD), 0.1),
        'diff_subln_g': gain((L, 2 * DIFF_HEAD)),
        'gqa_q_norm_g': gain((L, GQA_HEAD)),
        'gqa_k_norm_g': gain((L, GQA_HEAD)),
        'mla_q_norm_g': gain((L, Q_LORA)),
        'mla_q_up': nrm((L, Q_LORA, MLA_HEADS * (MLA_NOPE + MLA_ROPE)), Q_LORA ** -0.5),
        'mla_kv_norm_g': gain((L, KV_LORA)),
        'mla_kv_up': nrm((L, KV_LORA, MLA_HEADS * (MLA_NOPE + MLA_V)), KV_LORA ** -0.5),
        'w_out': nrm((L, MIX_W, D), MIX_W ** -0.5),
        'ffn_w_gate': nrm((N_DENSE, D, D_FF), D ** -0.5),
        'ffn_w_up': nrm((N_DENSE, D, D_FF), D ** -0.5),
        'ffn_w_down': nrm((N_DENSE, D_FF, D), D_FF ** -0.5),
        'moe_router': nrm((N_MOE, D, N_EXPERTS), D ** -0.5),
        'moe_w_gate': nrm((N_MOE, N_EXPERTS, D, D_FF_EXPERT), D ** -0.5),
        'moe_w_up': nrm((N_MOE, N_EXPERTS, D, D_FF_EXPERT), D ** -0.5),
        'moe_w_down': nrm((N_MOE, N_EXPERTS, D_FF_EXPERT, D), D_FF_EXPERT ** -0.5),
        'final_norm_g': gain((D,)),
    }


def reference(x, c, ctx, c_ctx, ada_w, ada_b, norm_mix_g, norm_ffn_g, w_in,
              rwkv_mu, rwkv_w0, rwkv_w_up, rwkv_a0, rwkv_a_up, rwkv_g_up, rwkv_k_k, rwkv_k_a, rwkv_r_k,
              rwkv_lnx_w, rwkv_lnx_b, diff_lq1, diff_lk1, diff_lq2, diff_lk2, diff_subln_g,
              gqa_q_norm_g, gqa_k_norm_g, mla_q_norm_g, mla_q_up, mla_kv_norm_g, mla_kv_up, w_out,
              ffn_w_gate, ffn_w_up, ffn_w_down, moe_router, moe_w_gate, moe_w_up, moe_w_down, final_norm_g):
    n_lat = x.shape[1]
    rope_diff = _axial_rope(n_lat, DIFF_HEAD)
    rope_gqa = _axial_rope(n_lat, GQA_HEAD)
    rope_mla = _axial_rope(n_lat, MLA_ROPE)
    silu_c = jax.nn.silu(c)
    silu_cc = jax.nn.silu(c_ctx)
    h_ctx = ctx
    for l in range(DEPTH):
        need_ctx = l < DEPTH - 1
        m_lat = _split((silu_c @ ada_w[l] + ada_b[l])[:, None, :], (D_MODEL,) * 6)
        m_ctx = _split(silu_cc @ ada_w[l] + ada_b[l], (D_MODEL,) * 6)

        a_lat = _modulate(_rmsnorm(x, norm_mix_g[l]), m_lat[0], m_lat[1])
        a_ctx = _modulate(_rmsnorm(h_ctx, norm_mix_g[l]), m_ctx[0], m_ctx[1])
        pa_l, pb_l, pc_l, pd_l = _split(a_lat @ w_in[l], GROUP_COLS)
        pa_c, pb_c, pc_c, pd_c = _split(a_ctx @ w_in[l], GROUP_COLS)
        oa_l, oa_c = _rwkv7_mixer(pa_l, pa_c, rwkv_mu[l], rwkv_w0[l], rwkv_w_up[l], rwkv_a0[l], rwkv_a_up[l],
                                  rwkv_g_up[l], rwkv_k_k[l], rwkv_k_a[l], rwkv_r_k[l], rwkv_lnx_w[l], rwkv_lnx_b[l],
                                  need_ctx)
        lambda_init = 0.8 - 0.6 * math.exp(-0.3 * l)
        ob_l, ob_c = _diff_attention(pb_l, pb_c, diff_lq1[l], diff_lk1[l], diff_lq2[l], diff_lk2[l],
                                     diff_subln_g[l], lambda_init, rope_diff[0], rope_diff[1], need_ctx)
        oc_l, oc_c = _gqa_attention(pc_l, pc_c, gqa_q_norm_g[l], gqa_k_norm_g[l], rope_gqa[0], rope_gqa[1], need_ctx)
        od_l, od_c = _mla_attention(pd_l, pd_c, mla_q_norm_g[l], mla_q_up[l], mla_kv_norm_g[l], mla_kv_up[l],
                                    rope_mla[0], rope_mla[1], need_ctx)
        x = x + m_lat[2] * (jnp.concatenate([oa_l, ob_l, oc_l, od_l], axis=-1) @ w_out[l])
        if need_ctx:
            h_ctx = h_ctx + m_ctx[2] * (jnp.concatenate([oa_c, ob_c, oc_c, od_c], axis=-1) @ w_out[l])

        j = l // 2
        if l % 2 == 0:
            ffn = lambda h, j=j: _swiglu(h, ffn_w_gate[j], ffn_w_up[j], ffn_w_down[j])
        else:
            ffn = lambda h, j=j: _moe(h, moe_router[j], moe_w_gate[j], moe_w_up[j], moe_w_down[j])
        x = x + m_lat[5] * ffn(_modulate(_rmsnorm(x, norm_ffn_g[l]), m_lat[3], m_lat[4]))
        if need_ctx:
            h_ctx = h_ctx + m_ctx[5] * ffn(_modulate(_rmsnorm(h_ctx, norm_ffn_g[l]), m_ctx[3], m_ctx[4]))
    return _rmsnorm(x, final_norm_g)
```

```python
import functools
import math

import jax
import jax.numpy as jnp
from jax import lax
from jax.experimental import pallas as pl
from jax.experimental.pallas import tpu as pltpu

F32 = jnp.float32
BF16 = jnp.bfloat16

GRID_W = 64
ROPE_THETA = 10000.0
NORM_EPS = 1e-6
SUBLN_EPS = 1e-5
GN_EPS = 64e-5
TOP_K = 2

LANES = 128
ROW_TILE = 256
RWKV_HEAD = 64
RWKV_CHUNK = 64
RWKV_QUAD = 4
MOE_ROWS = 512
VMEM_LIMIT = 56 * 1024 * 1024


def _cparams(sem):
    return pltpu.CompilerParams(dimension_semantics=sem, vmem_limit_bytes=VMEM_LIMIT)


def _pick(n, pref):
    if n <= pref:
        return n
    t = pref - pref % LANES
    while t >= LANES:
        if n % t == 0:
            return t
        t -= LANES
    return n


def _ada_kernel(c_ref, w_ref, b_ref, o_ref):
    c = c_ref[...]
    s = (c * jax.nn.sigmoid(c)).astype(BF16)
    o_ref[...] = jnp.dot(s, w_ref[...].astype(BF16), preferred_element_type=F32) + b_ref[...]


def _ada(cvec, ada_w, ada_b):
    n_layers, d, n = ada_w.shape
    tn = _pick(n, 512)
    return pl.pallas_call(
        _ada_kernel,
        grid=(n_layers, n // tn),
        in_specs=[
            pl.BlockSpec((8, d), lambda l, j: (0, 0)),
            pl.BlockSpec((None, d, tn), lambda l, j: (l, 0, j)),
            pl.BlockSpec((None, 1, tn), lambda l, j: (l, 0, j)),
        ],
        out_specs=pl.BlockSpec((None, 8, tn), lambda l, j: (l, 0, j)),
        out_shape=jax.ShapeDtypeStruct((n_layers, 8, n), F32),
        compiler_params=_cparams(("parallel", "parallel")),
        name="ada",
    )(cvec, ada_w, ada_b.reshape(n_layers, 1, n))


def _mod_sel(n_lat_tiles, n_batch):
    return lambda b, i: (jnp.where(i >= n_lat_tiles, n_batch, b), 0, 0)


def _norm_mod_kernel(x_ref, g_ref, sc_ref, sh_ref, o_ref):
    x = x_ref[...]
    y = x * lax.rsqrt(jnp.mean(x * x, axis=-1, keepdims=True) + NORM_EPS) * g_ref[...]
    o_ref[...] = (y * (1.0 + sc_ref[...]) + sh_ref[...]).astype(o_ref.dtype)


def _norm_mod(x, g, scale, shift, n_tiles, n_lat_tiles, out_dtype=BF16):
    nb, _, d = x.shape
    sel = _mod_sel(n_lat_tiles, nb)
    return pl.pallas_call(
        _norm_mod_kernel,
        grid=(nb, n_tiles),
        in_specs=[
            pl.BlockSpec((None, ROW_TILE, d), lambda b, i: (b, i, 0)),
            pl.BlockSpec((1, d), lambda b, i: (0, 0)),
            pl.BlockSpec((None, 1, d), sel),
            pl.BlockSpec((None, 1, d), sel),
        ],
        out_specs=pl.BlockSpec((None, ROW_TILE, d), lambda b, i: (b, i, 0)),
        out_shape=jax.ShapeDtypeStruct((nb, n_tiles * ROW_TILE, d), out_dtype),
        compiler_params=_cparams(("parallel", "parallel")),
        name="norm_mod",
    )(x, g.reshape(1, d), scale, shift)


def _norm_router_kernel(x_ref, g_ref, sc_ref, sh_ref, rh_ref, rl_ref, o_ref, lg_ref):
    x = x_ref[...]
    y = x * lax.rsqrt(jnp.mean(x * x, axis=-1, keepdims=True) + NORM_EPS) * g_ref[...]
    a = y * (1.0 + sc_ref[...]) + sh_ref[...]
    o_ref[...] = a
    ah = a.astype(BF16)
    al = (a - ah.astype(F32)).astype(BF16)
    rh = rh_ref[...]
    lg_ref[...] = (jnp.dot(ah, rh, preferred_element_type=F32)
                   + jnp.dot(al, rh, preferred_element_type=F32)
                   + jnp.dot(ah, rl_ref[...], preferred_element_type=F32))


def _norm_router(x, g, scale, shift, router, n_tiles, n_lat_tiles):
    nb, _, d = x.shape
    n_exp = router.shape[1]
    r_pad = jnp.pad(router, ((0, 0), (0, LANES - n_exp)))
    r_hi = r_pad.astype(BF16)
    r_lo = (r_pad - r_hi.astype(F32)).astype(BF16)
    sel = _mod_sel(n_lat_tiles, nb)
    rows = n_tiles * ROW_TILE
    return pl.pallas_call(
        _norm_router_kernel,
        grid=(nb, n_tiles),
        in_specs=[
            pl.BlockSpec((None, ROW_TILE, d), lambda b, i: (b, i, 0)),
            pl.BlockSpec((1, d), lambda b, i: (0, 0)),
            pl.BlockSpec((None, 1, d), sel),
            pl.BlockSpec((None, 1, d), sel),
            pl.BlockSpec((d, LANES), lambda b, i: (0, 0)),
            pl.BlockSpec((d, LANES), lambda b, i: (0, 0)),
        ],
        out_specs=[
            pl.BlockSpec((None, ROW_TILE, d), lambda b, i: (b, i, 0)),
            pl.BlockSpec((None, ROW_TILE, LANES), lambda b, i: (b, i, 0)),
        ],
        out_shape=[
            jax.ShapeDtypeStruct((nb, rows, d), F32),
            jax.ShapeDtypeStruct((nb, rows, LANES), F32),
        ],
        compiler_params=_cparams(("parallel", "parallel")),
        name="norm_router",
    )(x, g.reshape(1, d), scale, shift, r_hi, r_lo)


def _final_norm_kernel(x_ref, g_ref, o_ref):
    x = x_ref[...]
    o_ref[...] = x * lax.rsqrt(jnp.mean(x * x, axis=-1, keepdims=True) + NORM_EPS) * g_ref[...]


def _final_norm(x, g):
    nb, rows, d = x.shape
    return pl.pallas_call(
        _final_norm_kernel,
        grid=(nb, rows // ROW_TILE),
        in_specs=[
            pl.BlockSpec((None, ROW_TILE, d), lambda b, i: (b, i, 0)),
            pl.BlockSpec((1, d), lambda b, i: (0, 0)),
        ],
        out_specs=pl.BlockSpec((None, ROW_TILE, d), lambda b, i: (b, i, 0)),
        out_shape=jax.ShapeDtypeStruct((nb, rows, d), F32),
        compiler_params=_cparams(("parallel", "parallel")),
        name="final_norm",
    )(x, g.reshape(1, d))


def _mm_kernel(a_ref, w_ref, o_ref):
    o_ref[...] = jnp.dot(a_ref[...], w_ref[...], preferred_element_type=F32).astype(o_ref.dtype)


def _matmul(a, w, out_dtype=F32, tn_pref=1024):
    nb, rows, k = a.shape
    n = w.shape[1]
    tn = _pick(n, tn_pref)
    return pl.pallas_call(
        _mm_kernel,
        grid=(n // tn, nb, rows // ROW_TILE),
        in_specs=[
            pl.BlockSpec((None, ROW_TILE, k), lambda j, b, i: (b, i, 0)),
            pl.BlockSpec((k, tn), lambda j, b, i: (0, j)),
        ],
        out_specs=pl.BlockSpec((None, ROW_TILE, tn), lambda j, b, i: (b, i, j)),
        out_shape=jax.ShapeDtypeStruct((nb, rows, n), out_dtype),
        compiler_params=_cparams(("parallel", "parallel", "parallel")),
        name="matmul",
    )(a, w)


def _mm_swiglu_kernel(a_ref, wg_ref, wu_ref, o_ref):
    a = a_ref[...]
    g = jnp.dot(a, wg_ref[...], preferred_element_type=F32)
    u = jnp.dot(a, wu_ref[...], preferred_element_type=F32)
    o_ref[...] = (g * jax.nn.sigmoid(g) * u).astype(o_ref.dtype)


def _matmul_swiglu(a, wg, wu, tn_pref=512):
    nb, rows, k = a.shape
    n = wg.shape[1]
    tn = _pick(n, tn_pref)
    return pl.pallas_call(
        _mm_swiglu_kernel,
        grid=(n // tn, nb, rows // ROW_TILE),
        in_specs=[
            pl.BlockSpec((None, ROW_TILE, k), lambda j, b, i: (b, i, 0)),
            pl.BlockSpec((k, tn), lambda j, b, i: (0, j)),
            pl.BlockSpec((k, tn), lambda j, b, i: (0, j)),
        ],
        out_specs=pl.BlockSpec((None, ROW_TILE, tn), lambda j, b, i: (b, i, j)),
        out_shape=jax.ShapeDtypeStruct((nb, rows, n), BF16),
        compiler_params=_cparams(("parallel", "parallel", "parallel")),
        name="matmul_swiglu",
    )(a, wg, wu)


def _mm_resid_kernel(a_ref, w_ref, res_ref, gate_ref, o_ref):
    acc = jnp.dot(a_ref[...], w_ref[...], preferred_element_type=F32)
    o_ref[...] = res_ref[...] + gate_ref[...] * acc


def _matmul_resid(a, w, res, gate, n_lat_tiles, tn_pref=512):
    nb, rows, k = a.shape
    n = w.shape[1]
    tn = _pick(n, tn_pref)
    return pl.pallas_call(
        _mm_resid_kernel,
        grid=(n // tn, nb, rows // ROW_TILE),
        in_specs=[
            pl.BlockSpec((None, ROW_TILE, k), lambda j, b, i: (b, i, 0)),
            pl.BlockSpec((k, tn), lambda j, b, i: (0, j)),
            pl.BlockSpec((None, ROW_TILE, tn), lambda j, b, i: (b, i, j)),
            pl.BlockSpec((None, 1, tn), lambda j, b, i: (jnp.where(i >= n_lat_tiles, nb, b), 0, j)),
        ],
        out_specs=pl.BlockSpec((None, ROW_TILE, tn), lambda j, b, i: (b, i, j)),
        out_shape=jax.ShapeDtypeStruct((nb, rows, n), F32),
        compiler_params=_cparams(("parallel", "parallel", "parallel")),
        name="matmul_resid",
    )(a, w, res, gate)


def _attn_kernel(q_ref, kt_ref, v_ref, o_ref):
    s = jnp.dot(q_ref[...], kt_ref[...], preferred_element_type=F32)
    p = jnp.exp(s - jnp.max(s, axis=-1, keepdims=True))
    l = jnp.sum(p, axis=-1, keepdims=True)
    o = jnp.dot(p.astype(BF16), v_ref[...], preferred_element_type=F32)
    o_ref[...] = (o / l).astype(o_ref.dtype)


def _attention(q, kt, v, n_kv_heads, group, dk, dv, tq=256):
    nb, t_q, _ = q.shape
    s_len = v.shape[1]
    tq = min(tq, t_q)
    return pl.pallas_call(
        _attn_kernel,
        grid=(nb, n_kv_heads, group, t_q // tq),
        in_specs=[
            pl.BlockSpec((None, tq, dk), lambda b, h, g, i: (b, i, h * group + g)),
            pl.BlockSpec((None, dk, s_len), lambda b, h, g, i: (b, h, 0)),
            pl.BlockSpec((None, s_len, dv), lambda b, h, g, i: (b, 0, h)),
        ],
        out_specs=pl.BlockSpec((None, tq, dv), lambda b, h, g, i: (b, i, h * group + g)),
        out_shape=jax.ShapeDtypeStruct((nb, t_q, n_kv_heads * group * dv), BF16),
        compiler_params=_cparams(("parallel", "parallel", "parallel", "parallel")),
        name="attention",
    )(q, kt, v)


def _diff_attn_kernel(lam_ref, q_ref, kt_ref, v_ref, g_ref, o_ref, *, out_scale):
    q = q_ref[...]
    kt = kt_ref[...]
    v = v_ref[...]
    half = q.shape[-1] // 2

    def one_map(qm, ktm):
        s = jnp.dot(qm, ktm, preferred_element_type=F32)
        p = jnp.exp(s - jnp.max(s, axis=-1, keepdims=True))
        l = jnp.sum(p, axis=-1, keepdims=True)
        return jnp.dot(p.astype(BF16), v, preferred_element_type=F32) / l

    o = one_map(q[:, :half], kt[:half]) - lam_ref[0] * one_map(q[:, half:], kt[half:])
    o = o * lax.rsqrt(jnp.mean(o * o, axis=-1, keepdims=True) + SUBLN_EPS) * g_ref[...]
    o_ref[...] = (o * out_scale).astype(o_ref.dtype)


def _diff_attention(q, kt, v, lam, subln_g, n_heads, out_scale, tq=256):
    nb, t_q, w = q.shape
    dh = w // n_heads
    s_len = v.shape[1]
    tq = min(tq, t_q)
    return pl.pallas_call(
        functools.partial(_diff_attn_kernel, out_scale=out_scale),
        grid=(nb, n_heads, t_q // tq),
        in_specs=[
            pl.BlockSpec(memory_space=pltpu.SMEM),
            pl.BlockSpec((None, tq, dh), lambda b, h, i: (b, i, h)),
            pl.BlockSpec((None, dh, s_len), lambda b, h, i: (b, h, 0)),
            pl.BlockSpec((None, s_len, dh), lambda b, h, i: (b, 0, h)),
            pl.BlockSpec((1, dh), lambda b, h, i: (0, 0)),
        ],
        out_specs=pl.BlockSpec((None, tq, dh), lambda b, h, i: (b, i, h)),
        out_shape=jax.ShapeDtypeStruct((nb, t_q, w), BF16),
        compiler_params=_cparams(("parallel", "parallel", "parallel")),
        name="diff_attention",
    )(lam.reshape(1).astype(F32), q, kt, v, subln_g.reshape(1, dh))


def _split3(x):
    h = x.astype(BF16)
    r = x - h.astype(F32)
    m = r.astype(BF16)
    return h, m, (r - m.astype(F32)).astype(BF16)


def _rwkv_chunk(r, v, kk, wl, kd, bb, h_state, rev):
    c, w = r.shape
    nh = w // RWKV_HEAD
    ri = lax.broadcasted_iota(jnp.int32, (c, c), 0)
    ci = lax.broadcasted_iota(jnp.int32, (c, c), 1)
    before_incl = (ci >= ri) if rev else (ci <= ri)
    tri = jnp.where(before_incl, 1.0, 0.0).astype(BF16)
    w_h, w_m, w_l = _split3(wl)
    cum = (jnp.dot(tri, w_h, preferred_element_type=F32) + jnp.dot(tri, w_m, preferred_element_type=F32)
           + jnp.dot(tri, w_l, preferred_element_type=F32))
    cum_prev = cum - wl
    last = 0 if rev else c - 1
    total = cum[last:last + 1, :]
    mid = cum[c // 2:c // 2 + 1, :]
    a = -kk
    e_pm = jnp.exp(cum_prev - mid)
    e_cm = jnp.exp(cum - mid)
    e_mc = jnp.exp(mid - cum)
    e_end = jnp.exp(total - cum)

    lane_head = lax.broadcasted_iota(jnp.int32, (c, w), 1) // RWKV_HEAD

    def stack(x):
        return jnp.concatenate([jnp.where(lane_head == h, x, 0.0) for h in range(nh)], axis=0)

    def unstack(x):
        out = x[0:c]
        for h in range(1, nh):
            out = out + x[h * c:(h + 1) * c]
        return out

    l_a, l_r = stack(a * e_pm), stack(r * e_cm)
    r_b, r_k = stack(bb * e_mc), stack(kd * e_mc)
    v_st = stack(v)
    n = nh * c
    rn = lax.broadcasted_iota(jnp.int32, (n, n), 0)
    cn = lax.broadcasted_iota(jnp.int32, (n, n), 1)
    tr, ts = rn % c, cn % c
    strict = (ts > tr) if rev else (ts < tr)
    incl = (ts >= tr) if rev else (ts <= tr)
    nt = (((1,), (1,)), ((), ()))
    tn = (((0,), (0,)), ((), ()))

    def dot_nt(x, y):
        return lax.dot_general(x.astype(BF16), y.astype(BF16), nt, preferred_element_type=F32)

    def dot_tn(x, y):
        return lax.dot_general(x.astype(BF16), y.astype(BF16), tn, preferred_element_type=F32)

    def dot(x, y):
        return jnp.dot(x.astype(BF16), y.astype(BF16), preferred_element_type=F32)

    n_ab = jnp.where(strict, dot_nt(l_a, r_b), 0.0)
    a_ak = jnp.where(strict, dot_nt(l_a, r_k), 0.0)
    a_rb = jnp.where(incl, dot_nt(l_r, r_b), 0.0)
    a_rk = jnp.where(incl, dot_nt(l_r, r_k), 0.0)

    eye = jnp.where(rn == cn, 1.0, 0.0)
    t_inv = eye + n_ab
    pw = n_ab
    for _ in range(int(math.log2(c)) - 1):
        pw = dot(pw, pw)
        t_inv = t_inv + dot(pw, t_inv)

    l_a0 = stack(a * jnp.exp(cum_prev))
    l_r0 = stack(r * jnp.exp(cum))
    b_end, k_end = stack(bb * e_end), stack(kd * e_end)
    u0 = dot(t_inv, dot(a_ak, v_st))
    ta = dot(t_inv, l_a0)
    y0 = unstack(dot(a_rb, u0) + dot(a_rk, v_st))
    ra = unstack(l_r0 + dot(a_rb, ta))
    rw = lax.broadcasted_iota(jnp.int32, (w, w), 0)
    cw = lax.broadcasted_iota(jnp.int32, (w, w), 1)
    p = jnp.where(rw == cw, jnp.broadcast_to(jnp.exp(total), (w, w)), 0.0) + dot_tn(b_end, ta)
    q = dot_tn(b_end, u0) + dot_tn(k_end, v_st)
    y = y0 + dot(ra, h_state)
    return y, dot(p, h_state) + q


def _rwkv_scan_kernel(rf_ref, vf_ref, kkf_ref, wlf_ref, kdf_ref, bbf_ref,
                      rr_ref, vr_ref, kkr_ref, wlr_ref, kdr_ref, bbr_ref,
                      yf_ref, yr_ref, hf_ref, hr_ref):
    @pl.when(pl.program_id(2) == 0)
    def _():
        hf_ref[...] = jnp.zeros_like(hf_ref)
        hr_ref[...] = jnp.zeros_like(hr_ref)

    y, h = _rwkv_chunk(rf_ref[...], vf_ref[...], kkf_ref[...], wlf_ref[...], kdf_ref[...], bbf_ref[...],
                       hf_ref[...], rev=False)
    yf_ref[...] = y
    hf_ref[...] = h
    y, h = _rwkv_chunk(rr_ref[...], vr_ref[...], kkr_ref[...], wlr_ref[...], kdr_ref[...], bbr_ref[...],
                       hr_ref[...], rev=True)
    yr_ref[...] = y
    hr_ref[...] = h


def _rwkv_scan(r, v, kk, wl, kd, bb, n_lat):
    nb, s_len, w = r.shape
    c = RWKV_CHUNK
    qw = RWKV_QUAD * RWKV_HEAD
    n_l, n_all = n_lat // c, s_len // c
    n_c = n_all - n_l

    def fwd(s):
        return jnp.where(s < n_c, n_l + s, s - n_c)

    def rev(s):
        return n_all - 1 - s

    def shared(order):
        return pl.BlockSpec((None, c, qw), lambda b, q, s: (b, order(s), q))

    def per_dir(d, order):
        return pl.BlockSpec((None, None, c, qw), lambda b, q, s: (d, b, order(s), q))

    yf, yr = pl.pallas_call(
        _rwkv_scan_kernel,
        grid=(nb, w // qw, n_all),
        in_specs=[shared(fwd), shared(fwd), shared(fwd), per_dir(0, fwd), per_dir(0, fwd), per_dir(0, fwd),
                  shared(rev), shared(rev), shared(rev), per_dir(1, rev), per_dir(1, rev), per_dir(1, rev)],
        out_specs=[shared(fwd), shared(rev)],
        out_shape=[jax.ShapeDtypeStruct((nb, s_len, w), F32)] * 2,
        scratch_shapes=[pltpu.VMEM((qw, qw), F32), pltpu.VMEM((qw, qw), F32)],
        compiler_params=_cparams(("parallel", "parallel", "arbitrary")),
        name="rwkv_scan",
    )(r, v, kk, wl, kd, bb, r, v, kk, wl, kd, bb)
    return yf, yr


def _moe_gather_kernel(src_ref, tok_ref, o_ref, buf_ref, sem_ref):
    base = pl.program_id(0) * MOE_ROWS

    def copy(r):
        return pltpu.make_async_copy(tok_ref.at[pl.ds(src_ref[base + r], 1), :], buf_ref.at[pl.ds(r, 1), :],
                                     sem_ref.at[0])

    def start(r, carry):
        copy(r).start()
        return carry

    def wait(r, carry):
        copy(r).wait()
        return carry

    lax.fori_loop(0, MOE_ROWS, start, 0)
    lax.fori_loop(0, MOE_ROWS, wait, 0)
    o_ref[...] = buf_ref[...].astype(o_ref.dtype)


def _moe_gather(tok, src, n_blocks):
    d = tok.shape[1]
    return pl.pallas_call(
        _moe_gather_kernel,
        grid_spec=pltpu.PrefetchScalarGridSpec(
            num_scalar_prefetch=1,
            grid=(n_blocks,),
            in_specs=[pl.BlockSpec(memory_space=pl.ANY)],
            out_specs=pl.BlockSpec((MOE_ROWS, d), lambda i, src: (i, 0)),
            scratch_shapes=[pltpu.VMEM((MOE_ROWS, d), F32), pltpu.SemaphoreType.DMA((1,))],
        ),
        out_shape=jax.ShapeDtypeStruct((n_blocks * MOE_ROWS, d), BF16),
        compiler_params=_cparams(("arbitrary",)),
        name="moe_gather",
    )(src, tok)


def _moe_up_kernel(be_ref, a_ref, wg_ref, wu_ref, o_ref):
    a = a_ref[...]
    g = jnp.dot(a, wg_ref[...], preferred_element_type=F32)
    u = jnp.dot(a, wu_ref[...], preferred_element_type=F32)
    o_ref[...] = (g * jax.nn.sigmoid(g) * u).astype(o_ref.dtype)


def _moe_up(buf, block_expert, wg, wu, tn_pref=512):
    rows, d = buf.shape
    f = wg.shape[2]
    tn = _pick(f, tn_pref)
    n_blocks = rows // MOE_ROWS
    return pl.pallas_call(
        _moe_up_kernel,
        grid_spec=pltpu.PrefetchScalarGridSpec(
            num_scalar_prefetch=1,
            grid=(n_blocks, f // tn),
            in_specs=[
                pl.BlockSpec((MOE_ROWS, d), lambda i, j, be: (i, 0)),
                pl.BlockSpec((None, d, tn), lambda i, j, be: (be[i], 0, j)),
                pl.BlockSpec((None, d, tn), lambda i, j, be: (be[i], 0, j)),
            ],
            out_specs=pl.BlockSpec((MOE_ROWS, tn), lambda i, j, be: (i, j)),
        ),
        out_shape=jax.ShapeDtypeStruct((rows, f), BF16),
        compiler_params=_cparams(("parallel", "parallel")),
        name="moe_up",
    )(block_expert, buf, wg, wu)


def _moe_down_kernel(be_ref, h_ref, w_ref, o_ref):
    o_ref[...] = jnp.dot(h_ref[...], w_ref[...], preferred_element_type=F32)


def _moe_down(hid, block_expert, wd, tn_pref=512):
    rows, f = hid.shape
    d = wd.shape[2]
    tn = _pick(d, tn_pref)
    n_blocks = rows // MOE_ROWS
    return pl.pallas_call(
        _moe_down_kernel,
        grid_spec=pltpu.PrefetchScalarGridSpec(
            num_scalar_prefetch=1,
            grid=(n_blocks, d // tn),
            in_specs=[
                pl.BlockSpec((MOE_ROWS, f), lambda i, j, be: (i, 0)),
                pl.BlockSpec((None, f, tn), lambda i, j, be: (be[i], 0, j)),
            ],
            out_specs=pl.BlockSpec((MOE_ROWS, tn), lambda i, j, be: (i, j)),
        ),
        out_shape=jax.ShapeDtypeStruct((rows, d), F32),
        compiler_params=_cparams(("parallel", "parallel")),
        name="moe_down",
    )(block_expert, hid, wd)


def _moe_combine_kernel(dest_ref, y_ref, x_ref, gate_ref, g0_ref, g1_ref, o_ref, b0_ref, b1_ref, sem_ref):
    base = (pl.program_id(0) * pl.num_programs(1) + pl.program_id(1)) * ROW_TILE

    def copies(r):
        t = (base + r) * TOP_K
        return (pltpu.make_async_copy(y_ref.at[pl.ds(dest_ref[t], 1), :], b0_ref.at[pl.ds(r, 1), :], sem_ref.at[0]),
                pltpu.make_async_copy(y_ref.at[pl.ds(dest_ref[t + 1], 1), :], b1_ref.at[pl.ds(r, 1), :],
                                      sem_ref.at[1]))

    def start(r, carry):
        c0, c1 = copies(r)
        c0.start()
        c1.start()
        return carry

    def wait(r, carry):
        c0, c1 = copies(r)
        c0.wait()
        c1.wait()
        return carry

    lax.fori_loop(0, ROW_TILE, start, 0)
    lax.fori_loop(0, ROW_TILE, wait, 0)
    o_ref[...] = x_ref[...] + gate_ref[...] * (g0_ref[...] * b0_ref[...] + g1_ref[...] * b1_ref[...])


def _moe_combine(y, dest, x, gate, g0, g1):
    nb, t, d = x.shape
    return pl.pallas_call(
        _moe_combine_kernel,
        grid_spec=pltpu.PrefetchScalarGridSpec(
            num_scalar_prefetch=1,
            grid=(nb, t // ROW_TILE),
            in_specs=[
                pl.BlockSpec(memory_space=pl.ANY),
                pl.BlockSpec((None, ROW_TILE, d), lambda b, i, dest: (b, i, 0)),
                pl.BlockSpec((None, 1, d), lambda b, i, dest: (b, 0, 0)),
                pl.BlockSpec((None, ROW_TILE, 1), lambda b, i, dest: (b, i, 0)),
                pl.BlockSpec((None, ROW_TILE, 1), lambda b, i, dest: (b, i, 0)),
            ],
            out_specs=pl.BlockSpec((None, ROW_TILE, d), lambda b, i, dest: (b, i, 0)),
            scratch_shapes=[pltpu.VMEM((ROW_TILE, d), F32), pltpu.VMEM((ROW_TILE, d), F32),
                            pltpu.SemaphoreType.DMA((2,))],
        ),
        out_shape=jax.ShapeDtypeStruct((nb, t, d), F32),
        compiler_params=_cparams(("arbitrary", "arbitrary")),
        name="moe_combine",
    )(dest, y, x, gate, g0, g1)


def _moe_layer(a_tok, logits, x, gate, wg, wu, wd):
    nb, t, d = a_tok.shape
    n_exp = wg.shape[0]
    n_tok = nb * t
    n_assign = n_tok * TOP_K
    top_logit, top_idx = lax.top_k(logits.reshape(n_tok, -1)[:, :n_exp], TOP_K)
    gates = jax.nn.softmax(top_logit, axis=-1)
    flat_e = top_idx.reshape(-1)
    onehot = (flat_e[:, None] == jnp.arange(n_exp)[None, :]).astype(jnp.int32)
    rank = jnp.take_along_axis(jnp.cumsum(onehot, axis=0) - onehot, flat_e[:, None], axis=1)[:, 0]
    counts = jnp.sum(onehot, axis=0)
    padded = (counts + MOE_ROWS - 1) // MOE_ROWS * MOE_ROWS
    pad_end = jnp.cumsum(padded)
    pad_start = pad_end - padded
    dest = (pad_start[flat_e] + rank).astype(jnp.int32)
    n_blocks = -(-n_assign // MOE_ROWS) + n_exp
    src = jnp.zeros((n_blocks * MOE_ROWS,), jnp.int32).at[dest].set(jnp.arange(n_assign, dtype=jnp.int32) // TOP_K)
    block_expert = jnp.minimum(
        jnp.searchsorted(pad_end, jnp.arange(n_blocks) * MOE_ROWS, side='right'), n_exp - 1).astype(jnp.int32)
    buf = _moe_gather(a_tok.reshape(n_tok, d), src, n_blocks)
    hid = _moe_up(buf, block_expert, wg, wu)
    y = _moe_down(hid, block_expert, wd)
    return _moe_combine(y, dest, x, gate, gates[:, 0].reshape(nb, t, 1), gates[:, 1].reshape(nb, t, 1))


def _rope_tables(n_lat, n_ctx, rot_dim):
    rows = n_lat // GRID_W
    row = jnp.repeat(jnp.arange(rows), GRID_W).astype(F32)
    col = jnp.tile(jnp.arange(GRID_W), rows).astype(F32)
    n_freq = rot_dim // 4
    inv = ROPE_THETA ** (-jnp.arange(n_freq, dtype=F32) / n_freq)
    ang = jnp.concatenate([row[:, None] * inv, col[:, None] * inv], axis=-1)
    ang = jnp.concatenate([ang, jnp.zeros((n_ctx, rot_dim // 2), F32)], axis=0)
    return jnp.cos(ang), jnp.sin(ang)


def _rope(x, cs):
    cos, sin = cs
    half = x.shape[-1] // 2
    x1, x2 = x[..., :half], x[..., half:]
    c, s = cos[None, :, None, :], sin[None, :, None, :]
    return jnp.concatenate([x1 * c - x2 * s, x1 * s + x2 * c], axis=-1)


def _rms(x, g, eps=NORM_EPS):
    return x * lax.rsqrt(jnp.mean(x * x, axis=-1, keepdims=True) + eps) * g


def _seq_shift(p, mu, n_lat):
    def one(z):
        zero = jnp.zeros_like(z[:, :1])
        prev = jnp.concatenate([zero, z[:, :-1]], axis=1)
        nxt = jnp.concatenate([z[:, 1:], zero], axis=1)
        return z + mu * (0.5 * (prev + nxt) - z)
    return jnp.concatenate([one(p[:, :n_lat]), one(p[:, n_lat:])], axis=1)


def _cat_rows(lat, ctx):
    return lat if ctx is None else jnp.concatenate([lat, ctx], axis=1)


def _rwkv_mixer(pa, n_lat, need_ctx, mu, w0, w_up, a0, a_up, g_up, k_k, k_a, r_k, lnx_w, lnx_b):
    nb, s_len, _ = pa.shape
    gw = k_k.shape[0]
    nh = gw // RWKV_HEAD
    lora_w, lora_a = w_up.shape[1], a_up.shape[1]
    ps = _seq_shift(pa, mu, n_lat)
    r, k, v = ps[..., :gw], ps[..., gw:2 * gw], ps[..., 2 * gw:3 * gw]
    o = 3 * gw
    wd, ad, gd = ps[..., o:o + lora_w], ps[..., o + lora_w:o + lora_w + lora_a], ps[..., o + lora_w + lora_a:]
    hd = lambda z: z.reshape(nb, s_len, nh, RWKV_HEAD)
    kk = hd(k * k_k)
    kk = (kk / jnp.maximum(jnp.sqrt(jnp.sum(kk * kk, axis=-1, keepdims=True)), 1e-12)).reshape(nb, s_len, gw)
    tanh_wd = jnp.tanh(wd).astype(BF16)
    ad_b = ad.astype(BF16)
    wls, kds, bbs = [], [], []
    bonus = 0.0
    for d in range(2):
        w_log = -jax.nn.softplus(-(w0[d] + _matmul(tanh_wd, w_up[d].astype(BF16)))) - 0.5
        a_sig = jax.nn.sigmoid(a0[d] + _matmul(ad_b, a_up[d].astype(BF16)))
        kd = k * (1.0 + (a_sig - 1.0) * k_a)
        wls.append(-jnp.exp(w_log))
        kds.append(kd)
        bbs.append(kk * a_sig)
        bonus = bonus + jnp.sum(hd(r * kd) * r_k, axis=-1, keepdims=True) * hd(v)
    yf, yr = _rwkv_scan(r, v, kk, jnp.stack(wls), jnp.stack(kds), jnp.stack(bbs), n_lat)
    y = hd(yf + yr)
    rows = s_len if need_ctx else n_lat
    y, bonus, gd = y[:, :rows], bonus[:, :rows], gd[:, :rows]
    mean = jnp.mean(y, axis=-1, keepdims=True)
    var = jnp.mean(jnp.square(y - mean), axis=-1, keepdims=True)
    yn = ((y - mean) * lax.rsqrt(var + GN_EPS)).reshape(nb, rows, gw) * lnx_w + lnx_b
    lora_g = g_up.shape[0]
    pad_g = -lora_g % LANES
    sg = jnp.pad(jax.nn.sigmoid(gd), ((0, 0), (0, 0), (0, pad_g))).astype(BF16)
    g = _matmul(sg, jnp.pad(g_up, ((0, pad_g), (0, 0))).astype(BF16))
    return ((yn + bonus.reshape(nb, rows, gw)) * g).astype(BF16)


def _diff_mixer(pb, n_lat, need_ctx, lq1, lk1, lq2, lk2, subln_g, lambda_init, rope_cs):
    nb, s_len, w3 = pb.shape
    gw = w3 // 3
    dh = subln_g.shape[0] // 2
    nh = gw // (2 * dh)
    lam = jnp.exp(jnp.sum(lq1 * lk1)) - jnp.exp(jnp.sum(lq2 * lk2)) + lambda_init
    q, k, v = pb[..., :gw], pb[..., gw:2 * gw], pb[..., 2 * gw:]
    q = _rope(q.reshape(nb, s_len, nh * 2, dh), rope_cs) * dh ** -0.5
    k = _rope(k.reshape(nb, s_len, nh * 2, dh), rope_cs)
    q = q.reshape(nb, s_len, gw).astype(BF16)
    kt = jnp.swapaxes(k.reshape(nb, s_len, gw).astype(BF16), 1, 2)
    v = v.astype(BF16)
    scale = 1.0 - lambda_init
    out = _diff_attention(q[:, :n_lat], kt, v, lam, subln_g, nh, scale)
    if need_ctx:
        out = _cat_rows(out, _diff_attention(q[:, n_lat:], kt[:, :, n_lat:], v[:, n_lat:], lam, subln_g, nh, scale))
    return out


def _gqa_mixer(pc, n_lat, need_ctx, q_norm_g, k_norm_g, n_heads, n_kv, rope_cs):
    nb, s_len, _ = pc.shape
    dh = q_norm_g.shape[0]
    q, k, v = pc[..., :n_heads * dh], pc[..., n_heads * dh:(n_heads + n_kv) * dh], pc[..., (n_heads + n_kv) * dh:]
    q = _rope(_rms(q.reshape(nb, s_len, n_heads, dh), q_norm_g), rope_cs) * dh ** -0.5
    k = _rope(_rms(k.reshape(nb, s_len, n_kv, dh), k_norm_g), rope_cs)
    q = q.reshape(nb, s_len, n_heads * dh).astype(BF16)
    kt = jnp.swapaxes(k.reshape(nb, s_len, n_kv * dh).astype(BF16), 1, 2)
    v = v.astype(BF16)
    group = n_heads // n_kv
    out = _attention(q[:, :n_lat], kt, v, n_kv, group, dh, dh)
    if need_ctx:
        out = _cat_rows(out, _attention(q[:, n_lat:], kt[:, :, n_lat:], v[:, n_lat:], n_kv, group, dh, dh))
    return out


def _mla_mixer(pd, n_lat, need_ctx, q_norm_g, q_up, kv_norm_g, kv_up, n_heads, d_rope, rope_cs):
    nb, s_len, _ = pd.shape
    q_lora, kv_lora = q_norm_g.shape[0], kv_norm_g.shape[0]
    d_qk = q_up.shape[1] // n_heads
    d_nope = d_qk - d_rope
    d_v = kv_up.shape[1] // n_heads - d_nope
    c_q, c_kv, k_rope = pd[..., :q_lora], pd[..., q_lora:q_lora + kv_lora], pd[..., q_lora + kv_lora:
                                                                                 q_lora + kv_lora + d_rope]
    q = _matmul(_rms(c_q, q_norm_g).astype(BF16), q_up.astype(BF16)).reshape(nb, s_len, n_heads, d_qk)
    kv = _matmul(_rms(c_kv, kv_norm_g).astype(BF16), kv_up.astype(BF16)).reshape(nb, s_len, n_heads, d_nope + d_v)
    q_rope = _rope(q[..., d_nope:], rope_cs)
    k_rope = jnp.broadcast_to(_rope(k_rope[:, :, None, :], rope_cs), (nb, s_len, n_heads, d_rope))
    d_pad = -d_qk % LANES
    zeros = jnp.zeros((nb, s_len, n_heads, d_pad), F32)
    q = jnp.concatenate([q[..., :d_nope], q_rope, zeros], axis=-1) * d_qk ** -0.5
    k = jnp.concatenate([kv[..., :d_nope], k_rope, zeros], axis=-1)
    dk = d_qk + d_pad
    q = q.reshape(nb, s_len, n_heads * dk).astype(BF16)
    kt = jnp.swapaxes(k.reshape(nb, s_len, n_heads * dk).astype(BF16), 1, 2)
    v = kv[..., d_nope:].reshape(nb, s_len, n_heads * d_v).astype(BF16)
    out = _attention(q[:, :n_lat], kt, v, n_heads, 1, dk, d_v)
    if need_ctx:
        out = _cat_rows(out, _attention(q[:, n_lat:], kt[:, :, n_lat:], v[:, n_lat:], n_heads, 1, dk, d_v))
    return out


def _pad_cols(w, mult):
    return jnp.pad(w, ((0, 0), (0, -w.shape[1] % mult)))


def kernel(x, c, ctx, c_ctx, ada_w, ada_b, norm_mix_g, norm_ffn_g, w_in, rwkv_mu, rwkv_w0, rwkv_w_up, rwkv_a0, rwkv_a_up, rwkv_g_up, rwkv_k_k, rwkv_k_a, rwkv_r_k, rwkv_lnx_w, rwkv_lnx_b, diff_lq1, diff_lk1, diff_lq2, diff_lk2, diff_subln_g, gqa_q_norm_g, gqa_k_norm_g, mla_q_norm_g, mla_q_up, mla_kv_norm_g, mla_kv_up, w_out, ffn_w_gate, ffn_w_up, ffn_w_down, moe_router, moe_w_gate, moe_w_up, moe_w_down, final_norm_g):
    nb, n_lat, d = x.shape
    n_ctx = ctx.shape[1]
    depth = ada_w.shape[0]
    assert n_ctx == ROW_TILE and n_lat % ROW_TILE == 0 and n_lat % GRID_W == 0
    n_lat_tiles = n_lat // ROW_TILE
    n_all_tiles = n_lat_tiles + 1

    gw = rwkv_k_k.shape[1]
    cols_a = 3 * gw + rwkv_w_up.shape[2] + rwkv_a_up.shape[2] + rwkv_g_up.shape[1]
    cols_b = 3 * gw
    gqa_dh = gqa_q_norm_g.shape[1]
    mla_rope = 64
    mla_heads = 8
    gqa_heads = gw // gqa_dh
    cols_d = mla_q_norm_g.shape[1] + mla_kv_norm_g.shape[1] + mla_rope
    cols_c = w_in.shape[2] - cols_a - cols_b - cols_d
    gqa_kv = (cols_c - gw) // (2 * gqa_dh)
    offs = (0, cols_a, cols_a + cols_b, cols_a + cols_b + cols_c, w_in.shape[2])

    rope_diff = _rope_tables(n_lat, n_ctx, diff_subln_g.shape[1] // 2)
    rope_gqa = _rope_tables(n_lat, n_ctx, gqa_dh)
    rope_mla = _rope_tables(n_lat, n_ctx, mla_rope)

    cvec = jnp.concatenate([c, c_ctx[None, :], jnp.zeros((8 - nb - 1, d), F32)], axis=0)
    mods = _ada(cvec, ada_w, ada_b)[:, :nb + 1].reshape(depth, nb + 1, 6, 1, d)

    h = jnp.concatenate([x, ctx], axis=1)
    for l in range(depth):
        need_ctx = l < depth - 1
        m = [mods[l, :, i] for i in range(6)]
        n_out_tiles = n_all_tiles if need_ctx else n_lat_tiles

        a = _norm_mod(h, norm_mix_g[l], m[1], m[0], n_all_tiles, n_lat_tiles)
        groups = [_matmul(a, _pad_cols(w_in[l][:, offs[i]:offs[i + 1]], 768).astype(BF16), F32, 768)
                  for i in range(4)]
        oa = _rwkv_mixer(groups[0][..., :cols_a], n_lat, need_ctx, rwkv_mu[l], rwkv_w0[l], rwkv_w_up[l],
                         rwkv_a0[l], rwkv_a_up[l], rwkv_g_up[l], rwkv_k_k[l], rwkv_k_a[l], rwkv_r_k[l],
                         rwkv_lnx_w[l], rwkv_lnx_b[l])
        lambda_init = 0.8 - 0.6 * math.exp(-0.3 * l)
        ob = _diff_mixer(groups[1], n_lat, need_ctx, diff_lq1[l], diff_lk1[l], diff_lq2[l], diff_lk2[l],
                         diff_subln_g[l], lambda_init, rope_diff)
        oc = _gqa_mixer(groups[2], n_lat, need_ctx, gqa_q_norm_g[l], gqa_k_norm_g[l], gqa_heads, gqa_kv, rope_gqa)
        od = _mla_mixer(groups[3], n_lat, need_ctx, mla_q_norm_g[l], mla_q_up[l], mla_kv_norm_g[l], mla_kv_up[l],
                        mla_heads, mla_rope, rope_mla)
        mix = jnp.concatenate([oa, ob, oc, od], axis=-1)
        h = _matmul_resid(mix, w_out[l].astype(BF16), h, m[2], n_lat_tiles)

        j = l // 2
        if l % 2 == 0:
            a = _norm_mod(h, norm_ffn_g[l], m[4], m[3], n_out_tiles, n_lat_tiles)
            hid = _matmul_swiglu(a, ffn_w_gate[j].astype(BF16), ffn_w_up[j].astype(BF16))
            h = _matmul_resid(hid, ffn_w_down[j].astype(BF16), h, m[5], n_lat_tiles)
        else:
            assert not need_ctx
            a, logits = _norm_router(h, norm_ffn_g[l], m[4], m[3], moe_router[j], n_out_tiles, n_lat_tiles)
            h = _moe_layer(a, logits, h, m[5][:nb], moe_w_gate[j].astype(BF16), moe_w_up[j].astype(BF16),
                           moe_w_down[j].astype(BF16))
    return _final_norm(h[:, :n_lat], final_norm_g)
```

```python
import functools
import math

import jax
import jax.numpy as jnp
from jax import lax
from jax.experimental import pallas as pl
from jax.experimental.pallas import tpu as pltpu

F32 = jnp.float32
BF16 = jnp.bfloat16

GRID_W = 64
ROPE_THETA = 10000.0
NORM_EPS = 1e-6
SUBLN_EPS = 1e-5
GN_EPS = 64e-5
TOP_K = 2

LANES = 128
ROW_TILE = 256
RWKV_HEAD = 64
RWKV_CHUNK = 64
RWKV_QUAD = 4
RWKV_STEP_QUADS = 2
MOE_ROWS = 512
MM_ROWS = 512
ATTN_KEYS = 1408
ATTN_ROWS = 1024
LOG2E = 1.4426950408889634
VMEM_LIMIT = 56 * 1024 * 1024


def _cparams(sem):
    return pltpu.CompilerParams(dimension_semantics=sem, vmem_limit_bytes=VMEM_LIMIT)


def _pick(n, pref):
    if n <= pref:
        return n
    t = pref - pref % LANES
    while t >= LANES:
        if n % t == 0:
            return t
        t -= LANES
    return n


def _ada_kernel(c_ref, w_ref, b_ref, o_ref):
    c = c_ref[...]
    s = (c * jax.nn.sigmoid(c)).astype(BF16)
    o_ref[...] = jnp.dot(s, w_ref[...].astype(BF16), preferred_element_type=F32) + b_ref[...]


def _ada(cvec, ada_w, ada_b):
    n_layers, d, n = ada_w.shape
    tn = _pick(n, 512)
    return pl.pallas_call(
        _ada_kernel,
        grid=(n_layers, n // tn),
        in_specs=[
            pl.BlockSpec((8, d), lambda l, j: (0, 0)),
            pl.BlockSpec((None, d, tn), lambda l, j: (l, 0, j)),
            pl.BlockSpec((None, 1, tn), lambda l, j: (l, 0, j)),
        ],
        out_specs=pl.BlockSpec((None, 8, tn), lambda l, j: (l, 0, j)),
        out_shape=jax.ShapeDtypeStruct((n_layers, 8, n), F32),
        compiler_params=_cparams(("parallel", "parallel")),
        name="ada",
    )(cvec, ada_w, ada_b.reshape(n_layers, 1, n))


def _mod_sel(n_lat_tiles, n_batch):
    return lambda b, i: (jnp.where(i >= n_lat_tiles, n_batch, b), 0, 0)


def _norm_mod_kernel(x_ref, g_ref, sc_ref, sh_ref, o_ref):
    x = x_ref[...]
    y = x * lax.rsqrt(jnp.mean(x * x, axis=-1, keepdims=True) + NORM_EPS) * g_ref[...]
    o_ref[...] = (y * (1.0 + sc_ref[...]) + sh_ref[...]).astype(o_ref.dtype)


def _norm_mod(x, g, scale, shift, n_tiles, n_lat_tiles, out_dtype=BF16):
    nb, _, d = x.shape
    sel = _mod_sel(n_lat_tiles, nb)
    return pl.pallas_call(
        _norm_mod_kernel,
        grid=(nb, n_tiles),
        in_specs=[
            pl.BlockSpec((None, ROW_TILE, d), lambda b, i: (b, i, 0)),
            pl.BlockSpec((1, d), lambda b, i: (0, 0)),
            pl.BlockSpec((None, 1, d), sel),
            pl.BlockSpec((None, 1, d), sel),
        ],
        out_specs=pl.BlockSpec((None, ROW_TILE, d), lambda b, i: (b, i, 0)),
        out_shape=jax.ShapeDtypeStruct((nb, n_tiles * ROW_TILE, d), out_dtype),
        compiler_params=_cparams(("parallel", "parallel")),
        name="norm_mod",
    )(x, g.reshape(1, d), scale, shift)


def _norm_router_kernel(x_ref, g_ref, sc_ref, sh_ref, rh_ref, rl_ref, o_ref, lg_ref):
    x = x_ref[...]
    y = x * lax.rsqrt(jnp.mean(x * x, axis=-1, keepdims=True) + NORM_EPS) * g_ref[...]
    a = y * (1.0 + sc_ref[...]) + sh_ref[...]
    o_ref[...] = a
    ah = a.astype(BF16)
    al = (a - ah.astype(F32)).astype(BF16)
    rh = rh_ref[...]
    lg_ref[...] = (jnp.dot(ah, rh, preferred_element_type=F32)
                   + jnp.dot(al, rh, preferred_element_type=F32)
                   + jnp.dot(ah, rl_ref[...], preferred_element_type=F32))


def _norm_router(x, g, scale, shift, router, n_tiles, n_lat_tiles):
    nb, _, d = x.shape
    n_exp = router.shape[1]
    r_pad = jnp.pad(router, ((0, 0), (0, LANES - n_exp)))
    r_hi = r_pad.astype(BF16)
    r_lo = (r_pad - r_hi.astype(F32)).astype(BF16)
    sel = _mod_sel(n_lat_tiles, nb)
    rows = n_tiles * ROW_TILE
    return pl.pallas_call(
        _norm_router_kernel,
        grid=(nb, n_tiles),
        in_specs=[
            pl.BlockSpec((None, ROW_TILE, d), lambda b, i: (b, i, 0)),
            pl.BlockSpec((1, d), lambda b, i: (0, 0)),
            pl.BlockSpec((None, 1, d), sel),
            pl.BlockSpec((None, 1, d), sel),
            pl.BlockSpec((d, LANES), lambda b, i: (0, 0)),
            pl.BlockSpec((d, LANES), lambda b, i: (0, 0)),
        ],
        out_specs=[
            pl.BlockSpec((None, ROW_TILE, d), lambda b, i: (b, i, 0)),
            pl.BlockSpec((None, ROW_TILE, LANES), lambda b, i: (b, i, 0)),
        ],
        out_shape=[
            jax.ShapeDtypeStruct((nb, rows, d), F32),
            jax.ShapeDtypeStruct((nb, rows, LANES), F32),
        ],
        compiler_params=_cparams(("parallel", "parallel")),
        name="norm_router",
    )(x, g.reshape(1, d), scale, shift, r_hi, r_lo)


def _final_norm_kernel(x_ref, g_ref, o_ref):
    x = x_ref[...]
    o_ref[...] = x * lax.rsqrt(jnp.mean(x * x, axis=-1, keepdims=True) + NORM_EPS) * g_ref[...]


def _final_norm(x, g, n_tiles):
    nb, _, d = x.shape
    rows = n_tiles * ROW_TILE
    return pl.pallas_call(
        _final_norm_kernel,
        grid=(nb, n_tiles),
        in_specs=[
            pl.BlockSpec((None, ROW_TILE, d), lambda b, i: (b, i, 0)),
            pl.BlockSpec((1, d), lambda b, i: (0, 0)),
        ],
        out_specs=pl.BlockSpec((None, ROW_TILE, d), lambda b, i: (b, i, 0)),
        out_shape=jax.ShapeDtypeStruct((nb, rows, d), F32),
        compiler_params=_cparams(("parallel", "parallel")),
        name="final_norm",
    )(x, g.reshape(1, d))


def _row_tile(rows):
    return MM_ROWS if rows % MM_ROWS == 0 else ROW_TILE


def _mm_kernel(a_ref, w_ref, o_ref):
    o_ref[...] = jnp.dot(a_ref[...], w_ref[...], preferred_element_type=F32).astype(o_ref.dtype)


def _matmul(a, w, out_dtype=F32, tn_pref=1024):
    rows, k = a.shape
    n = w.shape[1]
    tn = _pick(n, tn_pref)
    tm = _row_tile(rows)
    return pl.pallas_call(
        _mm_kernel,
        grid=(n // tn, rows // tm),
        in_specs=[
            pl.BlockSpec((tm, k), lambda j, i: (i, 0)),
            pl.BlockSpec((k, tn), lambda j, i: (0, j)),
        ],
        out_specs=pl.BlockSpec((tm, tn), lambda j, i: (i, j)),
        out_shape=jax.ShapeDtypeStruct((rows, n), out_dtype),
        compiler_params=_cparams(("parallel", "parallel")),
        name="matmul",
    )(a, w)


def _mm_swiglu_kernel(a_ref, wg_ref, wu_ref, o_ref):
    a = a_ref[...]
    g = jnp.dot(a, wg_ref[...], preferred_element_type=F32)
    u = jnp.dot(a, wu_ref[...], preferred_element_type=F32)
    o_ref[...] = (g * jax.nn.sigmoid(g) * u).astype(o_ref.dtype)


def _matmul_swiglu(a, wg, wu, tn_pref=512):
    rows, k = a.shape
    n = wg.shape[1]
    tn = _pick(n, tn_pref)
    tm = _row_tile(rows)
    return pl.pallas_call(
        _mm_swiglu_kernel,
        grid=(n // tn, rows // tm),
        in_specs=[
            pl.BlockSpec((tm, k), lambda j, i: (i, 0)),
            pl.BlockSpec((k, tn), lambda j, i: (0, j)),
            pl.BlockSpec((k, tn), lambda j, i: (0, j)),
        ],
        out_specs=pl.BlockSpec((tm, tn), lambda j, i: (i, j)),
        out_shape=jax.ShapeDtypeStruct((rows, n), BF16),
        compiler_params=_cparams(("parallel", "parallel")),
        name="matmul_swiglu",
    )(a, wg, wu)


def _mm_resid_kernel(*refs, n_parts, n_gates):
    a_refs, w_refs = refs[:n_parts], refs[n_parts:2 * n_parts]
    res_ref = refs[2 * n_parts]
    gate_refs = refs[2 * n_parts + 1:2 * n_parts + 1 + n_gates]
    o_ref = refs[-1]
    acc = jnp.dot(a_refs[0][...], w_refs[0][...], preferred_element_type=F32)
    for a_ref, w_ref in zip(a_refs[1:], w_refs[1:]):
        acc = acc + jnp.dot(a_ref[...], w_ref[...], preferred_element_type=F32)
    rows = acc.shape[0] // n_gates
    for t, gate_ref in enumerate(gate_refs):
        sl = slice(t * rows, (t + 1) * rows)
        o_ref[sl, :] = res_ref[sl, :] + gate_ref[...] * acc[sl, :]


def _matmul_resid(parts, w, res, gate, tiles_per_batch, tm=None, tn_pref=512):
    rows, kg = parts[0].shape
    n_parts = len(parts)
    n = w.shape[1]
    nb = gate.shape[0] - 1
    tn = _pick(n, tn_pref)
    tm = tm or _row_tile(rows)
    n_gates = tm // ROW_TILE

    def gate_spec(t):
        def index(j, i):
            tile = i * n_gates + t
            pos = tile % tiles_per_batch
            return (jnp.where(pos == tiles_per_batch - 1, nb, tile // tiles_per_batch), 0, j)
        return pl.BlockSpec((None, 1, tn), index)

    return pl.pallas_call(
        functools.partial(_mm_resid_kernel, n_parts=n_parts, n_gates=n_gates),
        grid=(n // tn, rows // tm),
        in_specs=([pl.BlockSpec((tm, kg), lambda j, i: (i, 0))] * n_parts
                  + [pl.BlockSpec((kg, tn), lambda j, i, g=g: (g, j)) for g in range(n_parts)]
                  + [pl.BlockSpec((tm, tn), lambda j, i: (i, j))]
                  + [gate_spec(t) for t in range(n_gates)]),
        out_specs=pl.BlockSpec((tm, tn), lambda j, i: (i, j)),
        out_shape=jax.ShapeDtypeStruct((rows, n), F32),
        compiler_params=_cparams(("parallel", "parallel")),
        name="matmul_resid",
    )(*parts, *([w] * n_parts), res, *([gate] * n_gates))


_NT = (((1,), (1,)), ((), ()))


def _online_softmax(qs, k_ref, v_ref, ck):
    n_rows = qs.shape[0]
    m = jnp.full((n_rows, 1), -jnp.inf, F32)
    acc = jnp.zeros((n_rows, v_ref.shape[1]), F32)
    for c in range(k_ref.shape[0] // ck):
        keys = slice(c * ck, (c + 1) * ck)
        s = lax.dot_general(qs, k_ref[keys, :], _NT, preferred_element_type=F32)
        m_new = jnp.maximum(m, jnp.max(s, axis=-1, keepdims=True))
        p = jnp.exp2(s - m_new).astype(BF16)
        acc = jnp.exp2(m - m_new) * acc + jnp.dot(p, v_ref[keys, :], preferred_element_type=F32)
        m = m_new
    return acc


def _attn_kernel(q_ref, k_ref, v_ref, o_ref, *, pieces, dv, ck):
    q = q_ref[...]
    tq, dk = q.shape[0], q.shape[1] // pieces
    qs = q if pieces == 1 else jnp.concatenate([q[:, g * dk:(g + 1) * dk] for g in range(pieces)], axis=0)
    acc = _online_softmax(qs, k_ref, v_ref, ck)
    o = acc[:, :dv] / acc[:, dv:dv + 1]
    for g in range(pieces):
        o_ref[:, g * dv:(g + 1) * dv] = o[g * tq:(g + 1) * tq].astype(o_ref.dtype)


def _attention(q, k, v1, n_kv_heads, group, dk, dv, tq):
    nb, t_q, _ = q.shape
    s_len = k.shape[1]
    tq = math.gcd(tq, t_q)
    dvp = v1.shape[2] // n_kv_heads
    return pl.pallas_call(
        functools.partial(_attn_kernel, pieces=group, dv=dv, ck=_pick(s_len, ATTN_KEYS)),
        grid=(nb, n_kv_heads, t_q // tq),
        in_specs=[
            pl.BlockSpec((None, tq, group * dk), lambda b, h, i: (b, i, h)),
            pl.BlockSpec((None, s_len, dk), lambda b, h, i: (b, 0, h)),
            pl.BlockSpec((None, s_len, dvp), lambda b, h, i: (b, 0, h)),
        ],
        out_specs=pl.BlockSpec((None, tq, group * dv), lambda b, h, i: (b, i, h)),
        out_shape=jax.ShapeDtypeStruct((nb, t_q, n_kv_heads * group * dv), BF16),
        compiler_params=_cparams(("parallel", "parallel", "parallel")),
        name="attention",
    )(q, k, v1)


def _diff_attn_kernel(lam_ref, q_ref, k_ref, v_ref, g_ref, o_ref, *, dv, ck, out_scale):
    q = q_ref[...]
    tq, width = q.shape
    first = lax.broadcasted_iota(jnp.int32, (tq, width), 1) < width // 2
    zero = jnp.zeros_like(q)
    qs = jnp.concatenate([jnp.where(first, q, zero), jnp.where(first, zero, q)], axis=0)
    acc = _online_softmax(qs, k_ref, v_ref, ck)
    o = acc[:, :dv] / acc[:, dv:dv + 1]
    o = o[:tq] - lam_ref[0] * o[tq:]
    o = o * lax.rsqrt(jnp.mean(o * o, axis=-1, keepdims=True) + SUBLN_EPS) * g_ref[...]
    o_ref[...] = (o * out_scale).astype(o_ref.dtype)


def _diff_attention(q, k, v1, lam, subln_g, n_heads, out_scale, tq):
    nb, t_q, w = q.shape
    dh = w // n_heads
    s_len = k.shape[1]
    tq = math.gcd(tq, t_q)
    dvp = v1.shape[2] // n_heads
    return pl.pallas_call(
        functools.partial(_diff_attn_kernel, dv=dh, ck=_pick(s_len, ATTN_KEYS), out_scale=out_scale),
        grid=(nb, n_heads, t_q // tq),
        in_specs=[
            pl.BlockSpec(memory_space=pltpu.SMEM),
            pl.BlockSpec((None, tq, dh), lambda b, h, i: (b, i, h)),
            pl.BlockSpec((None, s_len, dh), lambda b, h, i: (b, 0, h)),
            pl.BlockSpec((None, s_len, dvp), lambda b, h, i: (b, 0, h)),
            pl.BlockSpec((1, dh), lambda b, h, i: (0, 0)),
        ],
        out_specs=pl.BlockSpec((None, tq, dh), lambda b, h, i: (b, i, h)),
        out_shape=jax.ShapeDtypeStruct((nb, t_q, w), BF16),
        compiler_params=_cparams(("parallel", "parallel", "parallel")),
        name="diff_attention",
    )(lam.reshape(1).astype(F32), q, k, v1, subln_g.reshape(1, dh))


def _split3(x):
    h = x.astype(BF16)
    r = x - h.astype(F32)
    m = r.astype(BF16)
    return h, m, (r - m.astype(F32)).astype(BF16)


def _rwkv_chunk(r, v, kk, wl, kd, bb, h_state, rev):
    c, w = r.shape
    nh = w // RWKV_HEAD
    ri = lax.broadcasted_iota(jnp.int32, (c, c), 0)
    ci = lax.broadcasted_iota(jnp.int32, (c, c), 1)
    before_incl = (ci >= ri) if rev else (ci <= ri)
    tri = jnp.where(before_incl, 1.0, 0.0).astype(BF16)
    w_h, w_m, w_l = _split3(wl)
    cum = (jnp.dot(tri, w_h, preferred_element_type=F32) + jnp.dot(tri, w_m, preferred_element_type=F32)
           + jnp.dot(tri, w_l, preferred_element_type=F32))
    cum_prev = cum - wl
    last = 0 if rev else c - 1
    total = cum[last:last + 1, :]
    mid = cum[c // 2:c // 2 + 1, :]
    a = -kk
    e_pm = jnp.exp(cum_prev - mid)
    e_cm = jnp.exp(cum - mid)
    e_mc = jnp.exp(mid - cum)
    e_end = jnp.exp(total - cum)

    lane_head = lax.broadcasted_iota(jnp.int32, (c, w), 1) // RWKV_HEAD

    def stack(x):
        return jnp.concatenate([jnp.where(lane_head == h, x, 0.0) for h in range(nh)], axis=0)

    def unstack(x):
        out = x[0:c]
        for h in range(1, nh):
            out = out + x[h * c:(h + 1) * c]
        return out

    l_a, l_r = stack(a * e_pm), stack(r * e_cm)
    r_b, r_k = stack(bb * e_mc), stack(kd * e_mc)
    v_st = stack(v)
    n = nh * c
    rn = lax.broadcasted_iota(jnp.int32, (n, n), 0)
    cn = lax.broadcasted_iota(jnp.int32, (n, n), 1)
    tr, ts = rn % c, cn % c
    strict = (ts > tr) if rev else (ts < tr)
    incl = (ts >= tr) if rev else (ts <= tr)
    nt = (((1,), (1,)), ((), ()))
    tn = (((0,), (0,)), ((), ()))

    def dot_nt(x, y):
        return lax.dot_general(x.astype(BF16), y.astype(BF16), nt, preferred_element_type=F32)

    def dot_tn(x, y):
        return lax.dot_general(x.astype(BF16), y.astype(BF16), tn, preferred_element_type=F32)

    def dot(x, y):
        return jnp.dot(x.astype(BF16), y.astype(BF16), preferred_element_type=F32)

    n_ab = jnp.where(strict, dot_nt(l_a, r_b), 0.0)
    a_ak = jnp.where(strict, dot_nt(l_a, r_k), 0.0)
    a_rb = jnp.where(incl, dot_nt(l_r, r_b), 0.0)
    a_rk = jnp.where(incl, dot_nt(l_r, r_k), 0.0)

    eye = jnp.where(rn == cn, 1.0, 0.0)
    t_inv = eye + n_ab
    pw = n_ab
    for _ in range(int(math.log2(c)) - 1):
        pw = dot(pw, pw)
        t_inv = t_inv + dot(pw, t_inv)

    l_a0 = stack(a * jnp.exp(cum_prev))
    l_r0 = stack(r * jnp.exp(cum))
    b_end, k_end = stack(bb * e_end), stack(kd * e_end)
    u0 = dot(t_inv, dot(a_ak, v_st))
    ta = dot(t_inv, l_a0)
    y0 = unstack(dot(a_rb, u0) + dot(a_rk, v_st))
    ra = unstack(l_r0 + dot(a_rb, ta))
    rw = lax.broadcasted_iota(jnp.int32, (w, w), 0)
    cw = lax.broadcasted_iota(jnp.int32, (w, w), 1)
    p = jnp.where(rw == cw, jnp.broadcast_to(jnp.exp(total), (w, w)), 0.0) + dot_tn(b_end, ta)
    q = dot_tn(b_end, u0) + dot_tn(k_end, v_st)
    y = y0 + dot(ra, h_state)
    return y, dot(p, h_state) + q


def _rwkv_scan_kernel(rf_ref, vf_ref, kkf_ref, wlf_ref, kdf_ref, bbf_ref,
                      rr_ref, vr_ref, kkr_ref, wlr_ref, kdr_ref, bbr_ref,
                      yf_ref, yr_ref, hf_ref, hr_ref):
    @pl.when(pl.program_id(2) == 0)
    def _():
        hf_ref[...] = jnp.zeros_like(hf_ref)
        hr_ref[...] = jnp.zeros_like(hr_ref)

    qw = RWKV_QUAD * RWKV_HEAD
    for i in range(hf_ref.shape[0]):
        sl = slice(i * qw, (i + 1) * qw)
        y, h = _rwkv_chunk(rf_ref[:, sl], vf_ref[:, sl], kkf_ref[:, sl], wlf_ref[:, sl], kdf_ref[:, sl],
                           bbf_ref[:, sl], hf_ref[i], rev=False)
        yf_ref[:, sl] = y
        hf_ref[i] = h
        y, h = _rwkv_chunk(rr_ref[:, sl], vr_ref[:, sl], kkr_ref[:, sl], wlr_ref[:, sl], kdr_ref[:, sl],
                           bbr_ref[:, sl], hr_ref[i], rev=True)
        yr_ref[:, sl] = y
        hr_ref[i] = h


def _rwkv_scan(r, v, kk, wl, kd, bb, n_lat):
    nb, s_len, w = r.shape
    c = RWKV_CHUNK
    qw = RWKV_QUAD * RWKV_HEAD
    nq = RWKV_STEP_QUADS if w % (RWKV_STEP_QUADS * qw) == 0 else 1
    bw = nq * qw
    n_l, n_all = n_lat // c, s_len // c
    n_c = n_all - n_l

    def fwd(s):
        return jnp.where(s < n_c, n_l + s, s - n_c)

    def rev(s):
        return n_all - 1 - s

    def shared(order):
        return pl.BlockSpec((None, c, bw), lambda b, q, s: (b, order(s), q))

    def per_dir(d, order):
        return pl.BlockSpec((None, None, c, bw), lambda b, q, s: (d, b, order(s), q))

    yf, yr = pl.pallas_call(
        _rwkv_scan_kernel,
        grid=(nb, w // bw, n_all),
        in_specs=[shared(fwd), shared(fwd), shared(fwd), per_dir(0, fwd), per_dir(0, fwd), per_dir(0, fwd),
                  shared(rev), shared(rev), shared(rev), per_dir(1, rev), per_dir(1, rev), per_dir(1, rev)],
        out_specs=[shared(fwd), shared(rev)],
        out_shape=[jax.ShapeDtypeStruct((nb, s_len, w), F32)] * 2,
        scratch_shapes=[pltpu.VMEM((nq, qw, qw), F32), pltpu.VMEM((nq, qw, qw), F32)],
        compiler_params=_cparams(("parallel", "parallel", "arbitrary")),
        name="rwkv_scan",
    )(r, v, kk, wl, kd, bb, r, v, kk, wl, kd, bb)
    return yf, yr


def _moe_gather_kernel(src_ref, tok_ref, o_ref, buf_ref, sem_ref):
    base = pl.program_id(0) * MOE_ROWS

    def copy(r):
        return pltpu.make_async_copy(tok_ref.at[pl.ds(src_ref[base + r], 1), :], buf_ref.at[pl.ds(r, 1), :],
                                     sem_ref.at[0])

    def start(r, carry):
        copy(r).start()
        return carry

    def wait(r, carry):
        copy(r).wait()
        return carry

    lax.fori_loop(0, MOE_ROWS, start, 0)
    lax.fori_loop(0, MOE_ROWS, wait, 0)
    o_ref[...] = buf_ref[...].astype(o_ref.dtype)


def _moe_gather(tok, src, n_blocks):
    d = tok.shape[1]
    return pl.pallas_call(
        _moe_gather_kernel,
        grid_spec=pltpu.PrefetchScalarGridSpec(
            num_scalar_prefetch=1,
            grid=(n_blocks,),
            in_specs=[pl.BlockSpec(memory_space=pl.ANY)],
            out_specs=pl.BlockSpec((MOE_ROWS, d), lambda i, src: (i, 0)),
            scratch_shapes=[pltpu.VMEM((MOE_ROWS, d), F32), pltpu.SemaphoreType.DMA((1,))],
        ),
        out_shape=jax.ShapeDtypeStruct((n_blocks * MOE_ROWS, d), BF16),
        compiler_params=_cparams(("arbitrary",)),
        name="moe_gather",
    )(src, tok)


def _moe_up_kernel(be_ref, a_ref, wg_ref, wu_ref, o_ref):
    a = a_ref[...]
    g = jnp.dot(a, wg_ref[...], preferred_element_type=F32)
    u = jnp.dot(a, wu_ref[...], preferred_element_type=F32)
    o_ref[...] = (g * jax.nn.sigmoid(g) * u).astype(o_ref.dtype)


def _moe_up(buf, block_expert, wg, wu, tn_pref=512):
    rows, d = buf.shape
    f = wg.shape[2]
    tn = _pick(f, tn_pref)
    n_blocks = rows // MOE_ROWS
    return pl.pallas_call(
        _moe_up_kernel,
        grid_spec=pltpu.PrefetchScalarGridSpec(
            num_scalar_prefetch=1,
            grid=(n_blocks, f // tn),
            in_specs=[
                pl.BlockSpec((MOE_ROWS, d), lambda i, j, be: (i, 0)),
                pl.BlockSpec((None, d, tn), lambda i, j, be: (be[i], 0, j)),
                pl.BlockSpec((None, d, tn), lambda i, j, be: (be[i], 0, j)),
            ],
            out_specs=pl.BlockSpec((MOE_ROWS, tn), lambda i, j, be: (i, j)),
        ),
        out_shape=jax.ShapeDtypeStruct((rows, f), BF16),
        compiler_params=_cparams(("parallel", "parallel")),
        name="moe_up",
    )(block_expert, buf, wg, wu)


def _moe_down_kernel(be_ref, h_ref, w_ref, o_ref):
    o_ref[...] = jnp.dot(h_ref[...], w_ref[...], preferred_element_type=F32)


def _moe_down(hid, block_expert, wd, tn_pref=512):
    rows, f = hid.shape
    d = wd.shape[2]
    tn = _pick(d, tn_pref)
    n_blocks = rows // MOE_ROWS
    return pl.pallas_call(
        _moe_down_kernel,
        grid_spec=pltpu.PrefetchScalarGridSpec(
            num_scalar_prefetch=1,
            grid=(n_blocks, d // tn),
            in_specs=[
                pl.BlockSpec((MOE_ROWS, f), lambda i, j, be: (i, 0)),
                pl.BlockSpec((None, f, tn), lambda i, j, be: (be[i], 0, j)),
            ],
            out_specs=pl.BlockSpec((MOE_ROWS, tn), lambda i, j, be: (i, j)),
        ),
        out_shape=jax.ShapeDtypeStruct((rows, d), F32),
        compiler_params=_cparams(("parallel", "parallel")),
        name="moe_down",
    )(block_expert, hid, wd)


def _moe_combine_kernel(dest_ref, y_ref, x_ref, gate_ref, g0_ref, g1_ref, o_ref, b0_ref, b1_ref, sem_ref):
    base = (pl.program_id(0) * pl.num_programs(1) + pl.program_id(1)) * ROW_TILE

    def copies(r):
        t = (base + r) * TOP_K
        return (pltpu.make_async_copy(y_ref.at[pl.ds(dest_ref[t], 1), :], b0_ref.at[pl.ds(r, 1), :], sem_ref.at[0]),
                pltpu.make_async_copy(y_ref.at[pl.ds(dest_ref[t + 1], 1), :], b1_ref.at[pl.ds(r, 1), :],
                                      sem_ref.at[1]))

    def start(r, carry):
        c0, c1 = copies(r)
        c0.start()
        c1.start()
        return carry

    def wait(r, carry):
        c0, c1 = copies(r)
        c0.wait()
        c1.wait()
        return carry

    lax.fori_loop(0, ROW_TILE, start, 0)
    lax.fori_loop(0, ROW_TILE, wait, 0)
    o_ref[...] = x_ref[...] + gate_ref[...] * (g0_ref[...] * b0_ref[...] + g1_ref[...] * b1_ref[...])


def _moe_combine(y, dest, x, gate, g0, g1):
    nb, _, d = x.shape
    t = g0.shape[1]
    return pl.pallas_call(
        _moe_combine_kernel,
        grid_spec=pltpu.PrefetchScalarGridSpec(
            num_scalar_prefetch=1,
            grid=(nb, t // ROW_TILE),
            in_specs=[
                pl.BlockSpec(memory_space=pl.ANY),
                pl.BlockSpec((None, ROW_TILE, d), lambda b, i, dest: (b, i, 0)),
                pl.BlockSpec((None, 1, d), lambda b, i, dest: (b, 0, 0)),
                pl.BlockSpec((None, ROW_TILE, 1), lambda b, i, dest: (b, i, 0)),
                pl.BlockSpec((None, ROW_TILE, 1), lambda b, i, dest: (b, i, 0)),
            ],
            out_specs=pl.BlockSpec((None, ROW_TILE, d), lambda b, i, dest: (b, i, 0)),
            scratch_shapes=[pltpu.VMEM((ROW_TILE, d), F32), pltpu.VMEM((ROW_TILE, d), F32),
                            pltpu.SemaphoreType.DMA((2,))],
        ),
        out_shape=jax.ShapeDtypeStruct((nb, t, d), F32),
        compiler_params=_cparams(("arbitrary", "arbitrary")),
        name="moe_combine",
    )(dest, y, x, gate, g0, g1)


def _moe_layer(a_tok, logits, x, gate, wg, wu, wd):
    nb, t, d = a_tok.shape
    n_exp = wg.shape[0]
    n_tok = nb * t
    n_assign = n_tok * TOP_K
    top_logit, top_idx = lax.top_k(logits.reshape(n_tok, -1)[:, :n_exp], TOP_K)
    gates = jax.nn.softmax(top_logit, axis=-1)
    flat_e = top_idx.reshape(-1)
    onehot = (flat_e[:, None] == jnp.arange(n_exp)[None, :]).astype(jnp.int32)
    rank = jnp.take_along_axis(jnp.cumsum(onehot, axis=0) - onehot, flat_e[:, None], axis=1)[:, 0]
    counts = jnp.sum(onehot, axis=0)
    padded = (counts + MOE_ROWS - 1) // MOE_ROWS * MOE_ROWS
    pad_end = jnp.cumsum(padded)
    pad_start = pad_end - padded
    dest = (pad_start[flat_e] + rank).astype(jnp.int32)
    n_blocks = -(-n_assign // MOE_ROWS) + n_exp
    src = jnp.zeros((n_blocks * MOE_ROWS,), jnp.int32).at[dest].set(jnp.arange(n_assign, dtype=jnp.int32) // TOP_K)
    block_expert = jnp.minimum(
        jnp.searchsorted(pad_end, jnp.arange(n_blocks) * MOE_ROWS, side='right'), n_exp - 1).astype(jnp.int32)
    buf = _moe_gather(a_tok.reshape(n_tok, d), src, n_blocks)
    hid = _moe_up(buf, block_expert, wg, wu)
    y = _moe_down(hid, block_expert, wd)
    return _moe_combine(y, dest, x, gate, gates[:, 0].reshape(nb, t, 1), gates[:, 1].reshape(nb, t, 1))


def _rope_tables(n_lat, n_ctx, rot_dim):
    rows = n_lat // GRID_W
    row = jnp.repeat(jnp.arange(rows), GRID_W).astype(F32)
    col = jnp.tile(jnp.arange(GRID_W), rows).astype(F32)
    n_freq = rot_dim // 4
    inv = ROPE_THETA ** (-jnp.arange(n_freq, dtype=F32) / n_freq)
    ang = jnp.concatenate([row[:, None] * inv, col[:, None] * inv], axis=-1)
    ang = jnp.concatenate([ang, jnp.zeros((n_ctx, rot_dim // 2), F32)], axis=0)
    return jnp.cos(ang), jnp.sin(ang)


def _rope(x, cs):
    cos, sin = cs
    half = x.shape[-1] // 2
    x1, x2 = x[..., :half], x[..., half:]
    c, s = cos[None, :, None, :], sin[None, :, None, :]
    return jnp.concatenate([x1 * c - x2 * s, x1 * s + x2 * c], axis=-1)


def _rms(x, g, eps=NORM_EPS):
    return x * lax.rsqrt(jnp.mean(x * x, axis=-1, keepdims=True) + eps) * g


def _seq_shift(p, mu, n_lat):
    def one(z):
        zero = jnp.zeros_like(z[:, :1])
        prev = jnp.concatenate([zero, z[:, :-1]], axis=1)
        nxt = jnp.concatenate([z[:, 1:], zero], axis=1)
        return z + mu * (0.5 * (prev + nxt) - z)
    return jnp.concatenate([one(p[:, :n_lat]), one(p[:, n_lat:])], axis=1)


def _cat_rows(lat, ctx):
    return jnp.concatenate([lat, ctx], axis=1)


def _mm3(a, w, **kw):
    return _matmul(a.reshape(-1, a.shape[-1]), w, **kw).reshape(a.shape[0], a.shape[1], -1)


def _value_ones(v, n_heads):
    nb, s_len, w = v.shape
    dv = w // n_heads
    v = v.reshape(nb, s_len, n_heads, dv)
    pad = jnp.zeros((nb, s_len, n_heads, dv), v.dtype).at[..., 0].set(1.0)
    return jnp.concatenate([v, pad], axis=-1).reshape(nb, s_len, 2 * w).astype(BF16)


def _rwkv_mixer(pa, n_lat, mu, w0, w_up, a0, a_up, g_up, k_k, k_a, r_k, lnx_w, lnx_b):
    nb, s_len, _ = pa.shape
    gw = k_k.shape[0]
    nh = gw // RWKV_HEAD
    lora_w, lora_a = w_up.shape[1], a_up.shape[1]
    ps = _seq_shift(pa, mu, n_lat)
    r, k, v = ps[..., :gw], ps[..., gw:2 * gw], ps[..., 2 * gw:3 * gw]
    o = 3 * gw
    wd, ad, gd = ps[..., o:o + lora_w], ps[..., o + lora_w:o + lora_w + lora_a], ps[..., o + lora_w + lora_a:]
    hd = lambda z: z.reshape(nb, s_len, nh, RWKV_HEAD)
    kk = hd(k * k_k)
    kk = (kk / jnp.maximum(jnp.sqrt(jnp.sum(kk * kk, axis=-1, keepdims=True)), 1e-12)).reshape(nb, s_len, gw)
    tanh_wd = jnp.tanh(wd).astype(BF16)
    ad_b = ad.astype(BF16)
    wls, kds, bbs = [], [], []
    bonus = 0.0
    for d in range(2):
        w_log = -jax.nn.softplus(-(w0[d] + _mm3(tanh_wd, w_up[d].astype(BF16)))) - 0.5
        a_sig = jax.nn.sigmoid(a0[d] + _mm3(ad_b, a_up[d].astype(BF16)))
        kd = k * (1.0 + (a_sig - 1.0) * k_a)
        wls.append(-jnp.exp(w_log))
        kds.append(kd)
        bbs.append(kk * a_sig)
        bonus = bonus + jnp.sum(hd(r * kd) * r_k, axis=-1, keepdims=True) * hd(v)
    yf, yr = _rwkv_scan(r, v, kk, jnp.stack(wls), jnp.stack(kds), jnp.stack(bbs), n_lat)
    y = hd(yf + yr)
    rows = s_len
    mean = jnp.mean(y, axis=-1, keepdims=True)
    var = jnp.mean(jnp.square(y - mean), axis=-1, keepdims=True)
    yn = ((y - mean) * lax.rsqrt(var + GN_EPS)).reshape(nb, rows, gw) * lnx_w + lnx_b
    lora_g = g_up.shape[0]
    pad_g = -lora_g % LANES
    sg = jnp.pad(jax.nn.sigmoid(gd), ((0, 0), (0, 0), (0, pad_g))).astype(BF16)
    g = _mm3(sg, jnp.pad(g_up, ((0, pad_g), (0, 0))).astype(BF16))
    return ((yn + bonus.reshape(nb, rows, gw)) * g).astype(BF16)


def _diff_mixer(pb, n_lat, lq1, lk1, lq2, lk2, subln_g, lambda_init, rope_cs):
    nb, s_len, w3 = pb.shape
    gw = w3 // 3
    dh = subln_g.shape[0] // 2
    nh = gw // (2 * dh)
    lam = jnp.exp(jnp.sum(lq1 * lk1)) - jnp.exp(jnp.sum(lq2 * lk2)) + lambda_init
    q, k, v = pb[..., :gw], pb[..., gw:2 * gw], pb[..., 2 * gw:]
    q = _rope(q.reshape(nb, s_len, nh * 2, dh), rope_cs) * (dh ** -0.5 * LOG2E)
    k = _rope(k.reshape(nb, s_len, nh * 2, dh), rope_cs)
    q = q.reshape(nb, s_len, gw).astype(BF16)
    k = k.reshape(nb, s_len, gw).astype(BF16)
    v1 = _value_ones(v, nh)
    scale = 1.0 - lambda_init
    tq = ATTN_ROWS // 2
    out = _diff_attention(q[:, :n_lat], k, v1, lam, subln_g, nh, scale, tq)
    return _cat_rows(out, _diff_attention(q[:, n_lat:], k[:, n_lat:], v1[:, n_lat:], lam, subln_g, nh, scale, tq))


def _gqa_mixer(pc, n_lat, q_norm_g, k_norm_g, n_heads, n_kv, rope_cs):
    nb, s_len, _ = pc.shape
    dh = q_norm_g.shape[0]
    q, k, v = pc[..., :n_heads * dh], pc[..., n_heads * dh:(n_heads + n_kv) * dh], pc[..., (n_heads + n_kv) * dh:]
    q = _rope(_rms(q.reshape(nb, s_len, n_heads, dh), q_norm_g), rope_cs) * (dh ** -0.5 * LOG2E)
    k = _rope(_rms(k.reshape(nb, s_len, n_kv, dh), k_norm_g), rope_cs)
    q = q.reshape(nb, s_len, n_heads * dh).astype(BF16)
    k = k.reshape(nb, s_len, n_kv * dh).astype(BF16)
    v1 = _value_ones(v, n_kv)
    group = n_heads // n_kv
    tq = ATTN_ROWS // group
    out = _attention(q[:, :n_lat], k, v1, n_kv, group, dh, dh, tq)
    return _cat_rows(out, _attention(q[:, n_lat:], k[:, n_lat:], v1[:, n_lat:], n_kv, group, dh, dh, tq))


def _mla_mixer(pd, n_lat, q_norm_g, q_up, kv_norm_g, kv_up, n_heads, d_rope, rope_cs):
    nb, s_len, _ = pd.shape
    q_lora, kv_lora = q_norm_g.shape[0], kv_norm_g.shape[0]
    d_qk = q_up.shape[1] // n_heads
    d_nope = d_qk - d_rope
    d_v = kv_up.shape[1] // n_heads - d_nope
    c_q, c_kv, k_rope = pd[..., :q_lora], pd[..., q_lora:q_lora + kv_lora], pd[..., q_lora + kv_lora:
                                                                                 q_lora + kv_lora + d_rope]
    q = _mm3(_rms(c_q, q_norm_g).astype(BF16), q_up.astype(BF16)).reshape(nb, s_len, n_heads, d_qk)
    kv = _mm3(_rms(c_kv, kv_norm_g).astype(BF16), kv_up.astype(BF16)).reshape(nb, s_len, n_heads, d_nope + d_v)
    q_rope = _rope(q[..., d_nope:], rope_cs)
    k_rope = jnp.broadcast_to(_rope(k_rope[:, :, None, :], rope_cs), (nb, s_len, n_heads, d_rope))
    d_pad = -d_qk % LANES
    zeros = jnp.zeros((nb, s_len, n_heads, d_pad), F32)
    q = jnp.concatenate([q[..., :d_nope], q_rope, zeros], axis=-1) * (d_qk ** -0.5 * LOG2E)
    k = jnp.concatenate([kv[..., :d_nope], k_rope, zeros], axis=-1)
    dk = d_qk + d_pad
    q = q.reshape(nb, s_len, n_heads * dk).astype(BF16)
    k = k.reshape(nb, s_len, n_heads * dk).astype(BF16)
    v1 = _value_ones(kv[..., d_nope:].reshape(nb, s_len, n_heads * d_v), n_heads)
    out = _attention(q[:, :n_lat], k, v1, n_heads, 1, dk, d_v, ATTN_ROWS)
    return _cat_rows(out, _attention(q[:, n_lat:], k[:, n_lat:], v1[:, n_lat:], n_heads, 1, dk, d_v, ATTN_ROWS))


def _pad_cols(w, mult):
    return jnp.pad(w, ((0, 0), (0, -w.shape[1] % mult)))


def kernel(x, c, ctx, c_ctx, ada_w, ada_b, norm_mix_g, norm_ffn_g, w_in, rwkv_mu, rwkv_w0, rwkv_w_up, rwkv_a0, rwkv_a_up, rwkv_g_up, rwkv_k_k, rwkv_k_a, rwkv_r_k, rwkv_lnx_w, rwkv_lnx_b, diff_lq1, diff_lk1, diff_lq2, diff_lk2, diff_subln_g, gqa_q_norm_g, gqa_k_norm_g, mla_q_norm_g, mla_q_up, mla_kv_norm_g, mla_kv_up, w_out, ffn_w_gate, ffn_w_up, ffn_w_down, moe_router, moe_w_gate, moe_w_up, moe_w_down, final_norm_g):
    nb, n_lat, d = x.shape
    n_ctx = ctx.shape[1]
    depth = ada_w.shape[0]
    assert n_ctx == ROW_TILE and n_lat % ROW_TILE == 0 and n_lat % GRID_W == 0
    n_lat_tiles = n_lat // ROW_TILE
    n_all_tiles = n_lat_tiles + 1

    gw = rwkv_k_k.shape[1]
    cols_a = 3 * gw + rwkv_w_up.shape[2] + rwkv_a_up.shape[2] + rwkv_g_up.shape[1]
    cols_b = 3 * gw
    gqa_dh = gqa_q_norm_g.shape[1]
    mla_rope = 64
    mla_heads = 8
    gqa_heads = gw // gqa_dh
    cols_d = mla_q_norm_g.shape[1] + mla_kv_norm_g.shape[1] + mla_rope
    cols_c = w_in.shape[2] - cols_a - cols_b - cols_d
    gqa_kv = (cols_c - gw) // (2 * gqa_dh)
    offs = (0, cols_a, cols_a + cols_b, cols_a + cols_b + cols_c, w_in.shape[2])

    rope_diff = _rope_tables(n_lat, n_ctx, diff_subln_g.shape[1] // 2)
    rope_gqa = _rope_tables(n_lat, n_ctx, gqa_dh)
    rope_mla = _rope_tables(n_lat, n_ctx, mla_rope)

    cvec = jnp.concatenate([c, c_ctx[None, :], jnp.zeros((8 - nb - 1, d), F32)], axis=0)
    mods = _ada(cvec, ada_w, ada_b)[:, :nb + 1].reshape(depth, nb + 1, 6, 1, d)

    h = jnp.concatenate([x, ctx], axis=1)
    s_len = n_lat + n_ctx
    flat = lambda z: z.reshape(nb * s_len, z.shape[-1])
    for l in range(depth):
        need_ctx = l < depth - 1
        m = [mods[l, :, i] for i in range(6)]

        a = flat(_norm_mod(h, norm_mix_g[l], m[1], m[0], n_all_tiles, n_lat_tiles))
        groups = [_matmul(a, _pad_cols(w_in[l][:, offs[i]:offs[i + 1]], 768).astype(BF16), F32, 768)
                  .reshape(nb, s_len, -1) for i in range(4)]
        oa = _rwkv_mixer(groups[0][..., :cols_a], n_lat, rwkv_mu[l], rwkv_w0[l], rwkv_w_up[l],
                         rwkv_a0[l], rwkv_a_up[l], rwkv_g_up[l], rwkv_k_k[l], rwkv_k_a[l], rwkv_r_k[l],
                         rwkv_lnx_w[l], rwkv_lnx_b[l])
        lambda_init = 0.8 - 0.6 * math.exp(-0.3 * l)
        ob = _diff_mixer(groups[1], n_lat, diff_lq1[l], diff_lk1[l], diff_lq2[l], diff_lk2[l],
                         diff_subln_g[l], lambda_init, rope_diff)
        oc = _gqa_mixer(groups[2], n_lat, gqa_q_norm_g[l], gqa_k_norm_g[l], gqa_heads, gqa_kv, rope_gqa)
        od = _mla_mixer(groups[3], n_lat, mla_q_norm_g[l], mla_q_up[l], mla_kv_norm_g[l], mla_kv_up[l],
                        mla_heads, mla_rope, rope_mla)
        h = _matmul_resid([flat(o) for o in (oa, ob, oc, od)], w_out[l].astype(BF16), flat(h), m[2],
                          n_all_tiles).reshape(nb, s_len, d)

        j = l // 2
        if l % 2 == 0:
            d_ff = ffn_w_gate.shape[2]
            pad_f = -d_ff % 512
            w_g = jnp.pad(ffn_w_gate[j], ((0, 0), (0, pad_f))).astype(BF16)
            w_u = jnp.pad(ffn_w_up[j], ((0, 0), (0, pad_f))).astype(BF16)
            w_d = jnp.pad(ffn_w_down[j], ((0, pad_f), (0, 0))).astype(BF16)
            a = flat(_norm_mod(h, norm_ffn_g[l], m[4], m[3], n_all_tiles, n_lat_tiles))
            hid = _matmul_swiglu(a, w_g, w_u)
            h = _matmul_resid([hid], w_d, flat(h), m[5], n_all_tiles, tm=ROW_TILE).reshape(nb, s_len, d)
        else:
            assert not need_ctx
            a, logits = _norm_router(h, norm_ffn_g[l], m[4], m[3], moe_router[j], n_lat_tiles, n_lat_tiles)
            h = _moe_layer(a, logits, h, m[5][:nb], moe_w_gate[j].astype(BF16), moe_w_up[j].astype(BF16),
                           moe_w_down[j].astype(BF16))
    return _final_norm(h, final_norm_g, n_lat_tiles)
```

```python
import functools
import math

import jax
import jax.numpy as jnp
from jax import lax
from jax.experimental import pallas as pl
from jax.experimental.pallas import tpu as pltpu

F32 = jnp.float32
BF16 = jnp.bfloat16

GRID_W = 64
ROPE_THETA = 10000.0
NORM_EPS = 1e-6
SUBLN_EPS = 1e-5
GN_EPS = 64e-5
TOP_K = 2

LANES = 128
ROW_TILE = 256
RWKV_HEAD = 64
RWKV_CHUNK = 64
RWKV_QUAD = 4
RWKV_STEP_QUADS = 2
MOE_ROWS = 512
MM_ROWS = 512
ATTN_KEYS = 1408
ATTN_ROWS = 1024
LOG2E = 1.4426950408889634
VMEM_LIMIT = 56 * 1024 * 1024


def _cparams(sem):
    return pltpu.CompilerParams(dimension_semantics=sem, vmem_limit_bytes=VMEM_LIMIT)


def _pick(n, pref):
    if n <= pref:
        return n
    t = pref - pref % LANES
    while t >= LANES:
        if n % t == 0:
            return t
        t -= LANES
    return n


def _ada_kernel(c_ref, w_ref, b_ref, o_ref):
    c = c_ref[...]
    s = (c * jax.nn.sigmoid(c)).astype(BF16)
    o_ref[...] = jnp.dot(s, w_ref[...].astype(BF16), preferred_element_type=F32) + b_ref[...]


def _ada(cvec, ada_w, ada_b):
    n_layers, d, n = ada_w.shape
    tn = _pick(n, 512)
    return pl.pallas_call(
        _ada_kernel,
        grid=(n_layers, n // tn),
        in_specs=[
            pl.BlockSpec((8, d), lambda l, j: (0, 0)),
            pl.BlockSpec((None, d, tn), lambda l, j: (l, 0, j)),
            pl.BlockSpec((None, 1, tn), lambda l, j: (l, 0, j)),
        ],
        out_specs=pl.BlockSpec((None, 8, tn), lambda l, j: (l, 0, j)),
        out_shape=jax.ShapeDtypeStruct((n_layers, 8, n), F32),
        compiler_params=_cparams(("parallel", "parallel")),
        name="ada",
    )(cvec, ada_w, ada_b.reshape(n_layers, 1, n))


def _mod_sel(n_lat_tiles, n_batch):
    return lambda b, i: (jnp.where(i >= n_lat_tiles, n_batch, b), 0, 0)


def _norm_mod_kernel(x_ref, g_ref, sc_ref, sh_ref, o_ref):
    x = x_ref[...]
    y = x * lax.rsqrt(jnp.mean(x * x, axis=-1, keepdims=True) + NORM_EPS) * g_ref[...]
    o_ref[...] = (y * (1.0 + sc_ref[...]) + sh_ref[...]).astype(o_ref.dtype)


def _norm_mod(x, g, scale, shift, n_tiles, n_lat_tiles, out_dtype=BF16):
    nb, _, d = x.shape
    sel = _mod_sel(n_lat_tiles, nb)
    return pl.pallas_call(
        _norm_mod_kernel,
        grid=(nb, n_tiles),
        in_specs=[
            pl.BlockSpec((None, ROW_TILE, d), lambda b, i: (b, i, 0)),
            pl.BlockSpec((1, d), lambda b, i: (0, 0)),
            pl.BlockSpec((None, 1, d), sel),
            pl.BlockSpec((None, 1, d), sel),
        ],
        out_specs=pl.BlockSpec((None, ROW_TILE, d), lambda b, i: (b, i, 0)),
        out_shape=jax.ShapeDtypeStruct((nb, n_tiles * ROW_TILE, d), out_dtype),
        compiler_params=_cparams(("parallel", "parallel")),
        name="norm_mod",
    )(x, g.reshape(1, d), scale, shift)


def _norm_router_kernel(x_ref, g_ref, sc_ref, sh_ref, rh_ref, rl_ref, o_ref, lg_ref):
    x = x_ref[...]
    y = x * lax.rsqrt(jnp.mean(x * x, axis=-1, keepdims=True) + NORM_EPS) * g_ref[...]
    a = y * (1.0 + sc_ref[...]) + sh_ref[...]
    o_ref[...] = a
    ah = a.astype(BF16)
    al = (a - ah.astype(F32)).astype(BF16)
    rh = rh_ref[...]
    lg_ref[...] = (jnp.dot(ah, rh, preferred_element_type=F32)
                   + jnp.dot(al, rh, preferred_element_type=F32)
                   + jnp.dot(ah, rl_ref[...], preferred_element_type=F32))


def _norm_router(x, g, scale, shift, router, n_tiles, n_lat_tiles):
    nb, _, d = x.shape
    n_exp = router.shape[1]
    r_pad = jnp.pad(router, ((0, 0), (0, LANES - n_exp)))
    r_hi = r_pad.astype(BF16)
    r_lo = (r_pad - r_hi.astype(F32)).astype(BF16)
    sel = _mod_sel(n_lat_tiles, nb)
    rows = n_tiles * ROW_TILE
    return pl.pallas_call(
        _norm_router_kernel,
        grid=(nb, n_tiles),
        in_specs=[
            pl.BlockSpec((None, ROW_TILE, d), lambda b, i: (b, i, 0)),
            pl.BlockSpec((1, d), lambda b, i: (0, 0)),
            pl.BlockSpec((None, 1, d), sel),
            pl.BlockSpec((None, 1, d), sel),
            pl.BlockSpec((d, LANES), lambda b, i: (0, 0)),
            pl.BlockSpec((d, LANES), lambda b, i: (0, 0)),
        ],
        out_specs=[
            pl.BlockSpec((None, ROW_TILE, d), lambda b, i: (b, i, 0)),
            pl.BlockSpec((None, ROW_TILE, LANES), lambda b, i: (b, i, 0)),
        ],
        out_shape=[
            jax.ShapeDtypeStruct((nb, rows, d), F32),
            jax.ShapeDtypeStruct((nb, rows, LANES), F32),
        ],
        compiler_params=_cparams(("parallel", "parallel")),
        name="norm_router",
    )(x, g.reshape(1, d), scale, shift, r_hi, r_lo)


def _final_norm_kernel(x_ref, g_ref, o_ref):
    x = x_ref[...]
    o_ref[...] = x * lax.rsqrt(jnp.mean(x * x, axis=-1, keepdims=True) + NORM_EPS) * g_ref[...]


def _final_norm(x, g, n_tiles):
    nb, _, d = x.shape
    rows = n_tiles * ROW_TILE
    return pl.pallas_call(
        _final_norm_kernel,
        grid=(nb, n_tiles),
        in_specs=[
            pl.BlockSpec((None, ROW_TILE, d), lambda b, i: (b, i, 0)),
            pl.BlockSpec((1, d), lambda b, i: (0, 0)),
        ],
        out_specs=pl.BlockSpec((None, ROW_TILE, d), lambda b, i: (b, i, 0)),
        out_shape=jax.ShapeDtypeStruct((nb, rows, d), F32),
        compiler_params=_cparams(("parallel", "parallel")),
        name="final_norm",
    )(x, g.reshape(1, d))


def _row_tile(rows):
    return MM_ROWS if rows % MM_ROWS == 0 else ROW_TILE


def _mm_kernel(a_ref, w_ref, o_ref):
    o_ref[...] = jnp.dot(a_ref[...], w_ref[...], preferred_element_type=F32).astype(o_ref.dtype)


def _matmul(a, w, out_dtype=F32, tn_pref=1024):
    rows, k = a.shape
    n = w.shape[1]
    tn = _pick(n, tn_pref)
    tm = _row_tile(rows)
    return pl.pallas_call(
        _mm_kernel,
        grid=(n // tn, rows // tm),
        in_specs=[
            pl.BlockSpec((tm, k), lambda j, i: (i, 0)),
            pl.BlockSpec((k, tn), lambda j, i: (0, j)),
        ],
        out_specs=pl.BlockSpec((tm, tn), lambda j, i: (i, j)),
        out_shape=jax.ShapeDtypeStruct((rows, n), out_dtype),
        compiler_params=_cparams(("parallel", "parallel")),
        name="matmul",
    )(a, w)


def _mm_swiglu_kernel(a_ref, wg_ref, wu_ref, o_ref):
    a = a_ref[...]
    g = jnp.dot(a, wg_ref[...], preferred_element_type=F32)
    u = jnp.dot(a, wu_ref[...], preferred_element_type=F32)
    o_ref[...] = (g * jax.nn.sigmoid(g) * u).astype(o_ref.dtype)


def _matmul_swiglu(a, wg, wu, tn_pref=512):
    rows, k = a.shape
    n = wg.shape[1]
    tn = _pick(n, tn_pref)
    tm = _row_tile(rows)
    return pl.pallas_call(
        _mm_swiglu_kernel,
        grid=(n // tn, rows // tm),
        in_specs=[
            pl.BlockSpec((tm, k), lambda j, i: (i, 0)),
            pl.BlockSpec((k, tn), lambda j, i: (0, j)),
            pl.BlockSpec((k, tn), lambda j, i: (0, j)),
        ],
        out_specs=pl.BlockSpec((tm, tn), lambda j, i: (i, j)),
        out_shape=jax.ShapeDtypeStruct((rows, n), BF16),
        compiler_params=_cparams(("parallel", "parallel")),
        name="matmul_swiglu",
    )(a, wg, wu)


def _mm_resid_kernel(*refs, n_parts, n_gates):
    a_refs, w_refs = refs[:n_parts], refs[n_parts:2 * n_parts]
    res_ref = refs[2 * n_parts]
    gate_refs = refs[2 * n_parts + 1:2 * n_parts + 1 + n_gates]
    o_ref = refs[-1]
    acc = jnp.dot(a_refs[0][...], w_refs[0][...], preferred_element_type=F32)
    for a_ref, w_ref in zip(a_refs[1:], w_refs[1:]):
        acc = acc + jnp.dot(a_ref[...], w_ref[...], preferred_element_type=F32)
    rows = acc.shape[0] // n_gates
    for t, gate_ref in enumerate(gate_refs):
        sl = slice(t * rows, (t + 1) * rows)
        o_ref[sl, :] = res_ref[sl, :] + gate_ref[...] * acc[sl, :]


def _matmul_resid(parts, w, res, gate, tiles_per_batch, tm=None, tn_pref=512):
    rows, kg = parts[0].shape
    n_parts = len(parts)
    n = w.shape[1]
    nb = gate.shape[0] - 1
    tn = _pick(n, tn_pref)
    tm = tm or _row_tile(rows)
    n_gates = tm // ROW_TILE

    def gate_spec(t):
        def index(j, i):
            tile = i * n_gates + t
            pos = tile % tiles_per_batch
            return (jnp.where(pos == tiles_per_batch - 1, nb, tile // tiles_per_batch), 0, j)
        return pl.BlockSpec((None, 1, tn), index)

    return pl.pallas_call(
        functools.partial(_mm_resid_kernel, n_parts=n_parts, n_gates=n_gates),
        grid=(n // tn, rows // tm),
        in_specs=([pl.BlockSpec((tm, kg), lambda j, i: (i, 0))] * n_parts
                  + [pl.BlockSpec((kg, tn), lambda j, i, g=g: (g, j)) for g in range(n_parts)]
                  + [pl.BlockSpec((tm, tn), lambda j, i: (i, j))]
                  + [gate_spec(t) for t in range(n_gates)]),
        out_specs=pl.BlockSpec((tm, tn), lambda j, i: (i, j)),
        out_shape=jax.ShapeDtypeStruct((rows, n), F32),
        compiler_params=_cparams(("parallel", "parallel")),
        name="matmul_resid",
    )(*parts, *([w] * n_parts), res, *([gate] * n_gates))


_NT = (((1,), (1,)), ((), ()))


def _online_softmax(qs, k_ref, v_ref, ck):
    n_rows = qs.shape[0]
    m = jnp.full((n_rows, 1), -jnp.inf, F32)
    acc = jnp.zeros((n_rows, v_ref.shape[1]), F32)
    for c in range(k_ref.shape[0] // ck):
        keys = slice(c * ck, (c + 1) * ck)
        s = lax.dot_general(qs, k_ref[keys, :], _NT, preferred_element_type=F32)
        m_new = jnp.maximum(m, jnp.max(s, axis=-1, keepdims=True))
        p = jnp.exp2(s - m_new).astype(BF16)
        acc = jnp.exp2(m - m_new) * acc + jnp.dot(p, v_ref[keys, :], preferred_element_type=F32)
        m = m_new
    return acc


def _attend_rows(body, pre_args, pre_specs, q, k, v1, n_heads, qw, dk, ow, tq, n_lat, name):
    nb, s_len, _ = q.shape
    n_ctx = s_len - n_lat
    dvp = v1.shape[2] // n_heads
    out_shape = jax.ShapeDtypeStruct((nb, s_len, n_heads * ow), BF16)

    def run(tq, n_q, q_off, k_rows, k_off, prev):
        kern = functools.partial(body, ck=_pick(k_rows, ATTN_KEYS))
        specs = pre_specs + [
            pl.BlockSpec((None, tq, qw), lambda b, h, i: (b, i + q_off, h)),
            pl.BlockSpec((None, k_rows, dk), lambda b, h, i: (b, k_off, h)),
            pl.BlockSpec((None, k_rows, dvp), lambda b, h, i: (b, k_off, h)),
        ]
        args = pre_args + [q, k, v1, prev]
        return pl.pallas_call(
            lambda *refs: kern(*refs[:-2], refs[-1]),
            grid=(nb, n_heads, n_q),
            in_specs=specs + [pl.BlockSpec(memory_space=pl.ANY)],
            out_specs=pl.BlockSpec((None, tq, ow), lambda b, h, i: (b, i + q_off, h)),
            out_shape=out_shape,
            input_output_aliases={len(args) - 1: 0},
            compiler_params=_cparams(("parallel", "parallel", "parallel")),
            name=name,
        )(*args)

    tq = math.gcd(tq, n_lat)
    out = run(tq, n_lat // tq, 0, s_len, 0, jnp.zeros(out_shape.shape, out_shape.dtype))
    return run(n_ctx, 1, n_lat // n_ctx, n_ctx, n_lat // n_ctx, out)


def _attn_kernel(q_ref, k_ref, v_ref, o_ref, *, pieces, dv, ck):
    q = q_ref[...]
    tq, dk = q.shape[0], q.shape[1] // pieces
    qs = q if pieces == 1 else jnp.concatenate([q[:, g * dk:(g + 1) * dk] for g in range(pieces)], axis=0)
    acc = _online_softmax(qs, k_ref, v_ref, ck)
    o = acc[:, :dv] / acc[:, dv:dv + 1]
    for g in range(pieces):
        o_ref[:, g * dv:(g + 1) * dv] = o[g * tq:(g + 1) * tq].astype(o_ref.dtype)


def _attention(q, k, v1, n_kv_heads, group, dk, dv, tq, n_lat):
    return _attend_rows(functools.partial(_attn_kernel, pieces=group, dv=dv), [], [], q, k, v1, n_kv_heads,
                        group * dk, dk, group * dv, tq, n_lat, "attention")


def _diff_attn_kernel(lam_ref, g_ref, q_ref, k_ref, v_ref, o_ref, *, dv, ck, out_scale):
    q = q_ref[...]
    tq, width = q.shape
    first = lax.broadcasted_iota(jnp.int32, (tq, width), 1) < width // 2
    zero = jnp.zeros_like(q)
    qs = jnp.concatenate([jnp.where(first, q, zero), jnp.where(first, zero, q)], axis=0)
    acc = _online_softmax(qs, k_ref, v_ref, ck)
    o = acc[:, :dv] / acc[:, dv:dv + 1]
    o = o[:tq] - lam_ref[0] * o[tq:]
    o = o * lax.rsqrt(jnp.mean(o * o, axis=-1, keepdims=True) + SUBLN_EPS) * g_ref[...]
    o_ref[...] = (o * out_scale).astype(o_ref.dtype)


def _diff_attention(q, k, v1, lam, subln_g, n_heads, out_scale, tq, n_lat):
    dh = q.shape[2] // n_heads
    return _attend_rows(functools.partial(_diff_attn_kernel, dv=dh, out_scale=out_scale),
                        [lam.reshape(1).astype(F32), subln_g.reshape(1, dh)],
                        [pl.BlockSpec(memory_space=pltpu.SMEM), pl.BlockSpec((1, dh), lambda b, h, i: (0, 0))],
                        q, k, v1, n_heads, dh, dh, dh, tq, n_lat, "diff_attention")


def _split3(x):
    h = x.astype(BF16)
    r = x - h.astype(F32)
    m = r.astype(BF16)
    return h, m, (r - m.astype(F32)).astype(BF16)


def _rwkv_chunk(r, v, kk, wl, kd, bb, h_state, rev):
    c, w = r.shape
    nh = w // RWKV_HEAD
    ri = lax.broadcasted_iota(jnp.int32, (c, c), 0)
    ci = lax.broadcasted_iota(jnp.int32, (c, c), 1)
    before_incl = (ci >= ri) if rev else (ci <= ri)
    tri = jnp.where(before_incl, 1.0, 0.0).astype(BF16)
    w_h, w_m, w_l = _split3(wl)
    cum = (jnp.dot(tri, w_h, preferred_element_type=F32) + jnp.dot(tri, w_m, preferred_element_type=F32)
           + jnp.dot(tri, w_l, preferred_element_type=F32))
    cum_prev = cum - wl
    last = 0 if rev else c - 1
    total = cum[last:last + 1, :]
    mid = cum[c // 2:c // 2 + 1, :]
    a = -kk
    e_pm = jnp.exp(cum_prev - mid)
    e_cm = jnp.exp(cum - mid)
    e_mc = jnp.exp(mid - cum)
    e_end = jnp.exp(total - cum)

    lane_head = lax.broadcasted_iota(jnp.int32, (c, w), 1) // RWKV_HEAD

    def stack(x):
        return jnp.concatenate([jnp.where(lane_head == h, x, 0.0) for h in range(nh)], axis=0)

    def unstack(x):
        out = x[0:c]
        for h in range(1, nh):
            out = out + x[h * c:(h + 1) * c]
        return out

    l_a, l_r = stack(a * e_pm), stack(r * e_cm)
    r_b, r_k = stack(bb * e_mc), stack(kd * e_mc)
    v_st = stack(v)
    n = nh * c
    rn = lax.broadcasted_iota(jnp.int32, (n, n), 0)
    cn = lax.broadcasted_iota(jnp.int32, (n, n), 1)
    tr, ts = rn % c, cn % c
    strict = (ts > tr) if rev else (ts < tr)
    incl = (ts >= tr) if rev else (ts <= tr)
    nt = (((1,), (1,)), ((), ()))
    tn = (((0,), (0,)), ((), ()))

    def dot_nt(x, y):
        return lax.dot_general(x.astype(BF16), y.astype(BF16), nt, preferred_element_type=F32)

    def dot_tn(x, y):
        return lax.dot_general(x.astype(BF16), y.astype(BF16), tn, preferred_element_type=F32)

    def dot(x, y):
        return jnp.dot(x.astype(BF16), y.astype(BF16), preferred_element_type=F32)

    n_ab = jnp.where(strict, dot_nt(l_a, r_b), 0.0)
    a_ak = jnp.where(strict, dot_nt(l_a, r_k), 0.0)
    a_rb = jnp.where(incl, dot_nt(l_r, r_b), 0.0)
    a_rk = jnp.where(incl, dot_nt(l_r, r_k), 0.0)

    eye = jnp.where(rn == cn, 1.0, 0.0)
    t_inv = eye + n_ab
    pw = n_ab
    for _ in range(int(math.log2(c)) - 1):
        pw = dot(pw, pw)
        t_inv = t_inv + dot(pw, t_inv)

    l_a0 = stack(a * jnp.exp(cum_prev))
    b_end, k_end = stack(bb * e_end), stack(kd * e_end)
    u = dot(t_inv, dot(l_a0, h_state) + dot(a_ak, v_st))
    y = unstack(dot(a_rb, u) + dot(a_rk, v_st)) + dot(r * jnp.exp(cum), h_state)
    rw = lax.broadcasted_iota(jnp.int32, (w, w), 0)
    cw = lax.broadcasted_iota(jnp.int32, (w, w), 1)
    decay_end = jnp.where(rw == cw, jnp.broadcast_to(jnp.exp(total), (w, w)), 0.0)
    return y, dot(decay_end, h_state) + dot_tn(b_end, u) + dot_tn(k_end, v_st)


def _rwkv_scan_kernel(rf_ref, vf_ref, kkf_ref, wlf_ref, kdf_ref, bbf_ref,
                      rr_ref, vr_ref, kkr_ref, wlr_ref, kdr_ref, bbr_ref,
                      yf_ref, yr_ref, hf_ref, hr_ref):
    @pl.when(pl.program_id(2) == 0)
    def _():
        hf_ref[...] = jnp.zeros_like(hf_ref)
        hr_ref[...] = jnp.zeros_like(hr_ref)

    qw = RWKV_QUAD * RWKV_HEAD
    for i in range(hf_ref.shape[0]):
        sl = slice(i * qw, (i + 1) * qw)
        y, h = _rwkv_chunk(rf_ref[:, sl], vf_ref[:, sl], kkf_ref[:, sl], wlf_ref[:, sl], kdf_ref[:, sl],
                           bbf_ref[:, sl], hf_ref[i], rev=False)
        yf_ref[:, sl] = y
        hf_ref[i] = h
        y, h = _rwkv_chunk(rr_ref[:, sl], vr_ref[:, sl], kkr_ref[:, sl], wlr_ref[:, sl], kdr_ref[:, sl],
                           bbr_ref[:, sl], hr_ref[i], rev=True)
        yr_ref[:, sl] = y
        hr_ref[i] = h


def _rwkv_scan(r, v, kk, wl, kd, bb, n_lat):
    nb, s_len, w = r.shape
    c = RWKV_CHUNK
    qw = RWKV_QUAD * RWKV_HEAD
    nq = RWKV_STEP_QUADS if w % (RWKV_STEP_QUADS * qw) == 0 else 1
    bw = nq * qw
    n_l, n_all = n_lat // c, s_len // c
    n_c = n_all - n_l

    def fwd(s):
        return jnp.where(s < n_c, n_l + s, s - n_c)

    def rev(s):
        return n_all - 1 - s

    def shared(order):
        return pl.BlockSpec((None, c, bw), lambda b, q, s: (b, order(s), q))

    def per_dir(d, order):
        return pl.BlockSpec((None, None, c, bw), lambda b, q, s: (d, b, order(s), q))

    yf, yr = pl.pallas_call(
        _rwkv_scan_kernel,
        grid=(nb, w // bw, n_all),
        in_specs=[shared(fwd), shared(fwd), shared(fwd), per_dir(0, fwd), per_dir(0, fwd), per_dir(0, fwd),
                  shared(rev), shared(rev), shared(rev), per_dir(1, rev), per_dir(1, rev), per_dir(1, rev)],
        out_specs=[shared(fwd), shared(rev)],
        out_shape=[jax.ShapeDtypeStruct((nb, s_len, w), F32)] * 2,
        scratch_shapes=[pltpu.VMEM((nq, qw, qw), F32), pltpu.VMEM((nq, qw, qw), F32)],
        compiler_params=_cparams(("parallel", "parallel", "arbitrary")),
        name="rwkv_scan",
    )(r, v, kk, wl, kd, bb, r, v, kk, wl, kd, bb)
    return yf, yr


def _group_sum(x, ones_bd):
    hi = x.astype(BF16)
    lo = (x - hi.astype(F32)).astype(BF16)
    return jnp.dot(hi, ones_bd, preferred_element_type=F32) + jnp.dot(lo, ones_bd, preferred_element_type=F32)


def _rwkv_prep_kernel(x_ref, prev_ref, next_ref, mu_ref, kkg_ref, ka_ref, rk_ref, w0_ref, a0_ref, wup_ref, aup_ref,
                      gup_ref, ones_ref, r_ref, v_ref, kk_ref, wl_ref, kd_ref, bb_ref, bonus_ref, g_ref,
                      *, n_lat_tiles, gw, lora_w, lora_a):
    i = pl.program_id(1)
    x = x_ref[...]
    rows = x.shape[0]
    row = lax.broadcasted_iota(jnp.int32, x.shape, 0)
    has_prev = jnp.logical_and(i > 0, i < n_lat_tiles)
    has_next = i < n_lat_tiles - 1
    prev_row = jnp.where(has_prev, prev_ref[7:8, :], 0.0)
    next_row = jnp.where(has_next, next_ref[0:1, :], 0.0)
    prev = jnp.where(row == 0, prev_row, pltpu.roll(x, 1, axis=0))
    nxt = jnp.where(row == rows - 1, next_row, pltpu.roll(x, rows - 1, axis=0))
    ps = x + mu_ref[...] * (0.5 * (prev + nxt) - x)
    r, k, v = ps[:, :gw], ps[:, gw:2 * gw], ps[:, 2 * gw:3 * gw]
    o = 3 * gw
    wd, ad, gd = ps[:, o:o + lora_w], ps[:, o + lora_w:o + lora_w + lora_a], ps[:, o + lora_w + lora_a:]
    ones_bd = ones_ref[...]
    kk = k * kkg_ref[...]
    kk = kk * lax.rsqrt(jnp.maximum(_group_sum(kk * kk, ones_bd), 1e-24))
    tanh_wd = jnp.tanh(wd).astype(BF16)
    ad_b = ad.astype(BF16)
    kd_sum = jnp.zeros_like(k)
    for d in range(2):
        z = w0_ref[d] + jnp.dot(tanh_wd, wup_ref[d], preferred_element_type=F32)
        softplus_neg = jnp.maximum(-z, 0.0) + jnp.log1p(jnp.exp(-jnp.abs(z)))
        wl_ref[d] = -jnp.exp(-softplus_neg - 0.5)
        a_sig = jax.nn.sigmoid(a0_ref[d] + jnp.dot(ad_b, aup_ref[d], preferred_element_type=F32))
        kd = k * (1.0 + (a_sig - 1.0) * ka_ref[...])
        kd_ref[d] = kd
        bb_ref[d] = kk * a_sig
        kd_sum = kd_sum + kd
    r_ref[...] = r
    v_ref[...] = v
    kk_ref[...] = kk
    bonus_ref[...] = _group_sum(r * kd_sum * rk_ref[...], ones_bd) * v
    g_ref[...] = jnp.dot(jax.nn.sigmoid(gd).astype(BF16), gup_ref[...], preferred_element_type=F32)


def _head_ones(width, head):
    idx = jnp.arange(width) // head
    return (idx[:, None] == idx[None, :]).astype(BF16)


def _rwkv_prep(pa, n_lat_tiles, mu, w0, w_up, a0, a_up, g_up, k_k, k_a, r_k):
    nb, s_len, wp = pa.shape
    gw = k_k.shape[0]
    lora_w, lora_a = w_up.shape[1], a_up.shape[1]
    lora_g = wp - 3 * gw - lora_w - lora_a
    n_tiles = s_len // ROW_TILE
    sub = ROW_TILE // 8
    row = lambda z: z.reshape(1, -1)
    full = lambda shape: pl.BlockSpec(shape, lambda b, i: (0,) * len(shape))
    tile = lambda w: pl.BlockSpec((None, ROW_TILE, w), lambda b, i: (b, i, 0))
    tile2 = pl.BlockSpec((2, None, ROW_TILE, gw), lambda b, i: (0, b, i, 0))
    one = jax.ShapeDtypeStruct((nb, s_len, gw), F32)
    two = jax.ShapeDtypeStruct((2, nb, s_len, gw), F32)
    return pl.pallas_call(
        functools.partial(_rwkv_prep_kernel, n_lat_tiles=n_lat_tiles, gw=gw, lora_w=lora_w, lora_a=lora_a),
        grid=(nb, n_tiles),
        in_specs=[
            tile(wp),
            pl.BlockSpec((None, 8, wp), lambda b, i: (b, jnp.maximum(i * sub - 1, 0), 0)),
            pl.BlockSpec((None, 8, wp), lambda b, i: (b, jnp.minimum((i + 1) * sub, s_len // 8 - 1), 0)),
            full((1, wp)), full((1, gw)), full((1, gw)), full((1, gw)),
            full((2, 1, gw)), full((2, 1, gw)),
            full((2, lora_w, gw)), full((2, lora_a, gw)), full((lora_g, gw)), full((gw, gw)),
        ],
        out_specs=[tile(gw), tile(gw), tile(gw), tile2, tile2, tile2, tile(gw), tile(gw)],
        out_shape=[one, one, one, two, two, two, one, one],
        compiler_params=_cparams(("parallel", "parallel")),
        name="rwkv_prep",
    )(pa, pa, pa, row(jnp.pad(mu, (0, wp - mu.shape[0]))), row(k_k), row(k_a), row(r_k),
      w0.reshape(2, 1, gw), a0.reshape(2, 1, gw), w_up.astype(BF16), a_up.astype(BF16),
      jnp.pad(g_up, ((0, lora_g - g_up.shape[0]), (0, 0))).astype(BF16), _head_ones(gw, RWKV_HEAD))


def _rwkv_out_kernel(yf_ref, yr_ref, bonus_ref, g_ref, lw_ref, lb_ref, ones_ref, o_ref):
    ones_bd = ones_ref[...]
    y = yf_ref[...] + yr_ref[...]
    inv_n = 1.0 / RWKV_HEAD
    dev = y - _group_sum(y, ones_bd) * inv_n
    var = _group_sum(dev * dev, ones_bd) * inv_n
    yn = dev * lax.rsqrt(var + GN_EPS) * lw_ref[...] + lb_ref[...]
    o_ref[...] = ((yn + bonus_ref[...]) * g_ref[...]).astype(o_ref.dtype)


def _rwkv_out(yf, yr, bonus, g, lnx_w, lnx_b):
    nb, s_len, gw = yf.shape
    tile = pl.BlockSpec((None, ROW_TILE, gw), lambda b, i: (b, i, 0))
    vec = pl.BlockSpec((1, gw), lambda b, i: (0, 0))
    return pl.pallas_call(
        _rwkv_out_kernel,
        grid=(nb, s_len // ROW_TILE),
        in_specs=[tile, tile, tile, tile, vec, vec, pl.BlockSpec((gw, gw), lambda b, i: (0, 0))],
        out_specs=tile,
        out_shape=jax.ShapeDtypeStruct((nb, s_len, gw), BF16),
        compiler_params=_cparams(("parallel", "parallel")),
        name="rwkv_out",
    )(yf, yr, bonus, g, lnx_w.reshape(1, gw), lnx_b.reshape(1, gw), _head_ones(gw, RWKV_HEAD))


def _rope_block(xb, cos_t, sin_t, half):
    if 2 * half == LANES:
        partner = pltpu.roll(xb, half, axis=1)
    else:
        lane = lax.broadcasted_iota(jnp.int32, xb.shape, 1)
        partner = jnp.where(lane % (2 * half) < half, pltpu.roll(xb, LANES - half, axis=1),
                            pltpu.roll(xb, half, axis=1))
    return xb * cos_t + partner * sin_t


def _lane_tables(cs):
    cos, sin = cs
    reps = LANES // (2 * cos.shape[1])
    return (jnp.tile(jnp.concatenate([cos, cos], axis=1), (1, reps)),
            jnp.tile(jnp.concatenate([-sin, sin], axis=1), (1, reps)))


def _ones_column(rows, dtype):
    return jnp.where(lax.broadcasted_iota(jnp.int32, (rows, LANES), 1) == 0, 1.0, 0.0).astype(dtype)


def _blk(b):
    return slice(b * LANES, (b + 1) * LANES)


def _diff_prep_kernel(x_ref, cos_ref, sin_ref, q_ref, k_ref, v_ref, *, gw, half, q_scale):
    cos_t, sin_t = cos_ref[...], sin_ref[...]
    e0 = _ones_column(x_ref.shape[0], v_ref.dtype)
    nblk = gw // LANES
    for b in range(nblk):
        q_ref[:, _blk(b)] = (_rope_block(x_ref[:, _blk(b)], cos_t, sin_t, half) * q_scale).astype(q_ref.dtype)
        k_ref[:, _blk(b)] = _rope_block(x_ref[:, _blk(nblk + b)], cos_t, sin_t, half).astype(k_ref.dtype)
        v_ref[:, _blk(2 * b)] = x_ref[:, _blk(2 * nblk + b)].astype(v_ref.dtype)
        v_ref[:, _blk(2 * b + 1)] = e0


def _diff_prep(pb, tables, half, q_scale):
    nb, s_len, w3 = pb.shape
    gw = w3 // 3
    tile = lambda w: pl.BlockSpec((None, ROW_TILE, w), lambda b, i: (b, i, 0))
    tab = pl.BlockSpec((ROW_TILE, LANES), lambda b, i: (i, 0))
    return pl.pallas_call(
        functools.partial(_diff_prep_kernel, gw=gw, half=half, q_scale=q_scale),
        grid=(nb, s_len // ROW_TILE),
        in_specs=[tile(w3), tab, tab],
        out_specs=[tile(gw), tile(gw), tile(2 * gw)],
        out_shape=[jax.ShapeDtypeStruct((nb, s_len, gw), BF16), jax.ShapeDtypeStruct((nb, s_len, gw), BF16),
                   jax.ShapeDtypeStruct((nb, s_len, 2 * gw), BF16)],
        compiler_params=_cparams(("parallel", "parallel")),
        name="diff_prep",
    )(pb, *tables)


def _gqa_prep_kernel(x_ref, cos_ref, sin_ref, qg_ref, kg_ref, q_ref, k_ref, v_ref, *, n_heads, n_kv, q_scale):
    cos_t, sin_t = cos_ref[...], sin_ref[...]
    e0 = _ones_column(x_ref.shape[0], v_ref.dtype)

    def normed_rope(xb, g):
        xb = xb * lax.rsqrt(jnp.mean(xb * xb, axis=-1, keepdims=True) + NORM_EPS) * g
        return _rope_block(xb, cos_t, sin_t, LANES // 2)

    for b in range(n_heads):
        q_ref[:, _blk(b)] = (normed_rope(x_ref[:, _blk(b)], qg_ref[...]) * q_scale).astype(q_ref.dtype)
    for b in range(n_kv):
        k_ref[:, _blk(b)] = normed_rope(x_ref[:, _blk(n_heads + b)], kg_ref[...]).astype(k_ref.dtype)
        v_ref[:, _blk(2 * b)] = x_ref[:, _blk(n_heads + n_kv + b)].astype(v_ref.dtype)
        v_ref[:, _blk(2 * b + 1)] = e0


def _gqa_prep(pc, tables, q_norm_g, k_norm_g, n_heads, n_kv, q_scale):
    nb, s_len, w = pc.shape
    tile = lambda w_: pl.BlockSpec((None, ROW_TILE, w_), lambda b, i: (b, i, 0))
    tab = pl.BlockSpec((ROW_TILE, LANES), lambda b, i: (i, 0))
    vec = pl.BlockSpec((1, LANES), lambda b, i: (0, 0))
    return pl.pallas_call(
        functools.partial(_gqa_prep_kernel, n_heads=n_heads, n_kv=n_kv, q_scale=q_scale),
        grid=(nb, s_len // ROW_TILE),
        in_specs=[tile(w), tab, tab, vec, vec],
        out_specs=[tile(n_heads * LANES), tile(n_kv * LANES), tile(2 * n_kv * LANES)],
        out_shape=[jax.ShapeDtypeStruct((nb, s_len, n_heads * LANES), BF16),
                   jax.ShapeDtypeStruct((nb, s_len, n_kv * LANES), BF16),
                   jax.ShapeDtypeStruct((nb, s_len, 2 * n_kv * LANES), BF16)],
        compiler_params=_cparams(("parallel", "parallel")),
        name="gqa_prep",
    )(pc, *tables, q_norm_g.reshape(1, LANES), k_norm_g.reshape(1, LANES))


def _mla_prep_kernel(x_ref, cos_ref, sin_ref, qg_ref, kvg_ref, qup_ref, kvup_ref, q_ref, k_ref, v_ref,
                     *, q_lora, kv_lora, n_heads, half, q_scale):
    cos_t, sin_t = cos_ref[...], sin_ref[...]
    e0 = _ones_column(x_ref.shape[0], v_ref.dtype)

    def normed(z, g):
        return (z * lax.rsqrt(jnp.mean(z * z, axis=-1, keepdims=True) + NORM_EPS) * g).astype(BF16)

    q = jnp.dot(normed(x_ref[:, :q_lora], qg_ref[...]), qup_ref[...], preferred_element_type=F32)
    kv = jnp.dot(normed(x_ref[:, q_lora:q_lora + kv_lora], kvg_ref[...]), kvup_ref[...], preferred_element_type=F32)
    k_rope = _rope_block(x_ref[:, q_lora + kv_lora:q_lora + kv_lora + LANES], cos_t, sin_t, half).astype(k_ref.dtype)
    for h in range(n_heads):
        q_ref[:, _blk(2 * h)] = (q[:, _blk(2 * h)] * q_scale).astype(q_ref.dtype)
        q_ref[:, _blk(2 * h + 1)] = (_rope_block(q[:, _blk(2 * h + 1)], cos_t, sin_t, half) * q_scale).astype(q_ref.dtype)
        k_ref[:, _blk(2 * h)] = kv[:, _blk(h)].astype(k_ref.dtype)
        k_ref[:, _blk(2 * h + 1)] = k_rope
        v_ref[:, _blk(2 * h)] = kv[:, _blk(n_heads + h)].astype(v_ref.dtype)
        v_ref[:, _blk(2 * h + 1)] = e0


def _mla_prep(pd, tables, q_norm_g, q_up, kv_norm_g, kv_up, n_heads, d_rope, q_scale):
    nb, s_len, wp = pd.shape
    q_lora, kv_lora = q_norm_g.shape[0], kv_norm_g.shape[0]
    d_qk = q_up.shape[1] // n_heads
    d_nope = d_qk - d_rope
    assert d_nope == LANES and kv_up.shape[1] == n_heads * 2 * LANES and q_lora + kv_lora + LANES <= wp
    qw = jnp.pad(q_up.reshape(q_lora, n_heads, d_qk), ((0, 0), (0, 0), (0, 2 * LANES - d_qk)))
    qw = qw.reshape(q_lora, n_heads * 2 * LANES).astype(BF16)
    kvw = kv_up.reshape(kv_lora, n_heads, 2, LANES).transpose(0, 2, 1, 3).reshape(kv_lora, 2 * n_heads * LANES)
    tile = lambda w_: pl.BlockSpec((None, ROW_TILE, w_), lambda b, i: (b, i, 0))
    tab = pl.BlockSpec((ROW_TILE, LANES), lambda b, i: (i, 0))
    full = lambda shape: pl.BlockSpec(shape, lambda b, i: (0,) * len(shape))
    wide = n_heads * 2 * LANES
    out = jax.ShapeDtypeStruct((nb, s_len, wide), BF16)
    return pl.pallas_call(
        functools.partial(_mla_prep_kernel, q_lora=q_lora, kv_lora=kv_lora, n_heads=n_heads, half=d_rope // 2,
                          q_scale=q_scale),
        grid=(nb, s_len // ROW_TILE),
        in_specs=[tile(wp), tab, tab, full((1, q_lora)), full((1, kv_lora)), full((q_lora, wide)),
                  full((kv_lora, wide))],
        out_specs=[tile(wide), tile(wide), tile(wide)],
        out_shape=[out, out, out],
        compiler_params=_cparams(("parallel", "parallel")),
        name="mla_prep",
    )(pd, *tables, q_norm_g.reshape(1, q_lora), kv_norm_g.reshape(1, kv_lora), qw, kvw.astype(BF16))


def _moe_gather_kernel(src_ref, used_ref, tok_ref, o_ref, buf_ref, sem_ref):
    base = pl.program_id(0) * MOE_ROWS

    def copy(r):
        return pltpu.make_async_copy(tok_ref.at[pl.ds(src_ref[base + r], 1), :], buf_ref.at[pl.ds(r, 1), :],
                                     sem_ref.at[0])

    def start(r, carry):
        copy(r).start()
        return carry

    def wait(r, carry):
        copy(r).wait()
        return carry

    @pl.when(pl.program_id(0) < used_ref[0])
    def _():
        lax.fori_loop(0, MOE_ROWS, start, 0)
        lax.fori_loop(0, MOE_ROWS, wait, 0)
        o_ref[...] = buf_ref[...].astype(o_ref.dtype)

    @pl.when(pl.program_id(0) >= used_ref[0])
    def _():
        o_ref[...] = jnp.zeros_like(o_ref)


def _used_block(i, used):
    return jnp.minimum(i, used[0] - 1)


def _moe_gather(tok, src, used, n_blocks):
    d = tok.shape[1]
    return pl.pallas_call(
        _moe_gather_kernel,
        grid_spec=pltpu.PrefetchScalarGridSpec(
            num_scalar_prefetch=2,
            grid=(n_blocks,),
            in_specs=[pl.BlockSpec(memory_space=pl.ANY)],
            out_specs=pl.BlockSpec((MOE_ROWS, d), lambda i, src, used: (i, 0)),
            scratch_shapes=[pltpu.VMEM((MOE_ROWS, d), F32), pltpu.SemaphoreType.DMA((1,))],
        ),
        out_shape=jax.ShapeDtypeStruct((n_blocks * MOE_ROWS, d), BF16),
        compiler_params=_cparams(("arbitrary",)),
        name="moe_gather",
    )(src, used, tok)


def _moe_up_kernel(be_ref, used_ref, a_ref, wg_ref, wu_ref, o_ref):
    @pl.when(pl.program_id(0) < used_ref[0])
    def _():
        a = a_ref[...]
        g = jnp.dot(a, wg_ref[...], preferred_element_type=F32)
        u = jnp.dot(a, wu_ref[...], preferred_element_type=F32)
        o_ref[...] = (g * jax.nn.sigmoid(g) * u).astype(o_ref.dtype)

    @pl.when(pl.program_id(0) >= used_ref[0])
    def _():
        o_ref[...] = jnp.zeros_like(o_ref)


def _moe_block_maps(n_col_tiles):
    def col(i, j, used):
        return jnp.where(i < used[0], j, n_col_tiles - 1)
    rows = lambda i, j, be, used: (_used_block(i, used), 0)
    weight = lambda i, j, be, used: (be[_used_block(i, used)], 0, col(i, j, used))
    out = lambda i, j, be, used: (i, j)
    return rows, weight, out


def _moe_up(buf, block_expert, used, wg, wu, tn_pref=512):
    rows, d = buf.shape
    f = wg.shape[2]
    tn = _pick(f, tn_pref)
    n_blocks = rows // MOE_ROWS
    row_map, w_map, out_map = _moe_block_maps(f // tn)
    return pl.pallas_call(
        _moe_up_kernel,
        grid_spec=pltpu.PrefetchScalarGridSpec(
            num_scalar_prefetch=2,
            grid=(n_blocks, f // tn),
            in_specs=[
                pl.BlockSpec((MOE_ROWS, d), row_map),
                pl.BlockSpec((None, d, tn), w_map),
                pl.BlockSpec((None, d, tn), w_map),
            ],
            out_specs=pl.BlockSpec((MOE_ROWS, tn), out_map),
        ),
        out_shape=jax.ShapeDtypeStruct((rows, f), BF16),
        compiler_params=_cparams(("arbitrary", "arbitrary")),
        name="moe_up",
    )(block_expert, used, buf, wg, wu)


def _moe_down_kernel(be_ref, used_ref, h_ref, w_ref, o_ref):
    @pl.when(pl.program_id(0) < used_ref[0])
    def _():
        o_ref[...] = jnp.dot(h_ref[...], w_ref[...], preferred_element_type=F32)

    @pl.when(pl.program_id(0) >= used_ref[0])
    def _():
        o_ref[...] = jnp.zeros_like(o_ref)


def _moe_down(hid, block_expert, used, wd, tn_pref=512):
    rows, f = hid.shape
    d = wd.shape[2]
    tn = _pick(d, tn_pref)
    n_blocks = rows // MOE_ROWS
    row_map, w_map, out_map = _moe_block_maps(d // tn)
    return pl.pallas_call(
        _moe_down_kernel,
        grid_spec=pltpu.PrefetchScalarGridSpec(
            num_scalar_prefetch=2,
            grid=(n_blocks, d // tn),
            in_specs=[
                pl.BlockSpec((MOE_ROWS, f), row_map),
                pl.BlockSpec((None, f, tn), w_map),
            ],
            out_specs=pl.BlockSpec((MOE_ROWS, tn), out_map),
        ),
        out_shape=jax.ShapeDtypeStruct((rows, d), F32),
        compiler_params=_cparams(("arbitrary", "arbitrary")),
        name="moe_down",
    )(block_expert, used, hid, wd)


def _moe_combine_kernel(dest_ref, y_ref, x_ref, gate_ref, g0_ref, g1_ref, o_ref, b0_ref, b1_ref, sem_ref):
    base = (pl.program_id(0) * pl.num_programs(1) + pl.program_id(1)) * ROW_TILE

    def copies(r):
        t = (base + r) * TOP_K
        return (pltpu.make_async_copy(y_ref.at[pl.ds(dest_ref[t], 1), :], b0_ref.at[pl.ds(r, 1), :], sem_ref.at[0]),
                pltpu.make_async_copy(y_ref.at[pl.ds(dest_ref[t + 1], 1), :], b1_ref.at[pl.ds(r, 1), :],
                                      sem_ref.at[1]))

    def start(r, carry):
        c0, c1 = copies(r)
        c0.start()
        c1.start()
        return carry

    def wait(r, carry):
        c0, c1 = copies(r)
        c0.wait()
        c1.wait()
        return carry

    lax.fori_loop(0, ROW_TILE, start, 0)
    lax.fori_loop(0, ROW_TILE, wait, 0)
    o_ref[...] = x_ref[...] + gate_ref[...] * (g0_ref[...] * b0_ref[...] + g1_ref[...] * b1_ref[...])


def _moe_combine(y, dest, x, gate, g0, g1):
    nb, _, d = x.shape
    t = g0.shape[1]
    return pl.pallas_call(
        _moe_combine_kernel,
        grid_spec=pltpu.PrefetchScalarGridSpec(
            num_scalar_prefetch=1,
            grid=(nb, t // ROW_TILE),
            in_specs=[
                pl.BlockSpec(memory_space=pl.ANY),
                pl.BlockSpec((None, ROW_TILE, d), lambda b, i, dest: (b, i, 0)),
                pl.BlockSpec((None, 1, d), lambda b, i, dest: (b, 0, 0)),
                pl.BlockSpec((None, ROW_TILE, 1), lambda b, i, dest: (b, i, 0)),
                pl.BlockSpec((None, ROW_TILE, 1), lambda b, i, dest: (b, i, 0)),
            ],
            out_specs=pl.BlockSpec((None, ROW_TILE, d), lambda b, i, dest: (b, i, 0)),
            scratch_shapes=[pltpu.VMEM((ROW_TILE, d), F32), pltpu.VMEM((ROW_TILE, d), F32),
                            pltpu.SemaphoreType.DMA((2,))],
        ),
        out_shape=jax.ShapeDtypeStruct((nb, t, d), F32),
        compiler_params=_cparams(("arbitrary", "arbitrary")),
        name="moe_combine",
    )(dest, y, x, gate, g0, g1)


def _moe_layer(a_tok, logits, x, gate, wg, wu, wd):
    nb, t, d = a_tok.shape
    n_exp = wg.shape[0]
    n_tok = nb * t
    n_assign = n_tok * TOP_K
    top_logit, top_idx = lax.top_k(logits.reshape(n_tok, -1)[:, :n_exp], TOP_K)
    gates = jax.nn.softmax(top_logit, axis=-1)
    flat_e = top_idx.reshape(-1)
    onehot = (flat_e[:, None] == jnp.arange(n_exp)[None, :]).astype(jnp.int32)
    rank = jnp.take_along_axis(jnp.cumsum(onehot, axis=0) - onehot, flat_e[:, None], axis=1)[:, 0]
    counts = jnp.sum(onehot, axis=0)
    padded = (counts + MOE_ROWS - 1) // MOE_ROWS * MOE_ROWS
    pad_end = jnp.cumsum(padded)
    pad_start = pad_end - padded
    dest = (pad_start[flat_e] + rank).astype(jnp.int32)
    n_blocks = -(-n_assign // MOE_ROWS) + n_exp
    src = jnp.zeros((n_blocks * MOE_ROWS,), jnp.int32).at[dest].set(jnp.arange(n_assign, dtype=jnp.int32) // TOP_K)
    block_expert = jnp.minimum(
        jnp.searchsorted(pad_end, jnp.arange(n_blocks) * MOE_ROWS, side='right'), n_exp - 1).astype(jnp.int32)
    used = (pad_end[-1:] // MOE_ROWS).astype(jnp.int32)
    buf = _moe_gather(a_tok.reshape(n_tok, d), src, used, n_blocks)
    hid = _moe_up(buf, block_expert, used, wg, wu)
    y = _moe_down(hid, block_expert, used, wd)
    return _moe_combine(y, dest, x, gate, gates[:, 0].reshape(nb, t, 1), gates[:, 1].reshape(nb, t, 1))


def _rope_tables(n_lat, n_ctx, rot_dim):
    rows = n_lat // GRID_W
    row = jnp.repeat(jnp.arange(rows), GRID_W).astype(F32)
    col = jnp.tile(jnp.arange(GRID_W), rows).astype(F32)
    n_freq = rot_dim // 4
    inv = ROPE_THETA ** (-jnp.arange(n_freq, dtype=F32) / n_freq)
    ang = jnp.concatenate([row[:, None] * inv, col[:, None] * inv], axis=-1)
    ang = jnp.concatenate([ang, jnp.zeros((n_ctx, rot_dim // 2), F32)], axis=0)
    return jnp.cos(ang), jnp.sin(ang)


def _rwkv_mixer(pa, n_lat, mu, w0, w_up, a0, a_up, g_up, k_k, k_a, r_k, lnx_w, lnx_b):
    r, v, kk, wl, kd, bb, bonus, g = _rwkv_prep(pa, n_lat // ROW_TILE, mu, w0, w_up, a0, a_up, g_up, k_k, k_a,
                                                 r_k.reshape(-1))
    yf, yr = _rwkv_scan(r, v, kk, wl, kd, bb, n_lat)
    return _rwkv_out(yf, yr, bonus, g, lnx_w, lnx_b)


def _diff_mixer(pb, n_lat, lq1, lk1, lq2, lk2, subln_g, lambda_init, tables):
    dh = subln_g.shape[0] // 2
    nh = pb.shape[2] // 3 // (2 * dh)
    lam = jnp.exp(jnp.sum(lq1 * lk1)) - jnp.exp(jnp.sum(lq2 * lk2)) + lambda_init
    q, k, v1 = _diff_prep(pb, tables, dh // 2, dh ** -0.5 * LOG2E)
    scale = 1.0 - lambda_init
    tq = ATTN_ROWS // 2
    return _diff_attention(q, k, v1, lam, subln_g, nh, scale, tq, n_lat)


def _gqa_mixer(pc, n_lat, q_norm_g, k_norm_g, n_heads, n_kv, tables):
    dh = q_norm_g.shape[0]
    assert dh == LANES
    q, k, v1 = _gqa_prep(pc, tables, q_norm_g, k_norm_g, n_heads, n_kv, dh ** -0.5 * LOG2E)
    group = n_heads // n_kv
    tq = ATTN_ROWS // group
    return _attention(q, k, v1, n_kv, group, dh, dh, tq, n_lat)


def _mla_mixer(pd, n_lat, q_norm_g, q_up, kv_norm_g, kv_up, n_heads, d_rope, tables):
    d_qk = q_up.shape[1] // n_heads
    q, k, v1 = _mla_prep(pd, tables, q_norm_g, q_up, kv_norm_g, kv_up, n_heads, d_rope, d_qk ** -0.5 * LOG2E)
    dk, d_v = 2 * LANES, LANES
    return _attention(q, k, v1, n_heads, 1, dk, d_v, ATTN_ROWS, n_lat)


def _pad_cols(w, mult):
    return jnp.pad(w, ((0, 0), (0, -w.shape[1] % mult)))


def kernel(x, c, ctx, c_ctx, ada_w, ada_b, norm_mix_g, norm_ffn_g, w_in, rwkv_mu, rwkv_w0, rwkv_w_up, rwkv_a0, rwkv_a_up, rwkv_g_up, rwkv_k_k, rwkv_k_a, rwkv_r_k, rwkv_lnx_w, rwkv_lnx_b, diff_lq1, diff_lk1, diff_lq2, diff_lk2, diff_subln_g, gqa_q_norm_g, gqa_k_norm_g, mla_q_norm_g, mla_q_up, mla_kv_norm_g, mla_kv_up, w_out, ffn_w_gate, ffn_w_up, ffn_w_down, moe_router, moe_w_gate, moe_w_up, moe_w_down, final_norm_g):
    nb, n_lat, d = x.shape
    n_ctx = ctx.shape[1]
    depth = ada_w.shape[0]
    assert n_ctx == ROW_TILE and n_lat % ROW_TILE == 0 and n_lat % GRID_W == 0
    n_lat_tiles = n_lat // ROW_TILE
    n_all_tiles = n_lat_tiles + 1

    gw = rwkv_k_k.shape[1]
    cols_a = 3 * gw + rwkv_w_up.shape[2] + rwkv_a_up.shape[2] + rwkv_g_up.shape[1]
    cols_b = 3 * gw
    gqa_dh = gqa_q_norm_g.shape[1]
    mla_rope = 64
    mla_heads = 8
    gqa_heads = gw // gqa_dh
    cols_d = mla_q_norm_g.shape[1] + mla_kv_norm_g.shape[1] + mla_rope
    cols_c = w_in.shape[2] - cols_a - cols_b - cols_d
    gqa_kv = (cols_c - gw) // (2 * gqa_dh)
    offs = (0, cols_a, cols_a + cols_b, cols_a + cols_b + cols_c, w_in.shape[2])

    rope_diff = _lane_tables(_rope_tables(n_lat, n_ctx, diff_subln_g.shape[1] // 2))
    rope_gqa = _lane_tables(_rope_tables(n_lat, n_ctx, gqa_dh))
    rope_mla = _lane_tables(_rope_tables(n_lat, n_ctx, mla_rope))

    cvec = jnp.concatenate([c, c_ctx[None, :], jnp.zeros((8 - nb - 1, d), F32)], axis=0)
    mods = _ada(cvec, ada_w, ada_b)[:, :nb + 1].reshape(depth, nb + 1, 6, 1, d)

    h = jnp.concatenate([x, ctx], axis=1)
    s_len = n_lat + n_ctx
    flat = lambda z: z.reshape(nb * s_len, z.shape[-1])
    for l in range(depth):
        need_ctx = l < depth - 1
        m = [mods[l, :, i] for i in range(6)]

        a = flat(_norm_mod(h, norm_mix_g[l], m[1], m[0], n_all_tiles, n_lat_tiles))
        groups = [_matmul(a, _pad_cols(w_in[l][:, offs[i]:offs[i + 1]], 768).astype(BF16), F32, 768)
                  .reshape(nb, s_len, -1) for i in range(4)]
        oa = _rwkv_mixer(groups[0], n_lat, rwkv_mu[l], rwkv_w0[l], rwkv_w_up[l],
                         rwkv_a0[l], rwkv_a_up[l], rwkv_g_up[l], rwkv_k_k[l], rwkv_k_a[l], rwkv_r_k[l],
                         rwkv_lnx_w[l], rwkv_lnx_b[l])
        lambda_init = 0.8 - 0.6 * math.exp(-0.3 * l)
        ob = _diff_mixer(groups[1], n_lat, diff_lq1[l], diff_lk1[l], diff_lq2[l], diff_lk2[l],
                         diff_subln_g[l], lambda_init, rope_diff)
        oc = _gqa_mixer(groups[2], n_lat, gqa_q_norm_g[l], gqa_k_norm_g[l], gqa_heads, gqa_kv, rope_gqa)
        od = _mla_mixer(groups[3], n_lat, mla_q_norm_g[l], mla_q_up[l], mla_kv_norm_g[l], mla_kv_up[l],
                        mla_heads, mla_rope, rope_mla)
        h = _matmul_resid([flat(o) for o in (oa, ob, oc, od)], w_out[l].astype(BF16), flat(h), m[2],
                          n_all_tiles).reshape(nb, s_len, d)

        j = l // 2
        if l % 2 == 0:
            d_ff = ffn_w_gate.shape[2]
            pad_f = -d_ff % 512
            w_g = jnp.pad(ffn_w_gate[j], ((0, 0), (0, pad_f))).astype(BF16)
            w_u = jnp.pad(ffn_w_up[j], ((0, 0), (0, pad_f))).astype(BF16)
            w_d = jnp.pad(ffn_w_down[j], ((0, pad_f), (0, 0))).astype(BF16)
            a = flat(_norm_mod(h, norm_ffn_g[l], m[4], m[3], n_all_tiles, n_lat_tiles))
            hid = _matmul_swiglu(a, w_g, w_u)
            h = _matmul_resid([hid], w_d, flat(h), m[5], n_all_tiles, tm=ROW_TILE).reshape(nb, s_len, d)
        else:
            assert not need_ctx
            a, logits = _norm_router(h, norm_ffn_g[l], m[4], m[3], moe_router[j], n_lat_tiles, n_lat_tiles)
            h = _moe_layer(a, logits, h, m[5][:nb], moe_w_gate[j].astype(BF16), moe_w_up[j].astype(BF16),
                           moe_w_down[j].astype(BF16))
    return _final_norm(h, final_norm_g, n_lat_tiles)
```

```python
import functools
import itertools
import math

import jax
import jax.numpy as jnp
from jax import lax
from jax.experimental import pallas as pl
from jax.experimental.pallas import tpu as pltpu

F32 = jnp.float32
BF16 = jnp.bfloat16

GRID_W = 64
ROPE_THETA = 10000.0
NORM_EPS = 1e-6
SUBLN_EPS = 1e-5
GN_EPS = 64e-5
TOP_K = 2

LANES = 128
ROW_TILE = 256
RWKV_HEAD = 64
RWKV_CHUNK = 64
RWKV_QUAD = 4
RWKV_STEP_QUADS = 4
MOE_ROWS = 512
MM_ROWS = 512
ATTN_KEYS = 1408
ATTN_ROWS = 1024
LOG2E = 1.4426950408889634
VMEM_LIMIT = 56 * 1024 * 1024


def _cparams(sem):
    return pltpu.CompilerParams(dimension_semantics=sem, vmem_limit_bytes=VMEM_LIMIT)


def _pick(n, pref):
    if n <= pref:
        return n
    t = pref - pref % LANES
    while t >= LANES:
        if n % t == 0:
            return t
        t -= LANES
    return n


def _ada_kernel(c_ref, w_ref, b_ref, o_ref):
    c = c_ref[...]
    s = (c * jax.nn.sigmoid(c)).astype(BF16)
    o_ref[...] = jnp.dot(s, w_ref[...].astype(BF16), preferred_element_type=F32) + b_ref[...]


def _ada(cvec, ada_w, ada_b):
    n_layers, d, n = ada_w.shape
    tn = _pick(n, 512)
    return pl.pallas_call(
        _ada_kernel,
        grid=(n_layers, n // tn),
        in_specs=[
            pl.BlockSpec((8, d), lambda l, j: (0, 0)),
            pl.BlockSpec((None, d, tn), lambda l, j: (l, 0, j)),
            pl.BlockSpec((None, 1, tn), lambda l, j: (l, 0, j)),
        ],
        out_specs=pl.BlockSpec((None, 8, tn), lambda l, j: (l, 0, j)),
        out_shape=jax.ShapeDtypeStruct((n_layers, 8, n), F32),
        compiler_params=_cparams(("parallel", "parallel")),
        name="ada",
    )(cvec, ada_w, ada_b.reshape(n_layers, 1, n))


def _mod_sel(n_lat_tiles, n_batch):
    return lambda b, i: (jnp.where(i >= n_lat_tiles, n_batch, b), 0, 0)


def _norm_mod_kernel(x_ref, g_ref, sc_ref, sh_ref, o_ref):
    x = x_ref[...]
    y = x * lax.rsqrt(jnp.mean(x * x, axis=-1, keepdims=True) + NORM_EPS) * g_ref[...]
    o_ref[...] = (y * (1.0 + sc_ref[...]) + sh_ref[...]).astype(o_ref.dtype)


def _norm_mod(x, g, scale, shift, n_tiles, n_lat_tiles, out_dtype=BF16):
    nb, _, d = x.shape
    sel = _mod_sel(n_lat_tiles, nb)
    return pl.pallas_call(
        _norm_mod_kernel,
        grid=(nb, n_tiles),
        in_specs=[
            pl.BlockSpec((None, ROW_TILE, d), lambda b, i: (b, i, 0)),
            pl.BlockSpec((1, d), lambda b, i: (0, 0)),
            pl.BlockSpec((None, 1, d), sel),
            pl.BlockSpec((None, 1, d), sel),
        ],
        out_specs=pl.BlockSpec((None, ROW_TILE, d), lambda b, i: (b, i, 0)),
        out_shape=jax.ShapeDtypeStruct((nb, n_tiles * ROW_TILE, d), out_dtype),
        compiler_params=_cparams(("parallel", "parallel")),
        name="norm_mod",
    )(x, g.reshape(1, d), scale, shift)


def _norm_router_kernel(x_ref, g_ref, sc_ref, sh_ref, rh_ref, rl_ref, o_ref, lg_ref):
    x = x_ref[...]
    y = x * lax.rsqrt(jnp.mean(x * x, axis=-1, keepdims=True) + NORM_EPS) * g_ref[...]
    a = y * (1.0 + sc_ref[...]) + sh_ref[...]
    o_ref[...] = a
    ah = a.astype(BF16)
    al = (a - ah.astype(F32)).astype(BF16)
    rh = rh_ref[...]
    lg_ref[...] = (jnp.dot(ah, rh, preferred_element_type=F32)
                   + jnp.dot(al, rh, preferred_element_type=F32)
                   + jnp.dot(ah, rl_ref[...], preferred_element_type=F32))


def _norm_router(x, g, scale, shift, router, n_tiles, n_lat_tiles):
    nb, _, d = x.shape
    n_exp = router.shape[1]
    r_pad = jnp.pad(router, ((0, 0), (0, LANES - n_exp)))
    r_hi = r_pad.astype(BF16)
    r_lo = (r_pad - r_hi.astype(F32)).astype(BF16)
    sel = _mod_sel(n_lat_tiles, nb)
    rows = n_tiles * ROW_TILE
    return pl.pallas_call(
        _norm_router_kernel,
        grid=(nb, n_tiles),
        in_specs=[
            pl.BlockSpec((None, ROW_TILE, d), lambda b, i: (b, i, 0)),
            pl.BlockSpec((1, d), lambda b, i: (0, 0)),
            pl.BlockSpec((None, 1, d), sel),
            pl.BlockSpec((None, 1, d), sel),
            pl.BlockSpec((d, LANES), lambda b, i: (0, 0)),
            pl.BlockSpec((d, LANES), lambda b, i: (0, 0)),
        ],
        out_specs=[
            pl.BlockSpec((None, ROW_TILE, d), lambda b, i: (b, i, 0)),
            pl.BlockSpec((None, ROW_TILE, LANES), lambda b, i: (b, i, 0)),
        ],
        out_shape=[
            jax.ShapeDtypeStruct((nb, rows, d), F32),
            jax.ShapeDtypeStruct((nb, rows, LANES), F32),
        ],
        compiler_params=_cparams(("parallel", "parallel")),
        name="norm_router",
    )(x, g.reshape(1, d), scale, shift, r_hi, r_lo)


def _final_norm_kernel(x_ref, g_ref, o_ref):
    x = x_ref[...]
    o_ref[...] = x * lax.rsqrt(jnp.mean(x * x, axis=-1, keepdims=True) + NORM_EPS) * g_ref[...]


def _final_norm(x, g, n_tiles):
    nb, _, d = x.shape
    rows = n_tiles * ROW_TILE
    return pl.pallas_call(
        _final_norm_kernel,
        grid=(nb, n_tiles),
        in_specs=[
            pl.BlockSpec((None, ROW_TILE, d), lambda b, i: (b, i, 0)),
            pl.BlockSpec((1, d), lambda b, i: (0, 0)),
        ],
        out_specs=pl.BlockSpec((None, ROW_TILE, d), lambda b, i: (b, i, 0)),
        out_shape=jax.ShapeDtypeStruct((nb, rows, d), F32),
        compiler_params=_cparams(("parallel", "parallel")),
        name="final_norm",
    )(x, g.reshape(1, d))


def _row_tile(rows):
    return MM_ROWS if rows % MM_ROWS == 0 else ROW_TILE


def _mm_kernel(a_ref, w_ref, o_ref):
    o_ref[...] = jnp.dot(a_ref[...], w_ref[...], preferred_element_type=F32).astype(o_ref.dtype)


def _matmul(a, w, out_dtype=F32, tn_pref=1024):
    rows, k = a.shape
    n = w.shape[1]
    tn = _pick(n, tn_pref)
    tm = _row_tile(rows)
    return pl.pallas_call(
        _mm_kernel,
        grid=(n // tn, rows // tm),
        in_specs=[
            pl.BlockSpec((tm, k), lambda j, i: (i, 0)),
            pl.BlockSpec((k, tn), lambda j, i: (0, j)),
        ],
        out_specs=pl.BlockSpec((tm, tn), lambda j, i: (i, j)),
        out_shape=jax.ShapeDtypeStruct((rows, n), out_dtype),
        compiler_params=_cparams(("parallel", "parallel")),
        name="matmul",
    )(a, w)


def _mm_swiglu_kernel(a_ref, wg_ref, wu_ref, o_ref):
    a = a_ref[...]
    g = jnp.dot(a, wg_ref[...], preferred_element_type=F32)
    u = jnp.dot(a, wu_ref[...], preferred_element_type=F32)
    o_ref[...] = (g * jax.nn.sigmoid(g) * u).astype(o_ref.dtype)


def _matmul_swiglu(a, wg, wu, tn_pref=512):
    rows, k = a.shape
    n = wg.shape[1]
    tn = _pick(n, tn_pref)
    tm = _row_tile(rows)
    return pl.pallas_call(
        _mm_swiglu_kernel,
        grid=(n // tn, rows // tm),
        in_specs=[
            pl.BlockSpec((tm, k), lambda j, i: (i, 0)),
            pl.BlockSpec((k, tn), lambda j, i: (0, j)),
            pl.BlockSpec((k, tn), lambda j, i: (0, j)),
        ],
        out_specs=pl.BlockSpec((tm, tn), lambda j, i: (i, j)),
        out_shape=jax.ShapeDtypeStruct((rows, n), BF16),
        compiler_params=_cparams(("parallel", "parallel")),
        name="matmul_swiglu",
    )(a, wg, wu)


def _mm_resid_kernel(*refs, n_parts, n_gates):
    a_refs, w_refs = refs[:n_parts], refs[n_parts:2 * n_parts]
    res_ref = refs[2 * n_parts]
    gate_refs = refs[2 * n_parts + 1:2 * n_parts + 1 + n_gates]
    o_ref = refs[-1]
    acc = jnp.dot(a_refs[0][...], w_refs[0][...], preferred_element_type=F32)
    for a_ref, w_ref in zip(a_refs[1:], w_refs[1:]):
        acc = acc + jnp.dot(a_ref[...], w_ref[...], preferred_element_type=F32)
    rows = acc.shape[0] // n_gates
    for t, gate_ref in enumerate(gate_refs):
        sl = slice(t * rows, (t + 1) * rows)
        o_ref[sl, :] = res_ref[sl, :] + gate_ref[...] * acc[sl, :]


def _matmul_resid(parts, w, res, gate, tiles_per_batch, tm=None, tn_pref=512):
    rows, kg = parts[0].shape
    n_parts = len(parts)
    n = w.shape[1]
    nb = gate.shape[0] - 1
    tn = _pick(n, tn_pref)
    tm = tm or _row_tile(rows)
    n_gates = tm // ROW_TILE

    def gate_spec(t):
        def index(j, i):
            tile = i * n_gates + t
            pos = tile % tiles_per_batch
            return (jnp.where(pos == tiles_per_batch - 1, nb, tile // tiles_per_batch), 0, j)
        return pl.BlockSpec((None, 1, tn), index)

    return pl.pallas_call(
        functools.partial(_mm_resid_kernel, n_parts=n_parts, n_gates=n_gates),
        grid=(n // tn, rows // tm),
        in_specs=([pl.BlockSpec((tm, kg), lambda j, i: (i, 0))] * n_parts
                  + [pl.BlockSpec((kg, tn), lambda j, i, g=g: (g, j)) for g in range(n_parts)]
                  + [pl.BlockSpec((tm, tn), lambda j, i: (i, j))]
                  + [gate_spec(t) for t in range(n_gates)]),
        out_specs=pl.BlockSpec((tm, tn), lambda j, i: (i, j)),
        out_shape=jax.ShapeDtypeStruct((rows, n), F32),
        compiler_params=_cparams(("parallel", "parallel")),
        name="matmul_resid",
    )(*parts, *([w] * n_parts), res, *([gate] * n_gates))


_NT = (((1,), (1,)), ((), ()))


def _online_softmax(qs, k_ref, v_ref, ck):
    n_rows = qs.shape[0]
    m = jnp.full((n_rows, 1), -jnp.inf, F32)
    acc = jnp.zeros((n_rows, v_ref.shape[1]), F32)
    for c in range(k_ref.shape[0] // ck):
        keys = slice(c * ck, (c + 1) * ck)
        s = lax.dot_general(qs, k_ref[keys, :], _NT, preferred_element_type=F32)
        m_new = jnp.maximum(m, jnp.max(s, axis=-1, keepdims=True))
        p = jnp.exp2(s - m_new).astype(BF16)
        acc = jnp.exp2(m - m_new) * acc + jnp.dot(p, v_ref[keys, :], preferred_element_type=F32)
        m = m_new
    return acc


def _attend_rows(body, pre_args, pre_specs, q, k, v1, n_heads, qw, dk, ow, tq, n_lat, name):
    nb, s_len, _ = q.shape
    n_ctx = s_len - n_lat
    dvp = v1.shape[2] // n_heads
    out_shape = jax.ShapeDtypeStruct((nb, s_len, n_heads * ow), BF16)

    def run(tq, n_q, q_off, k_rows, k_off, prev):
        kern = functools.partial(body, ck=_pick(k_rows, ATTN_KEYS))
        specs = pre_specs + [
            pl.BlockSpec((None, tq, qw), lambda b, h, i: (b, i + q_off, h)),
            pl.BlockSpec((None, k_rows, dk), lambda b, h, i: (b, k_off, h)),
            pl.BlockSpec((None, k_rows, dvp), lambda b, h, i: (b, k_off, h)),
        ]
        args = pre_args + [q, k, v1, prev]
        return pl.pallas_call(
            lambda *refs: kern(*refs[:-2], refs[-1]),
            grid=(nb, n_heads, n_q),
            in_specs=specs + [pl.BlockSpec(memory_space=pl.ANY)],
            out_specs=pl.BlockSpec((None, tq, ow), lambda b, h, i: (b, i + q_off, h)),
            out_shape=out_shape,
            input_output_aliases={len(args) - 1: 0},
            compiler_params=_cparams(("parallel", "parallel", "parallel")),
            name=name,
        )(*args)

    tq = math.gcd(tq, n_lat)
    out = run(tq, n_lat // tq, 0, s_len, 0, jnp.zeros(out_shape.shape, out_shape.dtype))
    return run(n_ctx, 1, n_lat // n_ctx, n_ctx, n_lat // n_ctx, out)


def _attn_kernel(q_ref, k_ref, v_ref, o_ref, *, pieces, dv, ck):
    q = q_ref[...]
    tq, dk = q.shape[0], q.shape[1] // pieces
    qs = q if pieces == 1 else jnp.concatenate([q[:, g * dk:(g + 1) * dk] for g in range(pieces)], axis=0)
    acc = _online_softmax(qs, k_ref, v_ref, ck)
    o = acc[:, :dv] / acc[:, dv:dv + 1]
    for g in range(pieces):
        o_ref[:, g * dv:(g + 1) * dv] = o[g * tq:(g + 1) * tq].astype(o_ref.dtype)


def _attention(q, k, v1, n_kv_heads, group, dk, dv, tq, n_lat):
    return _attend_rows(functools.partial(_attn_kernel, pieces=group, dv=dv), [], [], q, k, v1, n_kv_heads,
                        group * dk, dk, group * dv, tq, n_lat, "attention")


def _diff_attn_kernel(lam_ref, g_ref, q_ref, k_ref, v_ref, o_ref, *, dv, ck, out_scale):
    q = q_ref[...]
    tq, width = q.shape
    first = lax.broadcasted_iota(jnp.int32, (tq, width), 1) < width // 2
    zero = jnp.zeros_like(q)
    qs = jnp.concatenate([jnp.where(first, q, zero), jnp.where(first, zero, q)], axis=0)
    acc = _online_softmax(qs, k_ref, v_ref, ck)
    o = acc[:, :dv] / acc[:, dv:dv + 1]
    o = o[:tq] - lam_ref[0] * o[tq:]
    o = o * lax.rsqrt(jnp.mean(o * o, axis=-1, keepdims=True) + SUBLN_EPS) * g_ref[...]
    o_ref[...] = (o * out_scale).astype(o_ref.dtype)


def _diff_attention(q, k, v1, lam, subln_g, n_heads, out_scale, tq, n_lat):
    dh = q.shape[2] // n_heads
    return _attend_rows(functools.partial(_diff_attn_kernel, dv=dh, out_scale=out_scale),
                        [lam.reshape(1).astype(F32), subln_g.reshape(1, dh)],
                        [pl.BlockSpec(memory_space=pltpu.SMEM), pl.BlockSpec((1, dh), lambda b, h, i: (0, 0))],
                        q, k, v1, n_heads, dh, dh, dh, tq, n_lat, "diff_attention")


def _split3(x):
    h = x.astype(BF16)
    r = x - h.astype(F32)
    m = r.astype(BF16)
    return h, m, (r - m.astype(F32)).astype(BF16)


def _rwkv_chunk(r, v, kk, wl, kd, bb, h_state, rev, out):
    c, w = r.shape
    nh = w // RWKV_HEAD
    assert c == RWKV_HEAD
    ri = lax.broadcasted_iota(jnp.int32, (c, c), 0)
    ci = lax.broadcasted_iota(jnp.int32, (c, c), 1)
    before_incl = (ci >= ri) if rev else (ci <= ri)
    tri = jnp.where(before_incl, 1.0, 0.0).astype(BF16)
    w_h, w_m, w_l = _split3(wl)
    cum = (jnp.dot(tri, w_h, preferred_element_type=F32) + jnp.dot(tri, w_m, preferred_element_type=F32)
           + jnp.dot(tri, w_l, preferred_element_type=F32))
    yield
    cum_prev = cum - wl
    last = 0 if rev else c - 1
    total = cum[last:last + 1, :]
    mid = cum[c // 2:c // 2 + 1, :]
    a = -kk
    e_mc = jnp.exp(mid - cum)
    e_end = jnp.exp(total - cum)

    lane = lax.broadcasted_iota(jnp.int32, (c, w), 1)
    lane_head, ts = lane // c, lane % c
    tr = lax.broadcasted_iota(jnp.int32, (c, w), 0)
    strict = (ts > tr) if rev else (ts < tr)
    incl = (ts >= tr) if rev else (ts <= tr)

    def stack(x):
        return jnp.concatenate([jnp.where(lane_head == h, x, 0.0) for h in range(nh)], axis=0).astype(BF16)

    def dot_nt(x, y):
        return lax.dot_general(x.astype(BF16), y.astype(BF16), _NT, preferred_element_type=F32)

    def dot_tn(x, y):
        return lax.dot_general(x.astype(BF16), y.astype(BF16), (((0,), (0,)), ((), ())), preferred_element_type=F32)

    def dot(x, y):
        return jnp.dot(x.astype(BF16), y.astype(BF16), preferred_element_type=F32)

    lhs = jnp.concatenate([a * jnp.exp(cum_prev - mid), r * jnp.exp(cum - mid)], axis=0)
    g_b = dot_nt(lhs, stack(bb * e_mc))
    g_k = dot_nt(lhs, stack(kd * e_mc))
    n_ab = jnp.where(strict, g_b[:c], 0.0)
    a_rb = jnp.where(incl, g_b[c:], 0.0)
    a_ak = jnp.where(strict, g_k[:c], 0.0)
    a_rk = jnp.where(incl, g_k[c:], 0.0)
    yield

    t_inv = jnp.where(ts == tr, 1.0, 0.0) + n_ab
    pw = n_ab
    for _ in range(int(math.log2(c)) - 1):
        pw = dot(pw, stack(pw))
        yield
        t_inv = t_inv + dot(pw, stack(t_inv))
    yield

    v_st = stack(v)
    x = dot(a * jnp.exp(cum_prev), h_state) + dot(a_ak, v_st)
    yield
    u = dot(t_inv, stack(x))
    yield
    y = dot(r * jnp.exp(cum), h_state) + dot(a_rb, stack(u)) + dot(a_rk, v_st)
    rw = lax.broadcasted_iota(jnp.int32, (w, w), 0)
    cw = lax.broadcasted_iota(jnp.int32, (w, w), 1)
    decay_end = jnp.where(rw == cw, jnp.broadcast_to(jnp.exp(total), (w, w)), 0.0)
    update = dot_tn(jnp.concatenate([bb * e_end, kd * e_end], axis=0), jnp.concatenate([u, v], axis=0))
    out.append((y, dot(decay_end, h_state) + jnp.where(rw // RWKV_HEAD == cw // RWKV_HEAD, update, 0.0)))


def _rwkv_scan_kernel(rf_ref, vf_ref, kkf_ref, wlf_ref, kdf_ref, bbf_ref,
                      rr_ref, vr_ref, kkr_ref, wlr_ref, kdr_ref, bbr_ref,
                      yf_ref, yr_ref, hf_ref, hr_ref):
    @pl.when(pl.program_id(2) == 0)
    def _():
        hf_ref[...] = jnp.zeros_like(hf_ref)
        hr_ref[...] = jnp.zeros_like(hr_ref)

    qw = RWKV_QUAD * RWKV_HEAD
    chains = []
    for i in range(hf_ref.shape[0]):
        sl = slice(i * qw, (i + 1) * qw)
        for refs, y_ref, h_ref, rev in (((rf_ref, vf_ref, kkf_ref, wlf_ref, kdf_ref, bbf_ref), yf_ref, hf_ref, False),
                                        ((rr_ref, vr_ref, kkr_ref, wlr_ref, kdr_ref, bbr_ref), yr_ref, hr_ref, True)):
            out = []
            chains.append((_rwkv_chunk(*(ref[:, sl] for ref in refs), h_ref[i], rev, out), out, y_ref, h_ref, i, sl))
    for _ in itertools.zip_longest(*(chain[0] for chain in chains)):
        pass
    for _, out, y_ref, h_ref, i, sl in chains:
        y_ref[:, sl], h_ref[i] = out[0]


def _rwkv_scan(r, v, kk, wl, kd, bb, n_lat):
    nb, s_len, w = r.shape
    c = RWKV_CHUNK
    qw = RWKV_QUAD * RWKV_HEAD
    nq = RWKV_STEP_QUADS if w % (RWKV_STEP_QUADS * qw) == 0 else 1
    bw = nq * qw
    n_l, n_all = n_lat // c, s_len // c
    n_c = n_all - n_l

    def fwd(s):
        return jnp.where(s < n_c, n_l + s, s - n_c)

    def rev(s):
        return n_all - 1 - s

    def shared(order):
        return pl.BlockSpec((None, c, bw), lambda b, q, s: (b, order(s), q))

    def per_dir(d, order):
        return pl.BlockSpec((None, None, c, bw), lambda b, q, s: (d, b, order(s), q))

    yf, yr = pl.pallas_call(
        _rwkv_scan_kernel,
        grid=(nb, w // bw, n_all),
        in_specs=[shared(fwd), shared(fwd), shared(fwd), per_dir(0, fwd), per_dir(0, fwd), per_dir(0, fwd),
                  shared(rev), shared(rev), shared(rev), per_dir(1, rev), per_dir(1, rev), per_dir(1, rev)],
        out_specs=[shared(fwd), shared(rev)],
        out_shape=[jax.ShapeDtypeStruct((nb, s_len, w), F32)] * 2,
        scratch_shapes=[pltpu.VMEM((nq, qw, qw), F32), pltpu.VMEM((nq, qw, qw), F32)],
        compiler_params=_cparams(("parallel", "parallel", "arbitrary")),
        name="rwkv_scan",
    )(r, v, kk, wl, kd, bb, r, v, kk, wl, kd, bb)
    return yf, yr


def _group_sum(x, ones_bd):
    hi = x.astype(BF16)
    lo = (x - hi.astype(F32)).astype(BF16)
    return jnp.dot(hi, ones_bd, preferred_element_type=F32) + jnp.dot(lo, ones_bd, preferred_element_type=F32)


def _rwkv_prep_kernel(x_ref, prev_ref, next_ref, mu_ref, kkg_ref, ka_ref, rk_ref, w0_ref, a0_ref, wup_ref, aup_ref,
                      gup_ref, ones_ref, r_ref, v_ref, kk_ref, wl_ref, kd_ref, bb_ref, bonus_ref, g_ref,
                      *, n_lat_tiles, gw, lora_w, lora_a):
    i = pl.program_id(1)
    x = x_ref[...]
    rows = x.shape[0]
    row = lax.broadcasted_iota(jnp.int32, x.shape, 0)
    has_prev = jnp.logical_and(i > 0, i < n_lat_tiles)
    has_next = i < n_lat_tiles - 1
    prev_row = jnp.where(has_prev, prev_ref[7:8, :], 0.0)
    next_row = jnp.where(has_next, next_ref[0:1, :], 0.0)
    prev = jnp.where(row == 0, prev_row, pltpu.roll(x, 1, axis=0))
    nxt = jnp.where(row == rows - 1, next_row, pltpu.roll(x, rows - 1, axis=0))
    ps = x + mu_ref[...] * (0.5 * (prev + nxt) - x)
    r, k, v = ps[:, :gw], ps[:, gw:2 * gw], ps[:, 2 * gw:3 * gw]
    o = 3 * gw
    wd, ad, gd = ps[:, o:o + lora_w], ps[:, o + lora_w:o + lora_w + lora_a], ps[:, o + lora_w + lora_a:]
    ones_bd = ones_ref[...]
    kk = k * kkg_ref[...]
    kk = kk * lax.rsqrt(jnp.maximum(_group_sum(kk * kk, ones_bd), 1e-24))
    tanh_wd = jnp.tanh(wd).astype(BF16)
    ad_b = ad.astype(BF16)
    kd_sum = jnp.zeros_like(k)
    for d in range(2):
        z = w0_ref[d] + jnp.dot(tanh_wd, wup_ref[d], preferred_element_type=F32)
        softplus_neg = jnp.maximum(-z, 0.0) + jnp.log1p(jnp.exp(-jnp.abs(z)))
        wl_ref[d] = -jnp.exp(-softplus_neg - 0.5)
        a_sig = jax.nn.sigmoid(a0_ref[d] + jnp.dot(ad_b, aup_ref[d], preferred_element_type=F32))
        kd = k * (1.0 + (a_sig - 1.0) * ka_ref[...])
        kd_ref[d] = kd
        bb_ref[d] = kk * a_sig
        kd_sum = kd_sum + kd
    r_ref[...] = r
    v_ref[...] = v
    kk_ref[...] = kk
    bonus_ref[...] = _group_sum(r * kd_sum * rk_ref[...], ones_bd) * v
    g_ref[...] = jnp.dot(jax.nn.sigmoid(gd).astype(BF16), gup_ref[...], preferred_element_type=F32)


def _head_ones(width, head):
    idx = jnp.arange(width) // head
    return (idx[:, None] == idx[None, :]).astype(BF16)


def _rwkv_prep(pa, n_lat_tiles, mu, w0, w_up, a0, a_up, g_up, k_k, k_a, r_k):
    nb, s_len, wp = pa.shape
    gw = k_k.shape[0]
    lora_w, lora_a = w_up.shape[1], a_up.shape[1]
    lora_g = wp - 3 * gw - lora_w - lora_a
    n_tiles = s_len // ROW_TILE
    sub = ROW_TILE // 8
    row = lambda z: z.reshape(1, -1)
    full = lambda shape: pl.BlockSpec(shape, lambda b, i: (0,) * len(shape))
    tile = lambda w: pl.BlockSpec((None, ROW_TILE, w), lambda b, i: (b, i, 0))
    tile2 = pl.BlockSpec((2, None, ROW_TILE, gw), lambda b, i: (0, b, i, 0))
    one = jax.ShapeDtypeStruct((nb, s_len, gw), F32)
    two = jax.ShapeDtypeStruct((2, nb, s_len, gw), F32)
    return pl.pallas_call(
        functools.partial(_rwkv_prep_kernel, n_lat_tiles=n_lat_tiles, gw=gw, lora_w=lora_w, lora_a=lora_a),
        grid=(nb, n_tiles),
        in_specs=[
            tile(wp),
            pl.BlockSpec((None, 8, wp), lambda b, i: (b, jnp.maximum(i * sub - 1, 0), 0)),
            pl.BlockSpec((None, 8, wp), lambda b, i: (b, jnp.minimum((i + 1) * sub, s_len // 8 - 1), 0)),
            full((1, wp)), full((1, gw)), full((1, gw)), full((1, gw)),
            full((2, 1, gw)), full((2, 1, gw)),
            full((2, lora_w, gw)), full((2, lora_a, gw)), full((lora_g, gw)), full((gw, gw)),
        ],
        out_specs=[tile(gw), tile(gw), tile(gw), tile2, tile2, tile2, tile(gw), tile(gw)],
        out_shape=[one, one, one, two, two, two, one, one],
        compiler_params=_cparams(("parallel", "parallel")),
        name="rwkv_prep",
    )(pa, pa, pa, row(jnp.pad(mu, (0, wp - mu.shape[0]))), row(k_k), row(k_a), row(r_k),
      w0.reshape(2, 1, gw), a0.reshape(2, 1, gw), w_up.astype(BF16), a_up.astype(BF16),
      jnp.pad(g_up, ((0, lora_g - g_up.shape[0]), (0, 0))).astype(BF16), _head_ones(gw, RWKV_HEAD))


def _rwkv_out_kernel(yf_ref, yr_ref, bonus_ref, g_ref, lw_ref, lb_ref, ones_ref, o_ref):
    ones_bd = ones_ref[...]
    y = yf_ref[...] + yr_ref[...]
    inv_n = 1.0 / RWKV_HEAD
    dev = y - _group_sum(y, ones_bd) * inv_n
    var = _group_sum(dev * dev, ones_bd) * inv_n
    yn = dev * lax.rsqrt(var + GN_EPS) * lw_ref[...] + lb_ref[...]
    o_ref[...] = ((yn + bonus_ref[...]) * g_ref[...]).astype(o_ref.dtype)


def _rwkv_out(yf, yr, bonus, g, lnx_w, lnx_b):
    nb, s_len, gw = yf.shape
    tile = pl.BlockSpec((None, ROW_TILE, gw), lambda b, i: (b, i, 0))
    vec = pl.BlockSpec((1, gw), lambda b, i: (0, 0))
    return pl.pallas_call(
        _rwkv_out_kernel,
        grid=(nb, s_len // ROW_TILE),
        in_specs=[tile, tile, tile, tile, vec, vec, pl.BlockSpec((gw, gw), lambda b, i: (0, 0))],
        out_specs=tile,
        out_shape=jax.ShapeDtypeStruct((nb, s_len, gw), BF16),
        compiler_params=_cparams(("parallel", "parallel")),
        name="rwkv_out",
    )(yf, yr, bonus, g, lnx_w.reshape(1, gw), lnx_b.reshape(1, gw), _head_ones(gw, RWKV_HEAD))


def _rope_block(xb, cos_t, sin_t, half):
    if 2 * half == LANES:
        partner = pltpu.roll(xb, half, axis=1)
    else:
        lane = lax.broadcasted_iota(jnp.int32, xb.shape, 1)
        partner = jnp.where(lane % (2 * half) < half, pltpu.roll(xb, LANES - half, axis=1),
                            pltpu.roll(xb, half, axis=1))
    return xb * cos_t + partner * sin_t


def _lane_tables(cs):
    cos, sin = cs
    reps = LANES // (2 * cos.shape[1])
    return (jnp.tile(jnp.concatenate([cos, cos], axis=1), (1, reps)),
            jnp.tile(jnp.concatenate([-sin, sin], axis=1), (1, reps)))


def _ones_column(rows, dtype):
    return jnp.where(lax.broadcasted_iota(jnp.int32, (rows, LANES), 1) == 0, 1.0, 0.0).astype(dtype)


def _blk(b):
    return slice(b * LANES, (b + 1) * LANES)


def _diff_prep_kernel(x_ref, cos_ref, sin_ref, q_ref, k_ref, v_ref, *, gw, half, q_scale):
    cos_t, sin_t = cos_ref[...], sin_ref[...]
    e0 = _ones_column(x_ref.shape[0], v_ref.dtype)
    nblk = gw // LANES
    for b in range(nblk):
        q_ref[:, _blk(b)] = (_rope_block(x_ref[:, _blk(b)], cos_t, sin_t, half) * q_scale).astype(q_ref.dtype)
        k_ref[:, _blk(b)] = _rope_block(x_ref[:, _blk(nblk + b)], cos_t, sin_t, half).astype(k_ref.dtype)
        v_ref[:, _blk(2 * b)] = x_ref[:, _blk(2 * nblk + b)].astype(v_ref.dtype)
        v_ref[:, _blk(2 * b + 1)] = e0


def _diff_prep(pb, tables, half, q_scale):
    nb, s_len, w3 = pb.shape
    gw = w3 // 3
    tile = lambda w: pl.BlockSpec((None, ROW_TILE, w), lambda b, i: (b, i, 0))
    tab = pl.BlockSpec((ROW_TILE, LANES), lambda b, i: (i, 0))
    return pl.pallas_call(
        functools.partial(_diff_prep_kernel, gw=gw, half=half, q_scale=q_scale),
        grid=(nb, s_len // ROW_TILE),
        in_specs=[tile(w3), tab, tab],
        out_specs=[tile(gw), tile(gw), tile(2 * gw)],
        out_shape=[jax.ShapeDtypeStruct((nb, s_len, gw), BF16), jax.ShapeDtypeStruct((nb, s_len, gw), BF16),
                   jax.ShapeDtypeStruct((nb, s_len, 2 * gw), BF16)],
        compiler_params=_cparams(("parallel", "parallel")),
        name="diff_prep",
    )(pb, *tables)


def _gqa_prep_kernel(x_ref, cos_ref, sin_ref, qg_ref, kg_ref, q_ref, k_ref, v_ref, *, n_heads, n_kv, q_scale):
    cos_t, sin_t = cos_ref[...], sin_ref[...]
    e0 = _ones_column(x_ref.shape[0], v_ref.dtype)

    def normed_rope(xb, g):
        xb = xb * lax.rsqrt(jnp.mean(xb * xb, axis=-1, keepdims=True) + NORM_EPS) * g
        return _rope_block(xb, cos_t, sin_t, LANES // 2)

    for b in range(n_heads):
        q_ref[:, _blk(b)] = (normed_rope(x_ref[:, _blk(b)], qg_ref[...]) * q_scale).astype(q_ref.dtype)
    for b in range(n_kv):
        k_ref[:, _blk(b)] = normed_rope(x_ref[:, _blk(n_heads + b)], kg_ref[...]).astype(k_ref.dtype)
        v_ref[:, _blk(2 * b)] = x_ref[:, _blk(n_heads + n_kv + b)].astype(v_ref.dtype)
        v_ref[:, _blk(2 * b + 1)] = e0


def _gqa_prep(pc, tables, q_norm_g, k_norm_g, n_heads, n_kv, q_scale):
    nb, s_len, w = pc.shape
    tile = lambda w_: pl.BlockSpec((None, ROW_TILE, w_), lambda b, i: (b, i, 0))
    tab = pl.BlockSpec((ROW_TILE, LANES), lambda b, i: (i, 0))
    vec = pl.BlockSpec((1, LANES), lambda b, i: (0, 0))
    return pl.pallas_call(
        functools.partial(_gqa_prep_kernel, n_heads=n_heads, n_kv=n_kv, q_scale=q_scale),
        grid=(nb, s_len // ROW_TILE),
        in_specs=[tile(w), tab, tab, vec, vec],
        out_specs=[tile(n_heads * LANES), tile(n_kv * LANES), tile(2 * n_kv * LANES)],
        out_shape=[jax.ShapeDtypeStruct((nb, s_len, n_heads * LANES), BF16),
                   jax.ShapeDtypeStruct((nb, s_len, n_kv * LANES), BF16),
                   jax.ShapeDtypeStruct((nb, s_len, 2 * n_kv * LANES), BF16)],
        compiler_params=_cparams(("parallel", "parallel")),
        name="gqa_prep",
    )(pc, *tables, q_norm_g.reshape(1, LANES), k_norm_g.reshape(1, LANES))


def _mla_prep_kernel(x_ref, cos_ref, sin_ref, qg_ref, kvg_ref, qup_ref, kvup_ref, q_ref, k_ref, v_ref,
                     *, q_lora, kv_lora, n_heads, half, q_scale):
    cos_t, sin_t = cos_ref[...], sin_ref[...]
    e0 = _ones_column(x_ref.shape[0], v_ref.dtype)

    def normed(z, g):
        return (z * lax.rsqrt(jnp.mean(z * z, axis=-1, keepdims=True) + NORM_EPS) * g).astype(BF16)

    q = jnp.dot(normed(x_ref[:, :q_lora], qg_ref[...]), qup_ref[...], preferred_element_type=F32)
    kv = jnp.dot(normed(x_ref[:, q_lora:q_lora + kv_lora], kvg_ref[...]), kvup_ref[...], preferred_element_type=F32)
    k_rope = _rope_block(x_ref[:, q_lora + kv_lora:q_lora + kv_lora + LANES], cos_t, sin_t, half).astype(k_ref.dtype)
    for h in range(n_heads):
        q_ref[:, _blk(2 * h)] = (q[:, _blk(2 * h)] * q_scale).astype(q_ref.dtype)
        q_ref[:, _blk(2 * h + 1)] = (_rope_block(q[:, _blk(2 * h + 1)], cos_t, sin_t, half) * q_scale).astype(q_ref.dtype)
        k_ref[:, _blk(2 * h)] = kv[:, _blk(h)].astype(k_ref.dtype)
        k_ref[:, _blk(2 * h + 1)] = k_rope
        v_ref[:, _blk(2 * h)] = kv[:, _blk(n_heads + h)].astype(v_ref.dtype)
        v_ref[:, _blk(2 * h + 1)] = e0


def _mla_prep(pd, tables, q_norm_g, q_up, kv_norm_g, kv_up, n_heads, d_rope, q_scale):
    nb, s_len, wp = pd.shape
    q_lora, kv_lora = q_norm_g.shape[0], kv_norm_g.shape[0]
    d_qk = q_up.shape[1] // n_heads
    d_nope = d_qk - d_rope
    assert d_nope == LANES and kv_up.shape[1] == n_heads * 2 * LANES and q_lora + kv_lora + LANES <= wp
    qw = jnp.pad(q_up.reshape(q_lora, n_heads, d_qk), ((0, 0), (0, 0), (0, 2 * LANES - d_qk)))
    qw = qw.reshape(q_lora, n_heads * 2 * LANES).astype(BF16)
    kvw = kv_up.reshape(kv_lora, n_heads, 2, LANES).transpose(0, 2, 1, 3).reshape(kv_lora, 2 * n_heads * LANES)
    tile = lambda w_: pl.BlockSpec((None, ROW_TILE, w_), lambda b, i: (b, i, 0))
    tab = pl.BlockSpec((ROW_TILE, LANES), lambda b, i: (i, 0))
    full = lambda shape: pl.BlockSpec(shape, lambda b, i: (0,) * len(shape))
    wide = n_heads * 2 * LANES
    out = jax.ShapeDtypeStruct((nb, s_len, wide), BF16)
    return pl.pallas_call(
        functools.partial(_mla_prep_kernel, q_lora=q_lora, kv_lora=kv_lora, n_heads=n_heads, half=d_rope // 2,
                          q_scale=q_scale),
        grid=(nb, s_len // ROW_TILE),
        in_specs=[tile(wp), tab, tab, full((1, q_lora)), full((1, kv_lora)), full((q_lora, wide)),
                  full((kv_lora, wide))],
        out_specs=[tile(wide), tile(wide), tile(wide)],
        out_shape=[out, out, out],
        compiler_params=_cparams(("parallel", "parallel")),
        name="mla_prep",
    )(pd, *tables, q_norm_g.reshape(1, q_lora), kv_norm_g.reshape(1, kv_lora), qw, kvw.astype(BF16))


def _moe_gather_kernel(src_ref, used_ref, tok_ref, o_ref, buf_ref, sem_ref):
    base = pl.program_id(0) * MOE_ROWS

    def copy(r):
        return pltpu.make_async_copy(tok_ref.at[pl.ds(src_ref[base + r], 1), :], buf_ref.at[pl.ds(r, 1), :],
                                     sem_ref.at[0])

    def start(r, carry):
        copy(r).start()
        return carry

    @pl.when(pl.program_id(0) < used_ref[0])
    def _():
        lax.fori_loop(0, MOE_ROWS, start, 0, unroll=8)
        pltpu.make_async_copy(tok_ref.at[pl.ds(0, MOE_ROWS), :], buf_ref, sem_ref.at[0]).wait()
        o_ref[...] = buf_ref[...].astype(o_ref.dtype)

    @pl.when(pl.program_id(0) >= used_ref[0])
    def _():
        o_ref[...] = jnp.zeros_like(o_ref)


def _used_block(i, used):
    return jnp.minimum(i, used[0] - 1)


def _moe_gather(tok, src, used, n_blocks):
    d = tok.shape[1]
    return pl.pallas_call(
        _moe_gather_kernel,
        grid_spec=pltpu.PrefetchScalarGridSpec(
            num_scalar_prefetch=2,
            grid=(n_blocks,),
            in_specs=[pl.BlockSpec(memory_space=pl.ANY)],
            out_specs=pl.BlockSpec((MOE_ROWS, d), lambda i, src, used: (i, 0)),
            scratch_shapes=[pltpu.VMEM((MOE_ROWS, d), F32), pltpu.SemaphoreType.DMA((1,))],
        ),
        out_shape=jax.ShapeDtypeStruct((n_blocks * MOE_ROWS, d), BF16),
        compiler_params=_cparams(("arbitrary",)),
        name="moe_gather",
    )(src, used, tok)


def _moe_up_kernel(be_ref, used_ref, a_ref, wg_ref, wu_ref, o_ref):
    @pl.when(pl.program_id(0) < used_ref[0])
    def _():
        a = a_ref[...]
        g = jnp.dot(a, wg_ref[...], preferred_element_type=F32)
        u = jnp.dot(a, wu_ref[...], preferred_element_type=F32)
        o_ref[...] = (g * jax.nn.sigmoid(g) * u).astype(o_ref.dtype)

    @pl.when(pl.program_id(0) >= used_ref[0])
    def _():
        o_ref[...] = jnp.zeros_like(o_ref)


def _moe_block_maps(n_col_tiles):
    def col(i, j, used):
        return jnp.where(i < used[0], j, n_col_tiles - 1)
    rows = lambda i, j, be, used: (_used_block(i, used), 0)
    weight = lambda i, j, be, used: (be[_used_block(i, used)], 0, col(i, j, used))
    out = lambda i, j, be, used: (i, j)
    return rows, weight, out


def _moe_up(buf, block_expert, used, wg, wu, tn_pref=512):
    rows, d = buf.shape
    f = wg.shape[2]
    tn = _pick(f, tn_pref)
    n_blocks = rows // MOE_ROWS
    row_map, w_map, out_map = _moe_block_maps(f // tn)
    return pl.pallas_call(
        _moe_up_kernel,
        grid_spec=pltpu.PrefetchScalarGridSpec(
            num_scalar_prefetch=2,
            grid=(n_blocks, f // tn),
            in_specs=[
                pl.BlockSpec((MOE_ROWS, d), row_map),
                pl.BlockSpec((None, d, tn), w_map),
                pl.BlockSpec((None, d, tn), w_map),
            ],
            out_specs=pl.BlockSpec((MOE_ROWS, tn), out_map),
        ),
        out_shape=jax.ShapeDtypeStruct((rows, f), BF16),
        compiler_params=_cparams(("arbitrary", "arbitrary")),
        name="moe_up",
    )(block_expert, used, buf, wg, wu)


def _moe_down_kernel(be_ref, used_ref, h_ref, w_ref, o_ref):
    @pl.when(pl.program_id(0) < used_ref[0])
    def _():
        o_ref[...] = jnp.dot(h_ref[...], w_ref[...], preferred_element_type=F32)

    @pl.when(pl.program_id(0) >= used_ref[0])
    def _():
        o_ref[...] = jnp.zeros_like(o_ref)


def _moe_down(hid, block_expert, used, wd, tn_pref=512):
    rows, f = hid.shape
    d = wd.shape[2]
    tn = _pick(d, tn_pref)
    n_blocks = rows // MOE_ROWS
    row_map, w_map, out_map = _moe_block_maps(d // tn)
    return pl.pallas_call(
        _moe_down_kernel,
        grid_spec=pltpu.PrefetchScalarGridSpec(
            num_scalar_prefetch=2,
            grid=(n_blocks, d // tn),
            in_specs=[
                pl.BlockSpec((MOE_ROWS, f), row_map),
                pl.BlockSpec((None, f, tn), w_map),
            ],
            out_specs=pl.BlockSpec((MOE_ROWS, tn), out_map),
        ),
        out_shape=jax.ShapeDtypeStruct((rows, d), F32),
        compiler_params=_cparams(("arbitrary", "arbitrary")),
        name="moe_down",
    )(block_expert, used, hid, wd)


def _moe_combine_kernel(dest_ref, y_ref, x_ref, gate_ref, g0_ref, g1_ref, o_ref, b0_ref, b1_ref, sem_ref):
    base = (pl.program_id(0) * pl.num_programs(1) + pl.program_id(1)) * ROW_TILE

    def copies(r):
        t = (base + r) * TOP_K
        return (pltpu.make_async_copy(y_ref.at[pl.ds(dest_ref[t], 1), :], b0_ref.at[pl.ds(r, 1), :], sem_ref.at[0]),
                pltpu.make_async_copy(y_ref.at[pl.ds(dest_ref[t + 1], 1), :], b1_ref.at[pl.ds(r, 1), :],
                                      sem_ref.at[1]))

    def start(r, carry):
        c0, c1 = copies(r)
        c0.start()
        c1.start()
        return carry

    lax.fori_loop(0, ROW_TILE, start, 0, unroll=8)
    pltpu.make_async_copy(y_ref.at[pl.ds(0, ROW_TILE), :], b0_ref, sem_ref.at[0]).wait()
    pltpu.make_async_copy(y_ref.at[pl.ds(0, ROW_TILE), :], b1_ref, sem_ref.at[1]).wait()
    o_ref[...] = x_ref[...] + gate_ref[...] * (g0_ref[...] * b0_ref[...] + g1_ref[...] * b1_ref[...])


def _moe_combine(y, dest, x, gate, g0, g1):
    nb, _, d = x.shape
    t = g0.shape[1]
    return pl.pallas_call(
        _moe_combine_kernel,
        grid_spec=pltpu.PrefetchScalarGridSpec(
            num_scalar_prefetch=1,
            grid=(nb, t // ROW_TILE),
            in_specs=[
                pl.BlockSpec(memory_space=pl.ANY),
                pl.BlockSpec((None, ROW_TILE, d), lambda b, i, dest: (b, i, 0)),
                pl.BlockSpec((None, 1, d), lambda b, i, dest: (b, 0, 0)),
                pl.BlockSpec((None, ROW_TILE, 1), lambda b, i, dest: (b, i, 0)),
                pl.BlockSpec((None, ROW_TILE, 1), lambda b, i, dest: (b, i, 0)),
            ],
            out_specs=pl.BlockSpec((None, ROW_TILE, d), lambda b, i, dest: (b, i, 0)),
            scratch_shapes=[pltpu.VMEM((ROW_TILE, d), F32), pltpu.VMEM((ROW_TILE, d), F32),
                            pltpu.SemaphoreType.DMA((2,))],
        ),
        out_shape=jax.ShapeDtypeStruct((nb, t, d), F32),
        compiler_params=_cparams(("arbitrary", "arbitrary")),
        name="moe_combine",
    )(dest, y, x, gate, g0, g1)


def _moe_layer(a_tok, logits, x, gate, wg, wu, wd):
    nb, t, d = a_tok.shape
    n_exp = wg.shape[0]
    n_tok = nb * t
    n_assign = n_tok * TOP_K
    top_logit, top_idx = lax.top_k(logits.reshape(n_tok, -1)[:, :n_exp], TOP_K)
    gates = jax.nn.softmax(top_logit, axis=-1)
    flat_e = top_idx.reshape(-1)
    onehot = (flat_e[:, None] == jnp.arange(n_exp)[None, :]).astype(jnp.int32)
    rank = jnp.take_along_axis(jnp.cumsum(onehot, axis=0) - onehot, flat_e[:, None], axis=1)[:, 0]
    counts = jnp.sum(onehot, axis=0)
    padded = (counts + MOE_ROWS - 1) // MOE_ROWS * MOE_ROWS
    pad_end = jnp.cumsum(padded)
    pad_start = pad_end - padded
    dest = (pad_start[flat_e] + rank).astype(jnp.int32)
    n_blocks = -(-n_assign // MOE_ROWS) + n_exp
    src = jnp.zeros((n_blocks * MOE_ROWS,), jnp.int32).at[dest].set(jnp.arange(n_assign, dtype=jnp.int32) // TOP_K)
    block_expert = jnp.minimum(
        jnp.searchsorted(pad_end, jnp.arange(n_blocks) * MOE_ROWS, side='right'), n_exp - 1).astype(jnp.int32)
    used = (pad_end[-1:] // MOE_ROWS).astype(jnp.int32)
    buf = _moe_gather(a_tok.reshape(n_tok, d), src, used, n_blocks)
    hid = _moe_up(buf, block_expert, used, wg, wu)
    y = _moe_down(hid, block_expert, used, wd)
    return _moe_combine(y, dest, x, gate, gates[:, 0].reshape(nb, t, 1), gates[:, 1].reshape(nb, t, 1))


def _rope_tables(n_lat, n_ctx, rot_dim):
    rows = n_lat // GRID_W
    row = jnp.repeat(jnp.arange(rows), GRID_W).astype(F32)
    col = jnp.tile(jnp.arange(GRID_W), rows).astype(F32)
    n_freq = rot_dim // 4
    inv = ROPE_THETA ** (-jnp.arange(n_freq, dtype=F32) / n_freq)
    ang = jnp.concatenate([row[:, None] * inv, col[:, None] * inv], axis=-1)
    ang = jnp.concatenate([ang, jnp.zeros((n_ctx, rot_dim // 2), F32)], axis=0)
    return jnp.cos(ang), jnp.sin(ang)


def _rwkv_mixer(pa, n_lat, mu, w0, w_up, a0, a_up, g_up, k_k, k_a, r_k, lnx_w, lnx_b):
    r, v, kk, wl, kd, bb, bonus, g = _rwkv_prep(pa, n_lat // ROW_TILE, mu, w0, w_up, a0, a_up, g_up, k_k, k_a,
                                                 r_k.reshape(-1))
    yf, yr = _rwkv_scan(r, v, kk, wl, kd, bb, n_lat)
    return _rwkv_out(yf, yr, bonus, g, lnx_w, lnx_b)


def _diff_mixer(pb, n_lat, lq1, lk1, lq2, lk2, subln_g, lambda_init, tables):
    dh = subln_g.shape[0] // 2
    nh = pb.shape[2] // 3 // (2 * dh)
    lam = jnp.exp(jnp.sum(lq1 * lk1)) - jnp.exp(jnp.sum(lq2 * lk2)) + lambda_init
    q, k, v1 = _diff_prep(pb, tables, dh // 2, dh ** -0.5 * LOG2E)
    scale = 1.0 - lambda_init
    tq = ATTN_ROWS // 2
    return _diff_attention(q, k, v1, lam, subln_g, nh, scale, tq, n_lat)


def _gqa_mixer(pc, n_lat, q_norm_g, k_norm_g, n_heads, n_kv, tables):
    dh = q_norm_g.shape[0]
    assert dh == LANES
    q, k, v1 = _gqa_prep(pc, tables, q_norm_g, k_norm_g, n_heads, n_kv, dh ** -0.5 * LOG2E)
    group = n_heads // n_kv
    tq = ATTN_ROWS // group
    return _attention(q, k, v1, n_kv, group, dh, dh, tq, n_lat)


def _mla_mixer(pd, n_lat, q_norm_g, q_up, kv_norm_g, kv_up, n_heads, d_rope, tables):
    d_qk = q_up.shape[1] // n_heads
    q, k, v1 = _mla_prep(pd, tables, q_norm_g, q_up, kv_norm_g, kv_up, n_heads, d_rope, d_qk ** -0.5 * LOG2E)
    dk, d_v = 2 * LANES, LANES
    return _attention(q, k, v1, n_heads, 1, dk, d_v, ATTN_ROWS, n_lat)


def _pad_cols(w, mult):
    return jnp.pad(w, ((0, 0), (0, -w.shape[1] % mult)))


def kernel(x, c, ctx, c_ctx, ada_w, ada_b, norm_mix_g, norm_ffn_g, w_in, rwkv_mu, rwkv_w0, rwkv_w_up, rwkv_a0, rwkv_a_up, rwkv_g_up, rwkv_k_k, rwkv_k_a, rwkv_r_k, rwkv_lnx_w, rwkv_lnx_b, diff_lq1, diff_lk1, diff_lq2, diff_lk2, diff_subln_g, gqa_q_norm_g, gqa_k_norm_g, mla_q_norm_g, mla_q_up, mla_kv_norm_g, mla_kv_up, w_out, ffn_w_gate, ffn_w_up, ffn_w_down, moe_router, moe_w_gate, moe_w_up, moe_w_down, final_norm_g):
    nb, n_lat, d = x.shape
    n_ctx = ctx.shape[1]
    depth = ada_w.shape[0]
    assert n_ctx == ROW_TILE and n_lat % ROW_TILE == 0 and n_lat % GRID_W == 0
    n_lat_tiles = n_lat // ROW_TILE
    n_all_tiles = n_lat_tiles + 1

    gw = rwkv_k_k.shape[1]
    cols_a = 3 * gw + rwkv_w_up.shape[2] + rwkv_a_up.shape[2] + rwkv_g_up.shape[1]
    cols_b = 3 * gw
    gqa_dh = gqa_q_norm_g.shape[1]
    mla_rope = 64
    mla_heads = 8
    gqa_heads = gw // gqa_dh
    cols_d = mla_q_norm_g.shape[1] + mla_kv_norm_g.shape[1] + mla_rope
    cols_c = w_in.shape[2] - cols_a - cols_b - cols_d
    gqa_kv = (cols_c - gw) // (2 * gqa_dh)
    offs = (0, cols_a, cols_a + cols_b, cols_a + cols_b + cols_c, w_in.shape[2])

    rope_diff = _lane_tables(_rope_tables(n_lat, n_ctx, diff_subln_g.shape[1] // 2))
    rope_gqa = _lane_tables(_rope_tables(n_lat, n_ctx, gqa_dh))
    rope_mla = _lane_tables(_rope_tables(n_lat, n_ctx, mla_rope))

    cvec = jnp.concatenate([c, c_ctx[None, :], jnp.zeros((8 - nb - 1, d), F32)], axis=0)
    mods = _ada(cvec, ada_w, ada_b)[:, :nb + 1].reshape(depth, nb + 1, 6, 1, d)

    h = jnp.concatenate([x, ctx], axis=1)
    s_len = n_lat + n_ctx
    flat = lambda z: z.reshape(nb * s_len, z.shape[-1])
    for l in range(depth):
        need_ctx = l < depth - 1
        m = [mods[l, :, i] for i in range(6)]

        a = flat(_norm_mod(h, norm_mix_g[l], m[1], m[0], n_all_tiles, n_lat_tiles))
        groups = [_matmul(a, _pad_cols(w_in[l][:, offs[i]:offs[i + 1]], 768).astype(BF16), F32, 768)
                  .reshape(nb, s_len, -1) for i in range(4)]
        oa = _rwkv_mixer(groups[0], n_lat, rwkv_mu[l], rwkv_w0[l], rwkv_w_up[l],
                         rwkv_a0[l], rwkv_a_up[l], rwkv_g_up[l], rwkv_k_k[l], rwkv_k_a[l], rwkv_r_k[l],
                         rwkv_lnx_w[l], rwkv_lnx_b[l])
        lambda_init = 0.8 - 0.6 * math.exp(-0.3 * l)
        ob = _diff_mixer(groups[1], n_lat, diff_lq1[l], diff_lk1[l], diff_lq2[l], diff_lk2[l],
                         diff_subln_g[l], lambda_init, rope_diff)
        oc = _gqa_mixer(groups[2], n_lat, gqa_q_norm_g[l], gqa_k_norm_g[l], gqa_heads, gqa_kv, rope_gqa)
        od = _mla_mixer(groups[3], n_lat, mla_q_norm_g[l], mla_q_up[l], mla_kv_norm_g[l], mla_kv_up[l],
                        mla_heads, mla_rope, rope_mla)
        h = _matmul_resid([flat(o) for o in (oa, ob, oc, od)], w_out[l].astype(BF16), flat(h), m[2],
                          n_all_tiles).reshape(nb, s_len, d)

        j = l // 2
        if l % 2 == 0:
            d_ff = ffn_w_gate.shape[2]
            pad_f = -d_ff % 512
            w_g = jnp.pad(ffn_w_gate[j], ((0, 0), (0, pad_f))).astype(BF16)
            w_u = jnp.pad(ffn_w_up[j], ((0, 0), (0, pad_f))).astype(BF16)
            w_d = jnp.pad(ffn_w_down[j], ((0, pad_f), (0, 0))).astype(BF16)
            a = flat(_norm_mod(h, norm_ffn_g[l], m[4], m[3], n_all_tiles, n_lat_tiles))
            hid = _matmul_swiglu(a, w_g, w_u)
            h = _matmul_resid([hid], w_d, flat(h), m[5], n_all_tiles, tm=ROW_TILE).reshape(nb, s_len, d)
        else:
            assert not need_ctx
            a, logits = _norm_router(h, norm_ffn_g[l], m[4], m[3], moe_router[j], n_lat_tiles, n_lat_tiles)
            h = _moe_layer(a, logits, h, m[5][:nb], moe_w_gate[j].astype(BF16), moe_w_up[j].astype(BF16),
                           moe_w_down[j].astype(BF16))
    return _final_norm(h, final_norm_g, n_lat_tiles)
```

```python
import functools
import itertools
import math

import jax
import jax.numpy as jnp
from jax import lax
from jax.experimental import pallas as pl
from jax.experimental.pallas import tpu as pltpu

F32 = jnp.float32
BF16 = jnp.bfloat16

GRID_W = 64
ROPE_THETA = 10000.0
NORM_EPS = 1e-6
SUBLN_EPS = 1e-5
GN_EPS = 64e-5
TOP_K = 2

LANES = 128
ROW_TILE = 256
RWKV_HEAD = 64
RWKV_CHUNK = 64
RWKV_QUAD = 4
RWKV_STEP_QUADS = 4
MOE_ROWS = 512
MM_ROWS = 512
ATTN_KEYS = 1408
MLA_KEYS = 768
ATTN_ROWS = 1024
LOG2E = 1.4426950408889634
VMEM_LIMIT = 56 * 1024 * 1024


def _cparams(sem):
    return pltpu.CompilerParams(dimension_semantics=sem, vmem_limit_bytes=VMEM_LIMIT)


def _pick(n, pref):
    if n <= pref:
        return n
    t = pref - pref % LANES
    while t >= LANES:
        if n % t == 0:
            return t
        t -= LANES
    return n


def _ada_kernel(c_ref, w_ref, b_ref, o_ref):
    c = c_ref[...]
    s = (c * jax.nn.sigmoid(c)).astype(BF16)
    o_ref[...] = jnp.dot(s, w_ref[...].astype(BF16), preferred_element_type=F32) + b_ref[...]


def _ada(cvec, ada_w, ada_b):
    n_layers, d, n = ada_w.shape
    tn = _pick(n, 512)
    return pl.pallas_call(
        _ada_kernel,
        grid=(n_layers, n // tn),
        in_specs=[
            pl.BlockSpec((8, d), lambda l, j: (0, 0)),
            pl.BlockSpec((None, d, tn), lambda l, j: (l, 0, j)),
            pl.BlockSpec((None, 1, tn), lambda l, j: (l, 0, j)),
        ],
        out_specs=pl.BlockSpec((None, 8, tn), lambda l, j: (l, 0, j)),
        out_shape=jax.ShapeDtypeStruct((n_layers, 8, n), F32),
        compiler_params=_cparams(("parallel", "parallel")),
        name="ada",
    )(cvec, ada_w, ada_b.reshape(n_layers, 1, n))


def _mod_sel(n_lat_tiles, n_batch):
    return lambda b, i: (jnp.where(i >= n_lat_tiles, n_batch, b), 0, 0)


def _norm_mod_kernel(x_ref, g_ref, sc_ref, sh_ref, o_ref):
    x = x_ref[...]
    y = x * lax.rsqrt(jnp.mean(x * x, axis=-1, keepdims=True) + NORM_EPS) * g_ref[...]
    o_ref[...] = (y * (1.0 + sc_ref[...]) + sh_ref[...]).astype(o_ref.dtype)


def _norm_mod(x, g, scale, shift, n_tiles, n_lat_tiles, out_dtype=BF16):
    nb, _, d = x.shape
    sel = _mod_sel(n_lat_tiles, nb)
    return pl.pallas_call(
        _norm_mod_kernel,
        grid=(nb, n_tiles),
        in_specs=[
            pl.BlockSpec((None, ROW_TILE, d), lambda b, i: (b, i, 0)),
            pl.BlockSpec((1, d), lambda b, i: (0, 0)),
            pl.BlockSpec((None, 1, d), sel),
            pl.BlockSpec((None, 1, d), sel),
        ],
        out_specs=pl.BlockSpec((None, ROW_TILE, d), lambda b, i: (b, i, 0)),
        out_shape=jax.ShapeDtypeStruct((nb, n_tiles * ROW_TILE, d), out_dtype),
        compiler_params=_cparams(("parallel", "parallel")),
        name="norm_mod",
    )(x, g.reshape(1, d), scale, shift)


def _norm_router_kernel(x_ref, g_ref, sc_ref, sh_ref, rh_ref, rl_ref, o_ref, lg_ref):
    x = x_ref[...]
    y = x * lax.rsqrt(jnp.mean(x * x, axis=-1, keepdims=True) + NORM_EPS) * g_ref[...]
    a = y * (1.0 + sc_ref[...]) + sh_ref[...]
    o_ref[...] = a
    ah = a.astype(BF16)
    al = (a - ah.astype(F32)).astype(BF16)
    rh = rh_ref[...]
    lg_ref[...] = (jnp.dot(ah, rh, preferred_element_type=F32)
                   + jnp.dot(al, rh, preferred_element_type=F32)
                   + jnp.dot(ah, rl_ref[...], preferred_element_type=F32))


def _norm_router(x, g, scale, shift, router, n_tiles, n_lat_tiles):
    nb, _, d = x.shape
    n_exp = router.shape[1]
    r_pad = jnp.pad(router, ((0, 0), (0, LANES - n_exp)))
    r_hi = r_pad.astype(BF16)
    r_lo = (r_pad - r_hi.astype(F32)).astype(BF16)
    sel = _mod_sel(n_lat_tiles, nb)
    rows = n_tiles * ROW_TILE
    return pl.pallas_call(
        _norm_router_kernel,
        grid=(nb, n_tiles),
        in_specs=[
            pl.BlockSpec((None, ROW_TILE, d), lambda b, i: (b, i, 0)),
            pl.BlockSpec((1, d), lambda b, i: (0, 0)),
            pl.BlockSpec((None, 1, d), sel),
            pl.BlockSpec((None, 1, d), sel),
            pl.BlockSpec((d, LANES), lambda b, i: (0, 0)),
            pl.BlockSpec((d, LANES), lambda b, i: (0, 0)),
        ],
        out_specs=[
            pl.BlockSpec((None, ROW_TILE, d), lambda b, i: (b, i, 0)),
            pl.BlockSpec((None, ROW_TILE, LANES), lambda b, i: (b, i, 0)),
        ],
        out_shape=[
            jax.ShapeDtypeStruct((nb, rows, d), F32),
            jax.ShapeDtypeStruct((nb, rows, LANES), F32),
        ],
        compiler_params=_cparams(("parallel", "parallel")),
        name="norm_router",
    )(x, g.reshape(1, d), scale, shift, r_hi, r_lo)


def _final_norm_kernel(x_ref, g_ref, o_ref):
    x = x_ref[...]
    o_ref[...] = x * lax.rsqrt(jnp.mean(x * x, axis=-1, keepdims=True) + NORM_EPS) * g_ref[...]


def _final_norm(x, g, n_tiles):
    nb, _, d = x.shape
    rows = n_tiles * ROW_TILE
    return pl.pallas_call(
        _final_norm_kernel,
        grid=(nb, n_tiles),
        in_specs=[
            pl.BlockSpec((None, ROW_TILE, d), lambda b, i: (b, i, 0)),
            pl.BlockSpec((1, d), lambda b, i: (0, 0)),
        ],
        out_specs=pl.BlockSpec((None, ROW_TILE, d), lambda b, i: (b, i, 0)),
        out_shape=jax.ShapeDtypeStruct((nb, rows, d), F32),
        compiler_params=_cparams(("parallel", "parallel")),
        name="final_norm",
    )(x, g.reshape(1, d))


def _row_tile(rows):
    return MM_ROWS if rows % MM_ROWS == 0 else ROW_TILE


def _mm_kernel(a_ref, w_ref, o_ref):
    o_ref[...] = jnp.dot(a_ref[...], w_ref[...], preferred_element_type=F32).astype(o_ref.dtype)


def _matmul(a, w, out_dtype=F32, tn_pref=1024):
    rows, k = a.shape
    n = w.shape[1]
    tn = _pick(n, tn_pref)
    tm = _row_tile(rows)
    return pl.pallas_call(
        _mm_kernel,
        grid=(n // tn, rows // tm),
        in_specs=[
            pl.BlockSpec((tm, k), lambda j, i: (i, 0)),
            pl.BlockSpec((k, tn), lambda j, i: (0, j)),
        ],
        out_specs=pl.BlockSpec((tm, tn), lambda j, i: (i, j)),
        out_shape=jax.ShapeDtypeStruct((rows, n), out_dtype),
        compiler_params=_cparams(("parallel", "parallel")),
        name="matmul",
    )(a, w)


def _mm_swiglu_kernel(a_ref, wg_ref, wu_ref, o_ref):
    a = a_ref[...]
    g = jnp.dot(a, wg_ref[...], preferred_element_type=F32)
    u = jnp.dot(a, wu_ref[...], preferred_element_type=F32)
    o_ref[...] = (g * jax.nn.sigmoid(g) * u).astype(o_ref.dtype)


def _matmul_swiglu(a, wg, wu, tn_pref=1024):
    rows, k = a.shape
    n = wg.shape[1]
    tn = _pick(n, tn_pref)
    tm = _row_tile(rows)
    return pl.pallas_call(
        _mm_swiglu_kernel,
        grid=(n // tn, rows // tm),
        in_specs=[
            pl.BlockSpec((tm, k), lambda j, i: (i, 0)),
            pl.BlockSpec((k, tn), lambda j, i: (0, j)),
            pl.BlockSpec((k, tn), lambda j, i: (0, j)),
        ],
        out_specs=pl.BlockSpec((tm, tn), lambda j, i: (i, j)),
        out_shape=jax.ShapeDtypeStruct((rows, n), BF16),
        compiler_params=_cparams(("parallel", "parallel")),
        name="matmul_swiglu",
    )(a, wg, wu)


def _mm_resid_kernel(*refs, n_parts, n_gates):
    a_refs, w_refs = refs[:n_parts], refs[n_parts:2 * n_parts]
    res_ref = refs[2 * n_parts]
    gate_refs = refs[2 * n_parts + 1:2 * n_parts + 1 + n_gates]
    o_ref = refs[-1]
    acc = jnp.dot(a_refs[0][...], w_refs[0][...], preferred_element_type=F32)
    for a_ref, w_ref in zip(a_refs[1:], w_refs[1:]):
        acc = acc + jnp.dot(a_ref[...], w_ref[...], preferred_element_type=F32)
    rows = acc.shape[0] // n_gates
    for t, gate_ref in enumerate(gate_refs):
        sl = slice(t * rows, (t + 1) * rows)
        o_ref[sl, :] = res_ref[sl, :] + gate_ref[...] * acc[sl, :]


def _matmul_resid(parts, w, res, gate, tiles_per_batch, tm=None, tn_pref=512):
    rows, kg = parts[0].shape
    n_parts = len(parts)
    n = w.shape[1]
    nb = gate.shape[0] - 1
    tn = _pick(n, tn_pref)
    tm = tm or _row_tile(rows)
    n_gates = tm // ROW_TILE

    def gate_spec(t):
        def index(j, i):
            tile = i * n_gates + t
            pos = tile % tiles_per_batch
            return (jnp.where(pos == tiles_per_batch - 1, nb, tile // tiles_per_batch), 0, j)
        return pl.BlockSpec((None, 1, tn), index)

    return pl.pallas_call(
        functools.partial(_mm_resid_kernel, n_parts=n_parts, n_gates=n_gates),
        grid=(n // tn, rows // tm),
        in_specs=([pl.BlockSpec((tm, kg), lambda j, i: (i, 0))] * n_parts
                  + [pl.BlockSpec((kg, tn), lambda j, i, g=g: (g, j)) for g in range(n_parts)]
                  + [pl.BlockSpec((tm, tn), lambda j, i: (i, j))]
                  + [gate_spec(t) for t in range(n_gates)]),
        out_specs=pl.BlockSpec((tm, tn), lambda j, i: (i, j)),
        out_shape=jax.ShapeDtypeStruct((rows, n), F32),
        compiler_params=_cparams(("parallel", "parallel")),
        name="matmul_resid",
    )(*parts, *([w] * n_parts), res, *([gate] * n_gates))


_NT = (((1,), (1,)), ((), ()))


def _online_softmax(qs, k_ref, v_ref, ck):
    n_rows = qs.shape[0]
    m = jnp.full((n_rows, 1), -jnp.inf, F32)
    acc = jnp.zeros((n_rows, v_ref.shape[1]), F32)
    for c in range(k_ref.shape[0] // ck):
        keys = slice(c * ck, (c + 1) * ck)
        s = lax.dot_general(qs, k_ref[keys, :], _NT, preferred_element_type=F32)
        m_new = jnp.maximum(m, jnp.max(s, axis=-1, keepdims=True))
        p = jnp.exp2(s - m_new).astype(BF16)
        acc = jnp.exp2(m - m_new) * acc + jnp.dot(p, v_ref[keys, :], preferred_element_type=F32)
        m = m_new
    return acc


def _attend_rows(body, pre_args, pre_specs, q, k, v1, n_heads, qw, dk, ow, tq, n_lat, name, keys=ATTN_KEYS):
    nb, s_len, _ = q.shape
    n_ctx = s_len - n_lat
    dvp = v1.shape[2] // n_heads
    out_shape = jax.ShapeDtypeStruct((nb, s_len, n_heads * ow), BF16)

    def run(tq, n_q, q_off, k_rows, k_off, prev):
        kern = functools.partial(body, ck=_pick(k_rows, keys))
        specs = pre_specs + [
            pl.BlockSpec((None, tq, qw), lambda b, h, i: (b, i + q_off, h)),
            pl.BlockSpec((None, k_rows, dk), lambda b, h, i: (b, k_off, h)),
            pl.BlockSpec((None, k_rows, dvp), lambda b, h, i: (b, k_off, h)),
        ]
        args = pre_args + [q, k, v1, prev]
        return pl.pallas_call(
            lambda *refs: kern(*refs[:-2], refs[-1]),
            grid=(nb, n_heads, n_q),
            in_specs=specs + [pl.BlockSpec(memory_space=pl.ANY)],
            out_specs=pl.BlockSpec((None, tq, ow), lambda b, h, i: (b, i + q_off, h)),
            out_shape=out_shape,
            input_output_aliases={len(args) - 1: 0},
            compiler_params=_cparams(("parallel", "parallel", "parallel")),
            name=name,
        )(*args)

    tq = math.gcd(tq, n_lat)
    out = run(tq, n_lat // tq, 0, s_len, 0, jnp.zeros(out_shape.shape, out_shape.dtype))
    return run(n_ctx, 1, n_lat // n_ctx, n_ctx, n_lat // n_ctx, out)


def _attn_kernel(q_ref, k_ref, v_ref, o_ref, *, pieces, dv, ck):
    q = q_ref[...]
    tq, dk = q.shape[0], q.shape[1] // pieces
    qs = q if pieces == 1 else jnp.concatenate([q[:, g * dk:(g + 1) * dk] for g in range(pieces)], axis=0)
    acc = _online_softmax(qs, k_ref, v_ref, ck)
    o = acc[:, :dv] / acc[:, dv:dv + 1]
    for g in range(pieces):
        o_ref[:, g * dv:(g + 1) * dv] = o[g * tq:(g + 1) * tq].astype(o_ref.dtype)


def _attention(q, k, v1, n_kv_heads, group, dk, dv, tq, n_lat, keys=ATTN_KEYS):
    return _attend_rows(functools.partial(_attn_kernel, pieces=group, dv=dv), [], [], q, k, v1, n_kv_heads,
                        group * dk, dk, group * dv, tq, n_lat, "attention", keys)


def _diff_attn_kernel(lam_ref, g_ref, q_ref, k_ref, v_ref, o_ref, *, dv, ck, out_scale):
    q = q_ref[...]
    tq, width = q.shape
    first = lax.broadcasted_iota(jnp.int32, (tq, width), 1) < width // 2
    zero = jnp.zeros_like(q)
    qs = jnp.concatenate([jnp.where(first, q, zero), jnp.where(first, zero, q)], axis=0)
    acc = _online_softmax(qs, k_ref, v_ref, ck)
    o = acc[:, :dv] / acc[:, dv:dv + 1]
    o = o[:tq] - lam_ref[0] * o[tq:]
    o = o * lax.rsqrt(jnp.mean(o * o, axis=-1, keepdims=True) + SUBLN_EPS) * g_ref[...]
    o_ref[...] = (o * out_scale).astype(o_ref.dtype)


def _diff_attention(q, k, v1, lam, subln_g, n_heads, out_scale, tq, n_lat):
    dh = q.shape[2] // n_heads
    return _attend_rows(functools.partial(_diff_attn_kernel, dv=dh, out_scale=out_scale),
                        [lam.reshape(1).astype(F32), subln_g.reshape(1, dh)],
                        [pl.BlockSpec(memory_space=pltpu.SMEM), pl.BlockSpec((1, dh), lambda b, h, i: (0, 0))],
                        q, k, v1, n_heads, dh, dh, dh, tq, n_lat, "diff_attention")


def _split3(x):
    h = x.astype(BF16)
    r = x - h.astype(F32)
    m = r.astype(BF16)
    return h, m, (r - m.astype(F32)).astype(BF16)


def _rwkv_chunk(r, v, kk, wl, kd, bb, h_state, rev, out):
    c, w = r.shape
    nh = w // RWKV_HEAD
    assert c == RWKV_HEAD
    ri = lax.broadcasted_iota(jnp.int32, (c, c), 0)
    ci = lax.broadcasted_iota(jnp.int32, (c, c), 1)
    before_incl = (ci >= ri) if rev else (ci <= ri)
    tri = jnp.where(before_incl, 1.0, 0.0).astype(BF16)
    w_h, w_m, w_l = _split3(wl)
    cum = (jnp.dot(tri, w_h, preferred_element_type=F32) + jnp.dot(tri, w_m, preferred_element_type=F32)
           + jnp.dot(tri, w_l, preferred_element_type=F32))
    yield
    cum_prev = cum - wl
    last = 0 if rev else c - 1
    total = cum[last:last + 1, :]
    mid = cum[c // 2:c // 2 + 1, :]
    a = -kk
    e_mc = jnp.exp(mid - cum)
    e_end = jnp.exp(total - cum)

    lane = lax.broadcasted_iota(jnp.int32, (c, w), 1)
    lane_head, ts = lane // c, lane % c
    tr = lax.broadcasted_iota(jnp.int32, (c, w), 0)
    strict = (ts > tr) if rev else (ts < tr)
    incl = (ts >= tr) if rev else (ts <= tr)

    def stack(x):
        return jnp.concatenate([jnp.where(lane_head == h, x, 0.0) for h in range(nh)], axis=0).astype(BF16)

    def dot_nt(x, y):
        return lax.dot_general(x.astype(BF16), y.astype(BF16), _NT, preferred_element_type=F32)

    def dot_tn(x, y):
        return lax.dot_general(x.astype(BF16), y.astype(BF16), (((0,), (0,)), ((), ())), preferred_element_type=F32)

    def dot(x, y):
        return jnp.dot(x.astype(BF16), y.astype(BF16), preferred_element_type=F32)

    lhs = jnp.concatenate([a * jnp.exp(cum_prev - mid), r * jnp.exp(cum - mid)], axis=0)
    g_b = dot_nt(lhs, stack(bb * e_mc))
    g_k = dot_nt(lhs, stack(kd * e_mc))
    n_ab = jnp.where(strict, g_b[:c], 0.0)
    a_rb = jnp.where(incl, g_b[c:], 0.0)
    a_ak = jnp.where(strict, g_k[:c], 0.0)
    a_rk = jnp.where(incl, g_k[c:], 0.0)
    yield

    t_inv = jnp.where(ts == tr, 1.0, 0.0) + n_ab
    pw = n_ab
    for _ in range(int(math.log2(c)) - 1):
        pw = dot(pw, stack(pw))
        yield
        t_inv = t_inv + dot(pw, stack(t_inv))
    yield

    v_st = stack(v)
    x = dot(a * jnp.exp(cum_prev), h_state) + dot(a_ak, v_st)
    yield
    u = dot(t_inv, stack(x))
    yield
    y = dot(r * jnp.exp(cum), h_state) + dot(a_rb, stack(u)) + dot(a_rk, v_st)
    rw = lax.broadcasted_iota(jnp.int32, (w, w), 0)
    cw = lax.broadcasted_iota(jnp.int32, (w, w), 1)
    decay_end = jnp.where(rw == cw, jnp.broadcast_to(jnp.exp(total), (w, w)), 0.0)
    update = dot_tn(jnp.concatenate([bb * e_end, kd * e_end], axis=0), jnp.concatenate([u, v], axis=0))
    out.append((y, dot(decay_end, h_state) + jnp.where(rw // RWKV_HEAD == cw // RWKV_HEAD, update, 0.0)))


def _rwkv_scan_kernel(rf_ref, vf_ref, kkf_ref, wlf_ref, kdf_ref, bbf_ref,
                      rr_ref, vr_ref, kkr_ref, wlr_ref, kdr_ref, bbr_ref,
                      yf_ref, yr_ref, hf_ref, hr_ref):
    @pl.when(pl.program_id(2) == 0)
    def _():
        hf_ref[...] = jnp.zeros_like(hf_ref)
        hr_ref[...] = jnp.zeros_like(hr_ref)

    qw = RWKV_QUAD * RWKV_HEAD
    chains = []
    for i in range(hf_ref.shape[0]):
        sl = slice(i * qw, (i + 1) * qw)
        for refs, y_ref, h_ref, rev in (((rf_ref, vf_ref, kkf_ref, wlf_ref, kdf_ref, bbf_ref), yf_ref, hf_ref, False),
                                        ((rr_ref, vr_ref, kkr_ref, wlr_ref, kdr_ref, bbr_ref), yr_ref, hr_ref, True)):
            out = []
            chains.append((_rwkv_chunk(*(ref[:, sl] for ref in refs), h_ref[i], rev, out), out, y_ref, h_ref, i, sl))
    for _ in itertools.zip_longest(*(chain[0] for chain in chains)):
        pass
    for _, out, y_ref, h_ref, i, sl in chains:
        y_ref[:, sl], h_ref[i] = out[0]


def _rwkv_scan(r, v, kk, wl, kd, bb, n_lat):
    nb, s_len, w = r.shape
    c = RWKV_CHUNK
    qw = RWKV_QUAD * RWKV_HEAD
    nq = RWKV_STEP_QUADS if w % (RWKV_STEP_QUADS * qw) == 0 else 1
    bw = nq * qw
    n_l, n_all = n_lat // c, s_len // c
    n_c = n_all - n_l

    def fwd(s):
        return jnp.where(s < n_c, n_l + s, s - n_c)

    def rev(s):
        return n_all - 1 - s

    def shared(order):
        return pl.BlockSpec((None, c, bw), lambda b, q, s: (b, order(s), q))

    def per_dir(d, order):
        return pl.BlockSpec((None, None, c, bw), lambda b, q, s: (d, b, order(s), q))

    yf, yr = pl.pallas_call(
        _rwkv_scan_kernel,
        grid=(nb, w // bw, n_all),
        in_specs=[shared(fwd), shared(fwd), shared(fwd), per_dir(0, fwd), per_dir(0, fwd), per_dir(0, fwd),
                  shared(rev), shared(rev), shared(rev), per_dir(1, rev), per_dir(1, rev), per_dir(1, rev)],
        out_specs=[shared(fwd), shared(rev)],
        out_shape=[jax.ShapeDtypeStruct((nb, s_len, w), F32)] * 2,
        scratch_shapes=[pltpu.VMEM((nq, qw, qw), F32), pltpu.VMEM((nq, qw, qw), F32)],
        compiler_params=_cparams(("parallel", "parallel", "arbitrary")),
        name="rwkv_scan",
    )(r, v, kk, wl, kd, bb, r, v, kk, wl, kd, bb)
    return yf, yr


def _group_sum(x, ones_bd):
    hi = x.astype(BF16)
    lo = (x - hi.astype(F32)).astype(BF16)
    return jnp.dot(hi, ones_bd, preferred_element_type=F32) + jnp.dot(lo, ones_bd, preferred_element_type=F32)


def _rwkv_prep_kernel(x_ref, prev_ref, next_ref, mu_ref, kkg_ref, ka_ref, rk_ref, w0_ref, a0_ref, wup_ref, aup_ref,
                      gup_ref, ones_ref, r_ref, v_ref, kk_ref, wl_ref, kd_ref, bb_ref, bonus_ref, g_ref,
                      *, n_lat_tiles, gw, lora_w, lora_a):
    i = pl.program_id(1)
    x = x_ref[...]
    rows = x.shape[0]
    row = lax.broadcasted_iota(jnp.int32, x.shape, 0)
    has_prev = jnp.logical_and(i > 0, i < n_lat_tiles)
    has_next = i < n_lat_tiles - 1
    prev_row = jnp.where(has_prev, prev_ref[7:8, :], 0.0)
    next_row = jnp.where(has_next, next_ref[0:1, :], 0.0)
    prev = jnp.where(row == 0, prev_row, pltpu.roll(x, 1, axis=0))
    nxt = jnp.where(row == rows - 1, next_row, pltpu.roll(x, rows - 1, axis=0))
    ps = x + mu_ref[...] * (0.5 * (prev + nxt) - x)
    r, k, v = ps[:, :gw], ps[:, gw:2 * gw], ps[:, 2 * gw:3 * gw]
    o = 3 * gw
    wd, ad, gd = ps[:, o:o + lora_w], ps[:, o + lora_w:o + lora_w + lora_a], ps[:, o + lora_w + lora_a:]
    ones_bd = ones_ref[...]
    kk = k * kkg_ref[...]
    kk = kk * lax.rsqrt(jnp.maximum(_group_sum(kk * kk, ones_bd), 1e-24))
    tanh_wd = jnp.tanh(wd).astype(BF16)
    ad_b = ad.astype(BF16)
    kd_sum = jnp.zeros_like(k)
    for d in range(2):
        z = w0_ref[d] + jnp.dot(tanh_wd, wup_ref[d], preferred_element_type=F32)
        softplus_neg = jnp.maximum(-z, 0.0) + jnp.log1p(jnp.exp(-jnp.abs(z)))
        wl_ref[d] = -jnp.exp(-softplus_neg - 0.5)
        a_sig = jax.nn.sigmoid(a0_ref[d] + jnp.dot(ad_b, aup_ref[d], preferred_element_type=F32))
        kd = k * (1.0 + (a_sig - 1.0) * ka_ref[...])
        kd_ref[d] = kd
        bb_ref[d] = kk * a_sig
        kd_sum = kd_sum + kd
    r_ref[...] = r
    v_ref[...] = v
    kk_ref[...] = kk
    bonus_ref[...] = _group_sum(r * kd_sum * rk_ref[...], ones_bd) * v
    g_ref[...] = jnp.dot(jax.nn.sigmoid(gd).astype(BF16), gup_ref[...], preferred_element_type=F32)


def _head_ones(width, head):
    idx = jnp.arange(width) // head
    return (idx[:, None] == idx[None, :]).astype(BF16)


def _rwkv_prep(pa, n_lat_tiles, mu, w0, w_up, a0, a_up, g_up, k_k, k_a, r_k):
    nb, s_len, wp = pa.shape
    gw = k_k.shape[0]
    lora_w, lora_a = w_up.shape[1], a_up.shape[1]
    lora_g = wp - 3 * gw - lora_w - lora_a
    n_tiles = s_len // ROW_TILE
    sub = ROW_TILE // 8
    row = lambda z: z.reshape(1, -1)
    full = lambda shape: pl.BlockSpec(shape, lambda b, i: (0,) * len(shape))
    tile = lambda w: pl.BlockSpec((None, ROW_TILE, w), lambda b, i: (b, i, 0))
    tile2 = pl.BlockSpec((2, None, ROW_TILE, gw), lambda b, i: (0, b, i, 0))
    one = jax.ShapeDtypeStruct((nb, s_len, gw), F32)
    two = jax.ShapeDtypeStruct((2, nb, s_len, gw), F32)
    return pl.pallas_call(
        functools.partial(_rwkv_prep_kernel, n_lat_tiles=n_lat_tiles, gw=gw, lora_w=lora_w, lora_a=lora_a),
        grid=(nb, n_tiles),
        in_specs=[
            tile(wp),
            pl.BlockSpec((None, 8, wp), lambda b, i: (b, jnp.maximum(i * sub - 1, 0), 0)),
            pl.BlockSpec((None, 8, wp), lambda b, i: (b, jnp.minimum((i + 1) * sub, s_len // 8 - 1), 0)),
            full((1, wp)), full((1, gw)), full((1, gw)), full((1, gw)),
            full((2, 1, gw)), full((2, 1, gw)),
            full((2, lora_w, gw)), full((2, lora_a, gw)), full((lora_g, gw)), full((gw, gw)),
        ],
        out_specs=[tile(gw), tile(gw), tile(gw), tile2, tile2, tile2, tile(gw), tile(gw)],
        out_shape=[one, one, one, two, two, two, one, one],
        compiler_params=_cparams(("parallel", "parallel")),
        name="rwkv_prep",
    )(pa, pa, pa, row(jnp.pad(mu, (0, wp - mu.shape[0]))), row(k_k), row(k_a), row(r_k),
      w0.reshape(2, 1, gw), a0.reshape(2, 1, gw), w_up.astype(BF16), a_up.astype(BF16),
      jnp.pad(g_up, ((0, lora_g - g_up.shape[0]), (0, 0))).astype(BF16), _head_ones(gw, RWKV_HEAD))


def _rwkv_out_kernel(yf_ref, yr_ref, bonus_ref, g_ref, lw_ref, lb_ref, ones_ref, o_ref):
    ones_bd = ones_ref[...]
    y = yf_ref[...] + yr_ref[...]
    inv_n = 1.0 / RWKV_HEAD
    dev = y - _group_sum(y, ones_bd) * inv_n
    var = _group_sum(dev * dev, ones_bd) * inv_n
    yn = dev * lax.rsqrt(var + GN_EPS) * lw_ref[...] + lb_ref[...]
    o_ref[...] = ((yn + bonus_ref[...]) * g_ref[...]).astype(o_ref.dtype)


def _rwkv_out(yf, yr, bonus, g, lnx_w, lnx_b):
    nb, s_len, gw = yf.shape
    tile = pl.BlockSpec((None, ROW_TILE, gw), lambda b, i: (b, i, 0))
    vec = pl.BlockSpec((1, gw), lambda b, i: (0, 0))
    return pl.pallas_call(
        _rwkv_out_kernel,
        grid=(nb, s_len // ROW_TILE),
        in_specs=[tile, tile, tile, tile, vec, vec, pl.BlockSpec((gw, gw), lambda b, i: (0, 0))],
        out_specs=tile,
        out_shape=jax.ShapeDtypeStruct((nb, s_len, gw), BF16),
        compiler_params=_cparams(("parallel", "parallel")),
        name="rwkv_out",
    )(yf, yr, bonus, g, lnx_w.reshape(1, gw), lnx_b.reshape(1, gw), _head_ones(gw, RWKV_HEAD))


def _rope_block(xb, cos_t, sin_t, half):
    if 2 * half == LANES:
        partner = pltpu.roll(xb, half, axis=1)
    else:
        lane = lax.broadcasted_iota(jnp.int32, xb.shape, 1)
        partner = jnp.where(lane % (2 * half) < half, pltpu.roll(xb, LANES - half, axis=1),
                            pltpu.roll(xb, half, axis=1))
    return xb * cos_t + partner * sin_t


def _lane_tables(cs):
    cos, sin = cs
    reps = LANES // (2 * cos.shape[1])
    return (jnp.tile(jnp.concatenate([cos, cos], axis=1), (1, reps)),
            jnp.tile(jnp.concatenate([-sin, sin], axis=1), (1, reps)))


def _ones_column(rows, dtype):
    return jnp.where(lax.broadcasted_iota(jnp.int32, (rows, LANES), 1) == 0, 1.0, 0.0).astype(dtype)


def _blk(b):
    return slice(b * LANES, (b + 1) * LANES)


def _diff_prep_kernel(x_ref, cos_ref, sin_ref, q_ref, k_ref, v_ref, *, gw, half, q_scale):
    cos_t, sin_t = cos_ref[...], sin_ref[...]
    e0 = _ones_column(x_ref.shape[0], v_ref.dtype)
    nblk = gw // LANES
    for b in range(nblk):
        q_ref[:, _blk(b)] = (_rope_block(x_ref[:, _blk(b)], cos_t, sin_t, half) * q_scale).astype(q_ref.dtype)
        k_ref[:, _blk(b)] = _rope_block(x_ref[:, _blk(nblk + b)], cos_t, sin_t, half).astype(k_ref.dtype)
        v_ref[:, _blk(2 * b)] = x_ref[:, _blk(2 * nblk + b)].astype(v_ref.dtype)
        v_ref[:, _blk(2 * b + 1)] = e0


def _diff_prep(pb, tables, half, q_scale):
    nb, s_len, w3 = pb.shape
    gw = w3 // 3
    tile = lambda w: pl.BlockSpec((None, ROW_TILE, w), lambda b, i: (b, i, 0))
    tab = pl.BlockSpec((ROW_TILE, LANES), lambda b, i: (i, 0))
    return pl.pallas_call(
        functools.partial(_diff_prep_kernel, gw=gw, half=half, q_scale=q_scale),
        grid=(nb, s_len // ROW_TILE),
        in_specs=[tile(w3), tab, tab],
        out_specs=[tile(gw), tile(gw), tile(2 * gw)],
        out_shape=[jax.ShapeDtypeStruct((nb, s_len, gw), BF16), jax.ShapeDtypeStruct((nb, s_len, gw), BF16),
                   jax.ShapeDtypeStruct((nb, s_len, 2 * gw), BF16)],
        compiler_params=_cparams(("parallel", "parallel")),
        name="diff_prep",
    )(pb, *tables)


def _gqa_prep_kernel(x_ref, cos_ref, sin_ref, qg_ref, kg_ref, q_ref, k_ref, v_ref, *, n_heads, n_kv, q_scale):
    cos_t, sin_t = cos_ref[...], sin_ref[...]
    e0 = _ones_column(x_ref.shape[0], v_ref.dtype)

    def normed_rope(xb, g):
        xb = xb * lax.rsqrt(jnp.mean(xb * xb, axis=-1, keepdims=True) + NORM_EPS) * g
        return _rope_block(xb, cos_t, sin_t, LANES // 2)

    for b in range(n_heads):
        q_ref[:, _blk(b)] = (normed_rope(x_ref[:, _blk(b)], qg_ref[...]) * q_scale).astype(q_ref.dtype)
    for b in range(n_kv):
        k_ref[:, _blk(b)] = normed_rope(x_ref[:, _blk(n_heads + b)], kg_ref[...]).astype(k_ref.dtype)
        v_ref[:, _blk(2 * b)] = x_ref[:, _blk(n_heads + n_kv + b)].astype(v_ref.dtype)
        v_ref[:, _blk(2 * b + 1)] = e0


def _gqa_prep(pc, tables, q_norm_g, k_norm_g, n_heads, n_kv, q_scale):
    nb, s_len, w = pc.shape
    tile = lambda w_: pl.BlockSpec((None, ROW_TILE, w_), lambda b, i: (b, i, 0))
    tab = pl.BlockSpec((ROW_TILE, LANES), lambda b, i: (i, 0))
    vec = pl.BlockSpec((1, LANES), lambda b, i: (0, 0))
    return pl.pallas_call(
        functools.partial(_gqa_prep_kernel, n_heads=n_heads, n_kv=n_kv, q_scale=q_scale),
        grid=(nb, s_len // ROW_TILE),
        in_specs=[tile(w), tab, tab, vec, vec],
        out_specs=[tile(n_heads * LANES), tile(n_kv * LANES), tile(2 * n_kv * LANES)],
        out_shape=[jax.ShapeDtypeStruct((nb, s_len, n_heads * LANES), BF16),
                   jax.ShapeDtypeStruct((nb, s_len, n_kv * LANES), BF16),
                   jax.ShapeDtypeStruct((nb, s_len, 2 * n_kv * LANES), BF16)],
        compiler_params=_cparams(("parallel", "parallel")),
        name="gqa_prep",
    )(pc, *tables, q_norm_g.reshape(1, LANES), k_norm_g.reshape(1, LANES))


def _mla_prep_kernel(x_ref, cos_ref, sin_ref, qg_ref, kvg_ref, qup_ref, kvup_ref, q_ref, k_ref, v_ref,
                     *, q_lora, kv_lora, n_heads, half, q_scale):
    cos_t, sin_t = cos_ref[...], sin_ref[...]
    e0 = _ones_column(x_ref.shape[0], v_ref.dtype)

    def normed(z, g):
        return (z * lax.rsqrt(jnp.mean(z * z, axis=-1, keepdims=True) + NORM_EPS) * g).astype(BF16)

    q = jnp.dot(normed(x_ref[:, :q_lora], qg_ref[...]), qup_ref[...], preferred_element_type=F32)
    kv = jnp.dot(normed(x_ref[:, q_lora:q_lora + kv_lora], kvg_ref[...]), kvup_ref[...], preferred_element_type=F32)
    k_rope = _rope_block(x_ref[:, q_lora + kv_lora:q_lora + kv_lora + LANES], cos_t, sin_t, half).astype(k_ref.dtype)
    for h in range(n_heads):
        q_ref[:, _blk(2 * h)] = (q[:, _blk(2 * h)] * q_scale).astype(q_ref.dtype)
        q_ref[:, _blk(2 * h + 1)] = (_rope_block(q[:, _blk(2 * h + 1)], cos_t, sin_t, half) * q_scale).astype(q_ref.dtype)
        k_ref[:, _blk(2 * h)] = kv[:, _blk(h)].astype(k_ref.dtype)
        k_ref[:, _blk(2 * h + 1)] = k_rope
        v_ref[:, _blk(2 * h)] = kv[:, _blk(n_heads + h)].astype(v_ref.dtype)
        v_ref[:, _blk(2 * h + 1)] = e0


def _mla_prep(pd, tables, q_norm_g, q_up, kv_norm_g, kv_up, n_heads, d_rope, q_scale):
    nb, s_len, wp = pd.shape
    q_lora, kv_lora = q_norm_g.shape[0], kv_norm_g.shape[0]
    d_qk = q_up.shape[1] // n_heads
    d_nope = d_qk - d_rope
    assert d_nope == LANES and kv_up.shape[1] == n_heads * 2 * LANES and q_lora + kv_lora + LANES <= wp
    qw = jnp.pad(q_up.reshape(q_lora, n_heads, d_qk), ((0, 0), (0, 0), (0, 2 * LANES - d_qk)))
    qw = qw.reshape(q_lora, n_heads * 2 * LANES).astype(BF16)
    kvw = kv_up.reshape(kv_lora, n_heads, 2, LANES).transpose(0, 2, 1, 3).reshape(kv_lora, 2 * n_heads * LANES)
    tile = lambda w_: pl.BlockSpec((None, ROW_TILE, w_), lambda b, i: (b, i, 0))
    tab = pl.BlockSpec((ROW_TILE, LANES), lambda b, i: (i, 0))
    full = lambda shape: pl.BlockSpec(shape, lambda b, i: (0,) * len(shape))
    wide = n_heads * 2 * LANES
    out = jax.ShapeDtypeStruct((nb, s_len, wide), BF16)
    return pl.pallas_call(
        functools.partial(_mla_prep_kernel, q_lora=q_lora, kv_lora=kv_lora, n_heads=n_heads, half=d_rope // 2,
                          q_scale=q_scale),
        grid=(nb, s_len // ROW_TILE),
        in_specs=[tile(wp), tab, tab, full((1, q_lora)), full((1, kv_lora)), full((q_lora, wide)),
                  full((kv_lora, wide))],
        out_specs=[tile(wide), tile(wide), tile(wide)],
        out_shape=[out, out, out],
        compiler_params=_cparams(("parallel", "parallel")),
        name="mla_prep",
    )(pd, *tables, q_norm_g.reshape(1, q_lora), kv_norm_g.reshape(1, kv_lora), qw, kvw.astype(BF16))


def _moe_gather_kernel(src_ref, used_ref, tok_ref, o_ref, buf_ref, sem_ref):
    base = pl.program_id(0) * MOE_ROWS

    def copy(r):
        return pltpu.make_async_copy(tok_ref.at[pl.ds(src_ref[base + r], 1), :], buf_ref.at[pl.ds(r, 1), :],
                                     sem_ref.at[0])

    def start(r, carry):
        copy(r).start()
        return carry

    @pl.when(pl.program_id(0) < used_ref[0])
    def _():
        lax.fori_loop(0, MOE_ROWS, start, 0, unroll=8)
        pltpu.make_async_copy(tok_ref.at[pl.ds(0, MOE_ROWS), :], buf_ref, sem_ref.at[0]).wait()
        o_ref[...] = buf_ref[...].astype(o_ref.dtype)

    @pl.when(pl.program_id(0) >= used_ref[0])
    def _():
        o_ref[...] = jnp.zeros_like(o_ref)


def _used_block(i, used):
    return jnp.minimum(i, used[0] - 1)


def _moe_gather(tok, src, used, n_blocks):
    d = tok.shape[1]
    return pl.pallas_call(
        _moe_gather_kernel,
        grid_spec=pltpu.PrefetchScalarGridSpec(
            num_scalar_prefetch=2,
            grid=(n_blocks,),
            in_specs=[pl.BlockSpec(memory_space=pl.ANY)],
            out_specs=pl.BlockSpec((MOE_ROWS, d), lambda i, src, used: (i, 0)),
            scratch_shapes=[pltpu.VMEM((MOE_ROWS, d), F32), pltpu.SemaphoreType.DMA((1,))],
        ),
        out_shape=jax.ShapeDtypeStruct((n_blocks * MOE_ROWS, d), BF16),
        compiler_params=_cparams(("arbitrary",)),
        name="moe_gather",
    )(src, used, tok)


def _expert_changed(be_ref, i):
    return jnp.logical_or(i == 0, be_ref[i] != be_ref[jnp.maximum(i - 1, 0)])


def _moe_up_kernel(be_ref, used_ref, a_ref, wg_ref, wu_ref, o_ref, wg_bf, wu_bf):
    i = pl.program_id(1)

    @pl.when(i < used_ref[0])
    def _():
        @pl.when(_expert_changed(be_ref, i))
        def _():
            wg_bf[...] = wg_ref[...].astype(BF16)
            wu_bf[...] = wu_ref[...].astype(BF16)

        a = a_ref[...]
        g = jnp.dot(a, wg_bf[...], preferred_element_type=F32)
        u = jnp.dot(a, wu_bf[...], preferred_element_type=F32)
        o_ref[...] = (g * jax.nn.sigmoid(g) * u).astype(o_ref.dtype)

    @pl.when(i >= used_ref[0])
    def _():
        o_ref[...] = jnp.zeros_like(o_ref)


def _moe_block_maps():
    rows = lambda j, i, be, used: (_used_block(i, used), 0)
    weight = lambda j, i, be, used: (be[_used_block(i, used)], 0, j)
    out = lambda j, i, be, used: (i, j)
    return rows, weight, out


def _moe_up(buf, block_expert, used, wg, wu, tn_pref=512):
    rows, d = buf.shape
    f = wg.shape[2]
    tn = _pick(f, tn_pref)
    n_blocks = rows // MOE_ROWS
    row_map, w_map, out_map = _moe_block_maps()
    return pl.pallas_call(
        _moe_up_kernel,
        grid_spec=pltpu.PrefetchScalarGridSpec(
            num_scalar_prefetch=2,
            grid=(f // tn, n_blocks),
            in_specs=[
                pl.BlockSpec((MOE_ROWS, d), row_map),
                pl.BlockSpec((None, d, tn), w_map),
                pl.BlockSpec((None, d, tn), w_map),
            ],
            out_specs=pl.BlockSpec((MOE_ROWS, tn), out_map),
            scratch_shapes=[pltpu.VMEM((d, tn), BF16), pltpu.VMEM((d, tn), BF16)],
        ),
        out_shape=jax.ShapeDtypeStruct((rows, f), BF16),
        compiler_params=_cparams(("arbitrary", "arbitrary")),
        name="moe_up",
    )(block_expert, used, buf, wg, wu)


def _moe_down_kernel(be_ref, used_ref, h_ref, w_ref, o_ref, w_bf):
    i = pl.program_id(1)

    @pl.when(i < used_ref[0])
    def _():
        @pl.when(_expert_changed(be_ref, i))
        def _():
            w_bf[...] = w_ref[...].astype(BF16)

        o_ref[...] = jnp.dot(h_ref[...], w_bf[...], preferred_element_type=F32)

    @pl.when(i >= used_ref[0])
    def _():
        o_ref[...] = jnp.zeros_like(o_ref)


def _moe_down(hid, block_expert, used, wd, tn_pref=512):
    rows, f = hid.shape
    d = wd.shape[2]
    tn = _pick(d, tn_pref)
    n_blocks = rows // MOE_ROWS
    row_map, w_map, out_map = _moe_block_maps()
    return pl.pallas_call(
        _moe_down_kernel,
        grid_spec=pltpu.PrefetchScalarGridSpec(
            num_scalar_prefetch=2,
            grid=(d // tn, n_blocks),
            in_specs=[
                pl.BlockSpec((MOE_ROWS, f), row_map),
                pl.BlockSpec((None, f, tn), w_map),
            ],
            out_specs=pl.BlockSpec((MOE_ROWS, tn), out_map),
            scratch_shapes=[pltpu.VMEM((f, tn), BF16)],
        ),
        out_shape=jax.ShapeDtypeStruct((rows, d), F32),
        compiler_params=_cparams(("arbitrary", "arbitrary")),
        name="moe_down",
    )(block_expert, used, hid, wd)


def _moe_combine_kernel(dest_ref, y_ref, x_ref, gate_ref, g0_ref, g1_ref, o_ref, b0_ref, b1_ref, sem_ref):
    base = (pl.program_id(0) * pl.num_programs(1) + pl.program_id(1)) * ROW_TILE

    def copies(r):
        t = (base + r) * TOP_K
        return (pltpu.make_async_copy(y_ref.at[pl.ds(dest_ref[t], 1), :], b0_ref.at[pl.ds(r, 1), :], sem_ref.at[0]),
                pltpu.make_async_copy(y_ref.at[pl.ds(dest_ref[t + 1], 1), :], b1_ref.at[pl.ds(r, 1), :],
                                      sem_ref.at[1]))

    def start(r, carry):
        c0, c1 = copies(r)
        c0.start()
        c1.start()
        return carry

    lax.fori_loop(0, ROW_TILE, start, 0, unroll=8)
    pltpu.make_async_copy(y_ref.at[pl.ds(0, ROW_TILE), :], b0_ref, sem_ref.at[0]).wait()
    pltpu.make_async_copy(y_ref.at[pl.ds(0, ROW_TILE), :], b1_ref, sem_ref.at[1]).wait()
    o_ref[...] = x_ref[...] + gate_ref[...] * (g0_ref[...] * b0_ref[...] + g1_ref[...] * b1_ref[...])


def _moe_combine(y, dest, x, gate, g0, g1):
    nb, _, d = x.shape
    t = g0.shape[1]
    return pl.pallas_call(
        _moe_combine_kernel,
        grid_spec=pltpu.PrefetchScalarGridSpec(
            num_scalar_prefetch=1,
            grid=(nb, t // ROW_TILE),
            in_specs=[
                pl.BlockSpec(memory_space=pl.ANY),
                pl.BlockSpec((None, ROW_TILE, d), lambda b, i, dest: (b, i, 0)),
                pl.BlockSpec((None, 1, d), lambda b, i, dest: (b, 0, 0)),
                pl.BlockSpec((None, ROW_TILE, 1), lambda b, i, dest: (b, i, 0)),
                pl.BlockSpec((None, ROW_TILE, 1), lambda b, i, dest: (b, i, 0)),
            ],
            out_specs=pl.BlockSpec((None, ROW_TILE, d), lambda b, i, dest: (b, i, 0)),
            scratch_shapes=[pltpu.VMEM((ROW_TILE, d), F32), pltpu.VMEM((ROW_TILE, d), F32),
                            pltpu.SemaphoreType.DMA((2,))],
        ),
        out_shape=jax.ShapeDtypeStruct((nb, t, d), F32),
        compiler_params=_cparams(("arbitrary", "arbitrary")),
        name="moe_combine",
    )(dest, y, x, gate, g0, g1)


def _moe_layer(a_tok, logits, x, gate, wg, wu, wd):
    nb, t, d = a_tok.shape
    n_exp = wg.shape[0]
    n_tok = nb * t
    n_assign = n_tok * TOP_K
    top_logit, top_idx = lax.top_k(logits.reshape(n_tok, -1)[:, :n_exp], TOP_K)
    gates = jax.nn.softmax(top_logit, axis=-1)
    flat_e = top_idx.reshape(-1)
    onehot = (flat_e[:, None] == jnp.arange(n_exp)[None, :]).astype(jnp.int32)
    rank = jnp.take_along_axis(jnp.cumsum(onehot, axis=0) - onehot, flat_e[:, None], axis=1)[:, 0]
    counts = jnp.sum(onehot, axis=0)
    padded = (counts + MOE_ROWS - 1) // MOE_ROWS * MOE_ROWS
    pad_end = jnp.cumsum(padded)
    pad_start = pad_end - padded
    dest = (pad_start[flat_e] + rank).astype(jnp.int32)
    n_blocks = -(-n_assign // MOE_ROWS) + n_exp
    src = jnp.zeros((n_blocks * MOE_ROWS,), jnp.int32).at[dest].set(jnp.arange(n_assign, dtype=jnp.int32) // TOP_K)
    block_expert = jnp.minimum(
        jnp.searchsorted(pad_end, jnp.arange(n_blocks) * MOE_ROWS, side='right'), n_exp - 1).astype(jnp.int32)
    used = (pad_end[-1:] // MOE_ROWS).astype(jnp.int32)
    buf = _moe_gather(a_tok.reshape(n_tok, d), src, used, n_blocks)
    hid = _moe_up(buf, block_expert, used, wg, wu)
    y = _moe_down(hid, block_expert, used, wd)
    return _moe_combine(y, dest, x, gate, gates[:, 0].reshape(nb, t, 1), gates[:, 1].reshape(nb, t, 1))


def _rope_tables(n_lat, n_ctx, rot_dim):
    rows = n_lat // GRID_W
    row = jnp.repeat(jnp.arange(rows), GRID_W).astype(F32)
    col = jnp.tile(jnp.arange(GRID_W), rows).astype(F32)
    n_freq = rot_dim // 4
    inv = ROPE_THETA ** (-jnp.arange(n_freq, dtype=F32) / n_freq)
    ang = jnp.concatenate([row[:, None] * inv, col[:, None] * inv], axis=-1)
    ang = jnp.concatenate([ang, jnp.zeros((n_ctx, rot_dim // 2), F32)], axis=0)
    return jnp.cos(ang), jnp.sin(ang)


def _rwkv_mixer(pa, n_lat, mu, w0, w_up, a0, a_up, g_up, k_k, k_a, r_k, lnx_w, lnx_b):
    r, v, kk, wl, kd, bb, bonus, g = _rwkv_prep(pa, n_lat // ROW_TILE, mu, w0, w_up, a0, a_up, g_up, k_k, k_a,
                                                 r_k.reshape(-1))
    yf, yr = _rwkv_scan(r, v, kk, wl, kd, bb, n_lat)
    return _rwkv_out(yf, yr, bonus, g, lnx_w, lnx_b)


def _diff_mixer(pb, n_lat, lq1, lk1, lq2, lk2, subln_g, lambda_init, tables):
    dh = subln_g.shape[0] // 2
    nh = pb.shape[2] // 3 // (2 * dh)
    lam = jnp.exp(jnp.sum(lq1 * lk1)) - jnp.exp(jnp.sum(lq2 * lk2)) + lambda_init
    q, k, v1 = _diff_prep(pb, tables, dh // 2, dh ** -0.5 * LOG2E)
    scale = 1.0 - lambda_init
    tq = ATTN_ROWS // 2
    return _diff_attention(q, k, v1, lam, subln_g, nh, scale, tq, n_lat)


def _gqa_mixer(pc, n_lat, q_norm_g, k_norm_g, n_heads, n_kv, tables):
    dh = q_norm_g.shape[0]
    assert dh == LANES
    q, k, v1 = _gqa_prep(pc, tables, q_norm_g, k_norm_g, n_heads, n_kv, dh ** -0.5 * LOG2E)
    group = n_heads // n_kv
    tq = ATTN_ROWS // group
    return _attention(q, k, v1, n_kv, group, dh, dh, tq, n_lat)


def _mla_mixer(pd, n_lat, q_norm_g, q_up, kv_norm_g, kv_up, n_heads, d_rope, tables):
    d_qk = q_up.shape[1] // n_heads
    q, k, v1 = _mla_prep(pd, tables, q_norm_g, q_up, kv_norm_g, kv_up, n_heads, d_rope, d_qk ** -0.5 * LOG2E)
    dk, d_v = 2 * LANES, LANES
    return _attention(q, k, v1, n_heads, 1, dk, d_v, ATTN_ROWS, n_lat, MLA_KEYS)


def _pad_cols(w, mult):
    return jnp.pad(w, ((0, 0), (0, -w.shape[1] % mult)))


def kernel(x, c, ctx, c_ctx, ada_w, ada_b, norm_mix_g, norm_ffn_g, w_in, rwkv_mu, rwkv_w0, rwkv_w_up, rwkv_a0, rwkv_a_up, rwkv_g_up, rwkv_k_k, rwkv_k_a, rwkv_r_k, rwkv_lnx_w, rwkv_lnx_b, diff_lq1, diff_lk1, diff_lq2, diff_lk2, diff_subln_g, gqa_q_norm_g, gqa_k_norm_g, mla_q_norm_g, mla_q_up, mla_kv_norm_g, mla_kv_up, w_out, ffn_w_gate, ffn_w_up, ffn_w_down, moe_router, moe_w_gate, moe_w_up, moe_w_down, final_norm_g):
    nb, n_lat, d = x.shape
    n_ctx = ctx.shape[1]
    depth = ada_w.shape[0]
    assert n_ctx == ROW_TILE and n_lat % ROW_TILE == 0 and n_lat % GRID_W == 0
    n_lat_tiles = n_lat // ROW_TILE
    n_all_tiles = n_lat_tiles + 1

    gw = rwkv_k_k.shape[1]
    cols_a = 3 * gw + rwkv_w_up.shape[2] + rwkv_a_up.shape[2] + rwkv_g_up.shape[1]
    cols_b = 3 * gw
    gqa_dh = gqa_q_norm_g.shape[1]
    mla_rope = 64
    mla_heads = 8
    gqa_heads = gw // gqa_dh
    cols_d = mla_q_norm_g.shape[1] + mla_kv_norm_g.shape[1] + mla_rope
    cols_c = w_in.shape[2] - cols_a - cols_b - cols_d
    gqa_kv = (cols_c - gw) // (2 * gqa_dh)
    offs = (0, cols_a, cols_a + cols_b, cols_a + cols_b + cols_c, w_in.shape[2])

    rope_diff = _lane_tables(_rope_tables(n_lat, n_ctx, diff_subln_g.shape[1] // 2))
    rope_gqa = _lane_tables(_rope_tables(n_lat, n_ctx, gqa_dh))
    rope_mla = _lane_tables(_rope_tables(n_lat, n_ctx, mla_rope))

    cvec = jnp.concatenate([c, c_ctx[None, :], jnp.zeros((8 - nb - 1, d), F32)], axis=0)
    mods = _ada(cvec, ada_w, ada_b)[:, :nb + 1].reshape(depth, nb + 1, 6, 1, d)

    h = jnp.concatenate([x, ctx], axis=1)
    s_len = n_lat + n_ctx
    flat = lambda z: z.reshape(nb * s_len, z.shape[-1])
    for l in range(depth):
        need_ctx = l < depth - 1
        m = [mods[l, :, i] for i in range(6)]

        a = flat(_norm_mod(h, norm_mix_g[l], m[1], m[0], n_all_tiles, n_lat_tiles))
        groups = [_matmul(a, _pad_cols(w_in[l][:, offs[i]:offs[i + 1]], 768).astype(BF16), F32, 1536)
                  .reshape(nb, s_len, -1) for i in range(4)]
        oa = _rwkv_mixer(groups[0], n_lat, rwkv_mu[l], rwkv_w0[l], rwkv_w_up[l],
                         rwkv_a0[l], rwkv_a_up[l], rwkv_g_up[l], rwkv_k_k[l], rwkv_k_a[l], rwkv_r_k[l],
                         rwkv_lnx_w[l], rwkv_lnx_b[l])
        lambda_init = 0.8 - 0.6 * math.exp(-0.3 * l)
        ob = _diff_mixer(groups[1], n_lat, diff_lq1[l], diff_lk1[l], diff_lq2[l], diff_lk2[l],
                         diff_subln_g[l], lambda_init, rope_diff)
        oc = _gqa_mixer(groups[2], n_lat, gqa_q_norm_g[l], gqa_k_norm_g[l], gqa_heads, gqa_kv, rope_gqa)
        od = _mla_mixer(groups[3], n_lat, mla_q_norm_g[l], mla_q_up[l], mla_kv_norm_g[l], mla_kv_up[l],
                        mla_heads, mla_rope, rope_mla)
        h = _matmul_resid([flat(o) for o in (oa, ob, oc, od)], w_out[l].astype(BF16), flat(h), m[2],
                          n_all_tiles, tn_pref=1024).reshape(nb, s_len, d)

        j = l // 2
        if l % 2 == 0:
            d_ff = ffn_w_gate.shape[2]
            pad_f = -d_ff % 512
            w_g = jnp.pad(ffn_w_gate[j], ((0, 0), (0, pad_f))).astype(BF16)
            w_u = jnp.pad(ffn_w_up[j], ((0, 0), (0, pad_f))).astype(BF16)
            w_d = jnp.pad(ffn_w_down[j], ((0, pad_f), (0, 0))).astype(BF16)
            a = flat(_norm_mod(h, norm_ffn_g[l], m[4], m[3], n_all_tiles, n_lat_tiles))
            hid = _matmul_swiglu(a, w_g, w_u)
            h = _matmul_resid([hid], w_d, flat(h), m[5], n_all_tiles, tm=ROW_TILE).reshape(nb, s_len, d)
        else:
            assert not need_ctx
            a, logits = _norm_router(h, norm_ffn_g[l], m[4], m[3], moe_router[j], n_lat_tiles, n_lat_tiles)
            h = _moe_layer(a, logits, h, m[5][:nb], moe_w_gate[j], moe_w_up[j], moe_w_down[j])
    return _final_norm(h, final_norm_g, n_lat_tiles)
```

```python
import functools
import itertools
import math

import jax
import jax.numpy as jnp
from jax import lax
from jax.experimental import pallas as pl
from jax.experimental.pallas import tpu as pltpu

F32 = jnp.float32
BF16 = jnp.bfloat16

GRID_W = 64
ROPE_THETA = 10000.0
NORM_EPS = 1e-6
SUBLN_EPS = 1e-5
GN_EPS = 64e-5
TOP_K = 2

LANES = 128
MXU_WIDTH = 256
ROW_TILE = 256
RWKV_HEAD = 64
RWKV_CHUNK = 64
RWKV_QUAD = 4
RWKV_STEP_QUADS = 4
MOE_ROWS = 512
MM_ROWS = 512
ATTN_KEYS = 1408
MLA_KEYS = 768
ATTN_ROWS = 1024
LOG2E = 1.4426950408889634
VMEM_LIMIT = 56 * 1024 * 1024


def _cparams(sem):
    return pltpu.CompilerParams(dimension_semantics=sem, vmem_limit_bytes=VMEM_LIMIT)


def _pick(n, pref):
    if n <= pref:
        return n
    t = pref - pref % LANES
    while t >= LANES:
        if n % t == 0:
            return t
        t -= LANES
    return n


def _ada_kernel(c_ref, w_ref, b_ref, o_ref):
    c = c_ref[...]
    s = (c * jax.nn.sigmoid(c)).astype(BF16)
    o_ref[...] = jnp.dot(s, w_ref[...].astype(BF16), preferred_element_type=F32) + b_ref[...]


def _ada(cvec, ada_w, ada_b):
    n_layers, d, n = ada_w.shape
    tn = _pick(n, 512)
    return pl.pallas_call(
        _ada_kernel,
        grid=(n_layers, n // tn),
        in_specs=[
            pl.BlockSpec((8, d), lambda l, j: (0, 0)),
            pl.BlockSpec((None, d, tn), lambda l, j: (l, 0, j)),
            pl.BlockSpec((None, 1, tn), lambda l, j: (l, 0, j)),
        ],
        out_specs=pl.BlockSpec((None, 8, tn), lambda l, j: (l, 0, j)),
        out_shape=jax.ShapeDtypeStruct((n_layers, 8, n), F32),
        compiler_params=_cparams(("parallel", "parallel")),
        name="ada",
    )(cvec, ada_w, ada_b.reshape(n_layers, 1, n))


def _mod_sel(n_lat_tiles, n_batch):
    return lambda b, i: (jnp.where(i >= n_lat_tiles, n_batch, b), 0, 0)


def _norm_mod_kernel(x_ref, g_ref, sc_ref, sh_ref, o_ref):
    x = x_ref[...]
    y = x * lax.rsqrt(jnp.mean(x * x, axis=-1, keepdims=True) + NORM_EPS) * g_ref[...]
    o_ref[...] = (y * (1.0 + sc_ref[...]) + sh_ref[...]).astype(o_ref.dtype)


def _norm_mod(x, g, scale, shift, n_tiles, n_lat_tiles, out_dtype=BF16):
    nb, _, d = x.shape
    sel = _mod_sel(n_lat_tiles, nb)
    return pl.pallas_call(
        _norm_mod_kernel,
        grid=(nb, n_tiles),
        in_specs=[
            pl.BlockSpec((None, ROW_TILE, d), lambda b, i: (b, i, 0)),
            pl.BlockSpec((1, d), lambda b, i: (0, 0)),
            pl.BlockSpec((None, 1, d), sel),
            pl.BlockSpec((None, 1, d), sel),
        ],
        out_specs=pl.BlockSpec((None, ROW_TILE, d), lambda b, i: (b, i, 0)),
        out_shape=jax.ShapeDtypeStruct((nb, n_tiles * ROW_TILE, d), out_dtype),
        compiler_params=_cparams(("parallel", "parallel")),
        name="norm_mod",
    )(x, g.reshape(1, d), scale, shift)


def _norm_router_kernel(x_ref, g_ref, sc_ref, sh_ref, rh_ref, rl_ref, o_ref, lg_ref):
    x = x_ref[...]
    y = x * lax.rsqrt(jnp.mean(x * x, axis=-1, keepdims=True) + NORM_EPS) * g_ref[...]
    a = y * (1.0 + sc_ref[...]) + sh_ref[...]
    o_ref[...] = a
    ah = a.astype(BF16)
    al = (a - ah.astype(F32)).astype(BF16)
    rh = rh_ref[...]
    lg_ref[...] = (jnp.dot(ah, rh, preferred_element_type=F32)
                   + jnp.dot(al, rh, preferred_element_type=F32)
                   + jnp.dot(ah, rl_ref[...], preferred_element_type=F32))


def _norm_router(x, g, scale, shift, router, n_tiles, n_lat_tiles):
    nb, _, d = x.shape
    n_exp = router.shape[1]
    r_pad = jnp.pad(router, ((0, 0), (0, LANES - n_exp)))
    r_hi = r_pad.astype(BF16)
    r_lo = (r_pad - r_hi.astype(F32)).astype(BF16)
    sel = _mod_sel(n_lat_tiles, nb)
    rows = n_tiles * ROW_TILE
    return pl.pallas_call(
        _norm_router_kernel,
        grid=(nb, n_tiles),
        in_specs=[
            pl.BlockSpec((None, ROW_TILE, d), lambda b, i: (b, i, 0)),
            pl.BlockSpec((1, d), lambda b, i: (0, 0)),
            pl.BlockSpec((None, 1, d), sel),
            pl.BlockSpec((None, 1, d), sel),
            pl.BlockSpec((d, LANES), lambda b, i: (0, 0)),
            pl.BlockSpec((d, LANES), lambda b, i: (0, 0)),
        ],
        out_specs=[
            pl.BlockSpec((None, ROW_TILE, d), lambda b, i: (b, i, 0)),
            pl.BlockSpec((None, ROW_TILE, LANES), lambda b, i: (b, i, 0)),
        ],
        out_shape=[
            jax.ShapeDtypeStruct((nb, rows, d), F32),
            jax.ShapeDtypeStruct((nb, rows, LANES), F32),
        ],
        compiler_params=_cparams(("parallel", "parallel")),
        name="norm_router",
    )(x, g.reshape(1, d), scale, shift, r_hi, r_lo)


def _final_norm_kernel(x_ref, g_ref, o_ref):
    x = x_ref[...]
    o_ref[...] = x * lax.rsqrt(jnp.mean(x * x, axis=-1, keepdims=True) + NORM_EPS) * g_ref[...]


def _final_norm(x, g, n_tiles):
    nb, _, d = x.shape
    rows = n_tiles * ROW_TILE
    return pl.pallas_call(
        _final_norm_kernel,
        grid=(nb, n_tiles),
        in_specs=[
            pl.BlockSpec((None, ROW_TILE, d), lambda b, i: (b, i, 0)),
            pl.BlockSpec((1, d), lambda b, i: (0, 0)),
        ],
        out_specs=pl.BlockSpec((None, ROW_TILE, d), lambda b, i: (b, i, 0)),
        out_shape=jax.ShapeDtypeStruct((nb, rows, d), F32),
        compiler_params=_cparams(("parallel", "parallel")),
        name="final_norm",
    )(x, g.reshape(1, d))


def _row_tile(rows):
    return MM_ROWS if rows % MM_ROWS == 0 else ROW_TILE


def _mm_kernel(a_ref, w_ref, o_ref):
    o_ref[...] = jnp.dot(a_ref[...], w_ref[...], preferred_element_type=F32).astype(o_ref.dtype)


def _matmul(a, w, out_dtype=F32, tn_pref=1024):
    rows, k = a.shape
    n = w.shape[1]
    tn = _pick(n, tn_pref)
    tm = _row_tile(rows)
    return pl.pallas_call(
        _mm_kernel,
        grid=(n // tn, rows // tm),
        in_specs=[
            pl.BlockSpec((tm, k), lambda j, i: (i, 0)),
            pl.BlockSpec((k, tn), lambda j, i: (0, j)),
        ],
        out_specs=pl.BlockSpec((tm, tn), lambda j, i: (i, j)),
        out_shape=jax.ShapeDtypeStruct((rows, n), out_dtype),
        compiler_params=_cparams(("parallel", "parallel")),
        name="matmul",
    )(a, w)


def _mm_swiglu_kernel(a_ref, wg_ref, wu_ref, o_ref):
    a = a_ref[...]
    g = jnp.dot(a, wg_ref[...], preferred_element_type=F32)
    u = jnp.dot(a, wu_ref[...], preferred_element_type=F32)
    o_ref[...] = (g * jax.nn.sigmoid(g) * u).astype(o_ref.dtype)


def _matmul_swiglu(a, wg, wu, tn_pref=1024):
    rows, k = a.shape
    n = wg.shape[1]
    tn = _pick(n, tn_pref)
    tm = _row_tile(rows)
    return pl.pallas_call(
        _mm_swiglu_kernel,
        grid=(n // tn, rows // tm),
        in_specs=[
            pl.BlockSpec((tm, k), lambda j, i: (i, 0)),
            pl.BlockSpec((k, tn), lambda j, i: (0, j)),
            pl.BlockSpec((k, tn), lambda j, i: (0, j)),
        ],
        out_specs=pl.BlockSpec((tm, tn), lambda j, i: (i, j)),
        out_shape=jax.ShapeDtypeStruct((rows, n), BF16),
        compiler_params=_cparams(("parallel", "parallel")),
        name="matmul_swiglu",
    )(a, wg, wu)


def _mm_resid_kernel(*refs, n_parts, n_gates):
    a_refs, w_refs = refs[:n_parts], refs[n_parts:2 * n_parts]
    res_ref = refs[2 * n_parts]
    gate_refs = refs[2 * n_parts + 1:2 * n_parts + 1 + n_gates]
    o_ref = refs[-1]
    acc = jnp.dot(a_refs[0][...], w_refs[0][...], preferred_element_type=F32)
    for a_ref, w_ref in zip(a_refs[1:], w_refs[1:]):
        acc = acc + jnp.dot(a_ref[...], w_ref[...], preferred_element_type=F32)
    rows = acc.shape[0] // n_gates
    for t, gate_ref in enumerate(gate_refs):
        sl = slice(t * rows, (t + 1) * rows)
        o_ref[sl, :] = res_ref[sl, :] + gate_ref[...] * acc[sl, :]


def _matmul_resid(parts, w, res, gate, tiles_per_batch, tm=None, tn_pref=512):
    rows, kg = parts[0].shape
    n_parts = len(parts)
    n = w.shape[1]
    nb = gate.shape[0] - 1
    tn = _pick(n, tn_pref)
    tm = tm or _row_tile(rows)
    n_gates = tm // ROW_TILE

    def gate_spec(t):
        def index(j, i):
            tile = i * n_gates + t
            pos = tile % tiles_per_batch
            return (jnp.where(pos == tiles_per_batch - 1, nb, tile // tiles_per_batch), 0, j)
        return pl.BlockSpec((None, 1, tn), index)

    return pl.pallas_call(
        functools.partial(_mm_resid_kernel, n_parts=n_parts, n_gates=n_gates),
        grid=(n // tn, rows // tm),
        in_specs=([pl.BlockSpec((tm, kg), lambda j, i: (i, 0))] * n_parts
                  + [pl.BlockSpec((kg, tn), lambda j, i, g=g: (g, j)) for g in range(n_parts)]
                  + [pl.BlockSpec((tm, tn), lambda j, i: (i, j))]
                  + [gate_spec(t) for t in range(n_gates)]),
        out_specs=pl.BlockSpec((tm, tn), lambda j, i: (i, j)),
        out_shape=jax.ShapeDtypeStruct((rows, n), F32),
        compiler_params=_cparams(("parallel", "parallel")),
        name="matmul_resid",
    )(*parts, *([w] * n_parts), res, *([gate] * n_gates))


_NT = (((1,), (1,)), ((), ()))


def _online_softmax(qs, k_ref, v_ref, ck):
    n_rows = qs.shape[0]
    m = jnp.full((n_rows, 1), -jnp.inf, F32)
    acc = jnp.zeros((n_rows, v_ref.shape[1]), F32)
    for c in range(k_ref.shape[0] // ck):
        keys = slice(c * ck, (c + 1) * ck)
        s = lax.dot_general(qs, k_ref[keys, :], _NT, preferred_element_type=F32)
        m_new = jnp.maximum(m, jnp.max(s, axis=-1, keepdims=True))
        p = jnp.exp2(s - m_new).astype(BF16)
        acc = jnp.exp2(m - m_new) * acc + jnp.dot(p, v_ref[keys, :], preferred_element_type=F32)
        m = m_new
    return acc


def _attend_rows(body, pre_args, pre_specs, q, k, v1, n_heads, qw, dk, ow, tq, n_lat, name, keys=ATTN_KEYS):
    nb, s_len, _ = q.shape
    n_ctx = s_len - n_lat
    dvp = v1.shape[2] // n_heads
    out_shape = jax.ShapeDtypeStruct((nb, s_len, n_heads * ow), BF16)

    def run(tq, n_q, q_off, k_rows, k_off, prev):
        kern = functools.partial(body, ck=_pick(k_rows, keys))
        specs = pre_specs + [
            pl.BlockSpec((None, tq, qw), lambda b, h, i: (b, i + q_off, h)),
            pl.BlockSpec((None, k_rows, dk), lambda b, h, i: (b, k_off, h)),
            pl.BlockSpec((None, k_rows, dvp), lambda b, h, i: (b, k_off, h)),
        ]
        args = pre_args + [q, k, v1, prev]
        return pl.pallas_call(
            lambda *refs: kern(*refs[:-2], refs[-1]),
            grid=(nb, n_heads, n_q),
            in_specs=specs + [pl.BlockSpec(memory_space=pl.ANY)],
            out_specs=pl.BlockSpec((None, tq, ow), lambda b, h, i: (b, i + q_off, h)),
            out_shape=out_shape,
            input_output_aliases={len(args) - 1: 0},
            compiler_params=_cparams(("parallel", "parallel", "parallel")),
            name=name,
        )(*args)

    tq = math.gcd(tq, n_lat)
    out = run(tq, n_lat // tq, 0, s_len, 0, jnp.zeros(out_shape.shape, out_shape.dtype))
    return run(n_ctx, 1, n_lat // n_ctx, n_ctx, n_lat // n_ctx, out)


def _attn_kernel(q_ref, k_ref, v_ref, o_ref, *, pieces, dv, ck):
    q = q_ref[...]
    tq, dk = q.shape[0], q.shape[1] // pieces
    qs = q if pieces == 1 else jnp.concatenate([q[:, g * dk:(g + 1) * dk] for g in range(pieces)], axis=0)
    acc = _online_softmax(qs, k_ref, v_ref, ck)
    o = acc[:, :dv] / acc[:, dv:dv + 1]
    for g in range(pieces):
        o_ref[:, g * dv:(g + 1) * dv] = o[g * tq:(g + 1) * tq].astype(o_ref.dtype)


def _attention(q, k, v1, n_kv_heads, group, dk, dv, tq, n_lat, keys=ATTN_KEYS):
    return _attend_rows(functools.partial(_attn_kernel, pieces=group, dv=dv), [], [], q, k, v1, n_kv_heads,
                        group * dk, dk, group * dv, tq, n_lat, "attention", keys)


def _diff_attn_kernel(lam_ref, g_ref, q_ref, k_ref, v_ref, o_ref, *, dv, ck, out_scale):
    q = q_ref[...]
    tq, width = q.shape
    first = lax.broadcasted_iota(jnp.int32, (tq, width), 1) < width // 2
    zero = jnp.zeros_like(q)
    qs = jnp.concatenate([jnp.where(first, q, zero), jnp.where(first, zero, q)], axis=0)
    acc = _online_softmax(qs, k_ref, v_ref, ck)
    o = acc[:, :dv] / acc[:, dv:dv + 1]
    o = o[:tq] - lam_ref[0] * o[tq:]
    o = o * lax.rsqrt(jnp.mean(o * o, axis=-1, keepdims=True) + SUBLN_EPS) * g_ref[...]
    o_ref[...] = (o * out_scale).astype(o_ref.dtype)


def _diff_attention(q, k, v1, lam, subln_g, n_heads, out_scale, tq, n_lat):
    dh = q.shape[2] // n_heads
    return _attend_rows(functools.partial(_diff_attn_kernel, dv=dh, out_scale=out_scale),
                        [lam.reshape(1).astype(F32), subln_g.reshape(1, dh)],
                        [pl.BlockSpec(memory_space=pltpu.SMEM), pl.BlockSpec((1, dh), lambda b, h, i: (0, 0))],
                        q, k, v1, n_heads, dh, dh, dh, tq, n_lat, "diff_attention")


def _rwkv_chunk(r, v, kk, wl, kd, bb, h_state, rev, out):
    c, w = r.shape
    nh = w // RWKV_HEAD
    assert c == RWKV_HEAD
    tr = lax.broadcasted_iota(jnp.int32, (c, w), 0)
    cum = wl
    step = 1
    while step < c:
        shifted = pltpu.roll(cum, c - step if rev else step, axis=0)
        cum = cum + jnp.where((tr < c - step) if rev else (tr >= step), shifted, 0.0)
        step *= 2
    cum_prev = cum - wl
    last = 0 if rev else c - 1
    total = cum[last:last + 1, :]
    mid = cum[c // 2:c // 2 + 1, :]
    a = -kk
    e_mc = jnp.exp(mid - cum)
    e_end = jnp.exp(total - cum)

    lane = lax.broadcasted_iota(jnp.int32, (c, w), 1)
    lane_head, ts = lane // c, lane % c
    strict = (ts > tr) if rev else (ts < tr)
    incl = (ts >= tr) if rev else (ts <= tr)

    def stack(x):
        return jnp.concatenate([jnp.where(lane_head == h, x, 0.0) for h in range(nh)], axis=0).astype(BF16)

    def dot_nt(x, y):
        return lax.dot_general(x.astype(BF16), y.astype(BF16), _NT, preferred_element_type=F32)

    def dot_tn(x, y):
        return lax.dot_general(x.astype(BF16), y.astype(BF16), (((0,), (0,)), ((), ())), preferred_element_type=F32)

    def dot(x, y):
        return jnp.dot(x.astype(BF16), y.astype(BF16), preferred_element_type=F32)

    lhs = jnp.concatenate([a * jnp.exp(cum_prev - mid), r * jnp.exp(cum - mid)], axis=0)
    g_b = dot_nt(lhs, stack(bb * e_mc))
    g_k = dot_nt(lhs, stack(kd * e_mc))
    n_ab = jnp.where(strict, g_b[:c], 0.0)
    a_rb = jnp.where(incl, g_b[c:], 0.0)
    a_ak = jnp.where(strict, g_k[:c], 0.0)
    a_rk = jnp.where(incl, g_k[c:], 0.0)
    yield

    t_inv = jnp.where(ts == tr, 1.0, 0.0) + n_ab
    pw = n_ab
    for _ in range(int(math.log2(c)) - 1):
        pw = dot(pw, stack(pw))
        yield
        t_inv = t_inv + dot(pw, stack(t_inv))
    yield

    v_st = stack(v)
    x = dot(a * jnp.exp(cum_prev), h_state) + dot(a_ak, v_st)
    yield
    u = dot(t_inv, stack(x))
    yield
    y = dot(r * jnp.exp(cum), h_state) + dot(a_rb, stack(u)) + dot(a_rk, v_st)
    rw = lax.broadcasted_iota(jnp.int32, (w, w), 0)
    cw = lax.broadcasted_iota(jnp.int32, (w, w), 1)
    decay_end = jnp.where(rw == cw, jnp.broadcast_to(jnp.exp(total), (w, w)), 0.0)
    update = dot_tn(jnp.concatenate([bb * e_end, kd * e_end], axis=0), jnp.concatenate([u, v], axis=0))
    out.append((y, dot(decay_end, h_state) + jnp.where(rw // RWKV_HEAD == cw // RWKV_HEAD, update, 0.0)))


def _rwkv_scan_kernel(rf_ref, vf_ref, kkf_ref, wlf_ref, kdf_ref, bbf_ref,
                      rr_ref, vr_ref, kkr_ref, wlr_ref, kdr_ref, bbr_ref,
                      yf_ref, yr_ref, hf_ref, hr_ref):
    @pl.when(pl.program_id(2) == 0)
    def _():
        hf_ref[...] = jnp.zeros_like(hf_ref)
        hr_ref[...] = jnp.zeros_like(hr_ref)

    qw = RWKV_QUAD * RWKV_HEAD
    chains = []
    for i in range(hf_ref.shape[0]):
        sl = slice(i * qw, (i + 1) * qw)
        for refs, y_ref, h_ref, rev in (((rf_ref, vf_ref, kkf_ref, wlf_ref, kdf_ref, bbf_ref), yf_ref, hf_ref, False),
                                        ((rr_ref, vr_ref, kkr_ref, wlr_ref, kdr_ref, bbr_ref), yr_ref, hr_ref, True)):
            out = []
            chains.append((_rwkv_chunk(*(ref[:, sl] for ref in refs), h_ref[i], rev, out), out, y_ref, h_ref, i, sl))
    for _ in itertools.zip_longest(*(chain[0] for chain in chains)):
        pass
    for _, out, y_ref, h_ref, i, sl in chains:
        y_ref[:, sl], h_ref[i] = out[0]


def _rwkv_scan(r, v, kk, wl, kd, bb, n_lat):
    nb, s_len, w = r.shape
    c = RWKV_CHUNK
    qw = RWKV_QUAD * RWKV_HEAD
    nq = RWKV_STEP_QUADS if w % (RWKV_STEP_QUADS * qw) == 0 else 1
    bw = nq * qw
    n_l, n_all = n_lat // c, s_len // c
    n_c = n_all - n_l

    def fwd(s):
        return jnp.where(s < n_c, n_l + s, s - n_c)

    def rev(s):
        return n_all - 1 - s

    def shared(order):
        return pl.BlockSpec((None, c, bw), lambda b, q, s: (b, order(s), q))

    def per_dir(d, order):
        return pl.BlockSpec((None, None, c, bw), lambda b, q, s: (d, b, order(s), q))

    yf, yr = pl.pallas_call(
        _rwkv_scan_kernel,
        grid=(nb, w // bw, n_all),
        in_specs=[shared(fwd), shared(fwd), shared(fwd), per_dir(0, fwd), per_dir(0, fwd), per_dir(0, fwd),
                  shared(rev), shared(rev), shared(rev), per_dir(1, rev), per_dir(1, rev), per_dir(1, rev)],
        out_specs=[shared(fwd), shared(rev)],
        out_shape=[jax.ShapeDtypeStruct((nb, s_len, w), F32)] * 2,
        scratch_shapes=[pltpu.VMEM((nq, qw, qw), F32), pltpu.VMEM((nq, qw, qw), F32)],
        compiler_params=_cparams(("parallel", "parallel", "arbitrary")),
        name="rwkv_scan",
    )(r, v, kk, wl, kd, bb, r, v, kk, wl, kd, bb)
    return yf, yr


def _group_sum(x, ones_bd):
    hi = x.astype(BF16)
    lo = (x - hi.astype(F32)).astype(BF16)
    wb = ones_bd.shape[0]
    return jnp.concatenate(
        [jnp.dot(hi[:, b:b + wb], ones_bd, preferred_element_type=F32)
         + jnp.dot(lo[:, b:b + wb], ones_bd, preferred_element_type=F32) for b in range(0, x.shape[1], wb)], axis=1)


def _rwkv_prep_kernel(x_ref, prev_ref, next_ref, mu_ref, kkg_ref, ka_ref, rk_ref, w0_ref, a0_ref, wup_ref, aup_ref,
                      gup_ref, ones_ref, r_ref, v_ref, kk_ref, wl_ref, kd_ref, bb_ref, bonus_ref, g_ref,
                      *, n_lat_tiles, gw, lora_w, lora_a):
    i = pl.program_id(1)
    x = x_ref[...]
    rows = x.shape[0]
    row = lax.broadcasted_iota(jnp.int32, x.shape, 0)
    has_prev = jnp.logical_and(i > 0, i < n_lat_tiles)
    has_next = i < n_lat_tiles - 1
    prev_row = jnp.where(has_prev, prev_ref[7:8, :], 0.0)
    next_row = jnp.where(has_next, next_ref[0:1, :], 0.0)
    prev = jnp.where(row == 0, prev_row, pltpu.roll(x, 1, axis=0))
    nxt = jnp.where(row == rows - 1, next_row, pltpu.roll(x, rows - 1, axis=0))
    ps = x + mu_ref[...] * (0.5 * (prev + nxt) - x)
    r, k, v = ps[:, :gw], ps[:, gw:2 * gw], ps[:, 2 * gw:3 * gw]
    o = 3 * gw
    wd, ad, gd = ps[:, o:o + lora_w], ps[:, o + lora_w:o + lora_w + lora_a], ps[:, o + lora_w + lora_a:]
    ones_bd = ones_ref[...]
    kk = k * kkg_ref[...]
    kk = kk * lax.rsqrt(jnp.maximum(_group_sum(kk * kk, ones_bd), 1e-24))
    tanh_wd = jnp.tanh(wd).astype(BF16)
    ad_b = ad.astype(BF16)
    kd_sum = jnp.zeros_like(k)
    for d in range(2):
        z = w0_ref[d] + jnp.dot(tanh_wd, wup_ref[d], preferred_element_type=F32)
        softplus_neg = jnp.maximum(-z, 0.0) + jnp.log1p(jnp.exp(-jnp.abs(z)))
        wl_ref[d] = -jnp.exp(-softplus_neg - 0.5)
        a_sig = jax.nn.sigmoid(a0_ref[d] + jnp.dot(ad_b, aup_ref[d], preferred_element_type=F32))
        kd = k * (1.0 + (a_sig - 1.0) * ka_ref[...])
        kd_ref[d] = kd
        bb_ref[d] = kk * a_sig
        kd_sum = kd_sum + kd
    r_ref[...] = r
    v_ref[...] = v
    kk_ref[...] = kk
    bonus_ref[...] = _group_sum(r * kd_sum * rk_ref[...], ones_bd) * v
    g_ref[...] = jnp.dot(jax.nn.sigmoid(gd).astype(BF16), gup_ref[...], preferred_element_type=F32)


def _head_ones(width, head):
    idx = jnp.arange(width) // head
    return (idx[:, None] == idx[None, :]).astype(BF16)


def _rwkv_prep(pa, n_lat_tiles, mu, w0, w_up, a0, a_up, g_up, k_k, k_a, r_k):
    nb, s_len, wp = pa.shape
    gw = k_k.shape[0]
    lora_w, lora_a = w_up.shape[1], a_up.shape[1]
    lora_g = wp - 3 * gw - lora_w - lora_a
    n_tiles = s_len // ROW_TILE
    sub = ROW_TILE // 8
    row = lambda z: z.reshape(1, -1)
    full = lambda shape: pl.BlockSpec(shape, lambda b, i: (0,) * len(shape))
    tile = lambda w: pl.BlockSpec((None, ROW_TILE, w), lambda b, i: (b, i, 0))
    tile2 = pl.BlockSpec((2, None, ROW_TILE, gw), lambda b, i: (0, b, i, 0))
    one = jax.ShapeDtypeStruct((nb, s_len, gw), F32)
    two = jax.ShapeDtypeStruct((2, nb, s_len, gw), F32)
    return pl.pallas_call(
        functools.partial(_rwkv_prep_kernel, n_lat_tiles=n_lat_tiles, gw=gw, lora_w=lora_w, lora_a=lora_a),
        grid=(nb, n_tiles),
        in_specs=[
            tile(wp),
            pl.BlockSpec((None, 8, wp), lambda b, i: (b, jnp.maximum(i * sub - 1, 0), 0)),
            pl.BlockSpec((None, 8, wp), lambda b, i: (b, jnp.minimum((i + 1) * sub, s_len // 8 - 1), 0)),
            full((1, wp)), full((1, gw)), full((1, gw)), full((1, gw)),
            full((2, 1, gw)), full((2, 1, gw)),
            full((2, lora_w, gw)), full((2, lora_a, gw)), full((lora_g, gw)), full((MXU_WIDTH, MXU_WIDTH)),
        ],
        out_specs=[tile(gw), tile(gw), tile(gw), tile2, tile2, tile2, tile(gw), tile(gw)],
        out_shape=[one, one, one, two, two, two, one, one],
        compiler_params=_cparams(("parallel", "parallel")),
        name="rwkv_prep",
    )(pa, pa, pa, row(jnp.pad(mu, (0, wp - mu.shape[0]))), row(k_k), row(k_a), row(r_k),
      w0.reshape(2, 1, gw), a0.reshape(2, 1, gw), w_up.astype(BF16), a_up.astype(BF16),
      jnp.pad(g_up, ((0, lora_g - g_up.shape[0]), (0, 0))).astype(BF16), _head_ones(MXU_WIDTH, RWKV_HEAD))


def _rwkv_out_kernel(yf_ref, yr_ref, bonus_ref, g_ref, lw_ref, lb_ref, ones_ref, o_ref):
    ones_bd = ones_ref[...]
    y = yf_ref[...] + yr_ref[...]
    inv_n = 1.0 / RWKV_HEAD
    dev = y - _group_sum(y, ones_bd) * inv_n
    var = _group_sum(dev * dev, ones_bd) * inv_n
    yn = dev * lax.rsqrt(var + GN_EPS) * lw_ref[...] + lb_ref[...]
    o_ref[...] = ((yn + bonus_ref[...]) * g_ref[...]).astype(o_ref.dtype)


def _rwkv_out(yf, yr, bonus, g, lnx_w, lnx_b):
    nb, s_len, gw = yf.shape
    tile = pl.BlockSpec((None, ROW_TILE, gw), lambda b, i: (b, i, 0))
    vec = pl.BlockSpec((1, gw), lambda b, i: (0, 0))
    return pl.pallas_call(
        _rwkv_out_kernel,
        grid=(nb, s_len // ROW_TILE),
        in_specs=[tile, tile, tile, tile, vec, vec, pl.BlockSpec((MXU_WIDTH, MXU_WIDTH), lambda b, i: (0, 0))],
        out_specs=tile,
        out_shape=jax.ShapeDtypeStruct((nb, s_len, gw), BF16),
        compiler_params=_cparams(("parallel", "parallel")),
        name="rwkv_out",
    )(yf, yr, bonus, g, lnx_w.reshape(1, gw), lnx_b.reshape(1, gw), _head_ones(MXU_WIDTH, RWKV_HEAD))


def _rope_block(xb, cos_t, sin_t, half):
    if 2 * half == LANES:
        partner = pltpu.roll(xb, half, axis=1)
    else:
        lane = lax.broadcasted_iota(jnp.int32, xb.shape, 1)
        partner = jnp.where(lane % (2 * half) < half, pltpu.roll(xb, LANES - half, axis=1),
                            pltpu.roll(xb, half, axis=1))
    return xb * cos_t + partner * sin_t


def _lane_tables(cs):
    cos, sin = cs
    reps = LANES // (2 * cos.shape[1])
    return (jnp.tile(jnp.concatenate([cos, cos], axis=1), (1, reps)),
            jnp.tile(jnp.concatenate([-sin, sin], axis=1), (1, reps)))


def _ones_column(rows, dtype):
    return jnp.where(lax.broadcasted_iota(jnp.int32, (rows, LANES), 1) == 0, 1.0, 0.0).astype(dtype)


def _blk(b):
    return slice(b * LANES, (b + 1) * LANES)


def _diff_prep_kernel(x_ref, cos_ref, sin_ref, q_ref, k_ref, v_ref, *, gw, half, q_scale):
    cos_t, sin_t = cos_ref[...], sin_ref[...]
    e0 = _ones_column(x_ref.shape[0], v_ref.dtype)
    nblk = gw // LANES
    for b in range(nblk):
        q_ref[:, _blk(b)] = (_rope_block(x_ref[:, _blk(b)], cos_t, sin_t, half) * q_scale).astype(q_ref.dtype)
        k_ref[:, _blk(b)] = _rope_block(x_ref[:, _blk(nblk + b)], cos_t, sin_t, half).astype(k_ref.dtype)
        v_ref[:, _blk(2 * b)] = x_ref[:, _blk(2 * nblk + b)].astype(v_ref.dtype)
        v_ref[:, _blk(2 * b + 1)] = e0


def _diff_prep(pb, tables, half, q_scale):
    nb, s_len, w3 = pb.shape
    gw = w3 // 3
    tile = lambda w: pl.BlockSpec((None, ROW_TILE, w), lambda b, i: (b, i, 0))
    tab = pl.BlockSpec((ROW_TILE, LANES), lambda b, i: (i, 0))
    return pl.pallas_call(
        functools.partial(_diff_prep_kernel, gw=gw, half=half, q_scale=q_scale),
        grid=(nb, s_len // ROW_TILE),
        in_specs=[tile(w3), tab, tab],
        out_specs=[tile(gw), tile(gw), tile(2 * gw)],
        out_shape=[jax.ShapeDtypeStruct((nb, s_len, gw), BF16), jax.ShapeDtypeStruct((nb, s_len, gw), BF16),
                   jax.ShapeDtypeStruct((nb, s_len, 2 * gw), BF16)],
        compiler_params=_cparams(("parallel", "parallel")),
        name="diff_prep",
    )(pb, *tables)


def _gqa_prep_kernel(x_ref, cos_ref, sin_ref, qg_ref, kg_ref, q_ref, k_ref, v_ref, *, n_heads, n_kv, q_scale):
    cos_t, sin_t = cos_ref[...], sin_ref[...]
    e0 = _ones_column(x_ref.shape[0], v_ref.dtype)

    def normed_rope(xb, g):
        xb = xb * lax.rsqrt(jnp.mean(xb * xb, axis=-1, keepdims=True) + NORM_EPS) * g
        return _rope_block(xb, cos_t, sin_t, LANES // 2)

    for b in range(n_heads):
        q_ref[:, _blk(b)] = (normed_rope(x_ref[:, _blk(b)], qg_ref[...]) * q_scale).astype(q_ref.dtype)
    for b in range(n_kv):
        k_ref[:, _blk(b)] = normed_rope(x_ref[:, _blk(n_heads + b)], kg_ref[...]).astype(k_ref.dtype)
        v_ref[:, _blk(2 * b)] = x_ref[:, _blk(n_heads + n_kv + b)].astype(v_ref.dtype)
        v_ref[:, _blk(2 * b + 1)] = e0


def _gqa_prep(pc, tables, q_norm_g, k_norm_g, n_heads, n_kv, q_scale):
    nb, s_len, w = pc.shape
    tile = lambda w_: pl.BlockSpec((None, ROW_TILE, w_), lambda b, i: (b, i, 0))
    tab = pl.BlockSpec((ROW_TILE, LANES), lambda b, i: (i, 0))
    vec = pl.BlockSpec((1, LANES), lambda b, i: (0, 0))
    return pl.pallas_call(
        functools.partial(_gqa_prep_kernel, n_heads=n_heads, n_kv=n_kv, q_scale=q_scale),
        grid=(nb, s_len // ROW_TILE),
        in_specs=[tile(w), tab, tab, vec, vec],
        out_specs=[tile(n_heads * LANES), tile(n_kv * LANES), tile(2 * n_kv * LANES)],
        out_shape=[jax.ShapeDtypeStruct((nb, s_len, n_heads * LANES), BF16),
                   jax.ShapeDtypeStruct((nb, s_len, n_kv * LANES), BF16),
                   jax.ShapeDtypeStruct((nb, s_len, 2 * n_kv * LANES), BF16)],
        compiler_params=_cparams(("parallel", "parallel")),
        name="gqa_prep",
    )(pc, *tables, q_norm_g.reshape(1, LANES), k_norm_g.reshape(1, LANES))


def _mla_prep_kernel(x_ref, cos_ref, sin_ref, qg_ref, kvg_ref, qup_ref, kvup_ref, q_ref, k_ref, v_ref,
                     *, q_lora, kv_lora, n_heads, half, q_scale):
    cos_t, sin_t = cos_ref[...], sin_ref[...]
    e0 = _ones_column(x_ref.shape[0], v_ref.dtype)

    def normed(z, g):
        return (z * lax.rsqrt(jnp.mean(z * z, axis=-1, keepdims=True) + NORM_EPS) * g).astype(BF16)

    q = jnp.dot(normed(x_ref[:, :q_lora], qg_ref[...]), qup_ref[...], preferred_element_type=F32)
    kv = jnp.dot(normed(x_ref[:, q_lora:q_lora + kv_lora], kvg_ref[...]), kvup_ref[...], preferred_element_type=F32)
    k_rope = _rope_block(x_ref[:, q_lora + kv_lora:q_lora + kv_lora + LANES], cos_t, sin_t, half).astype(k_ref.dtype)
    for h in range(n_heads):
        q_ref[:, _blk(2 * h)] = (q[:, _blk(2 * h)] * q_scale).astype(q_ref.dtype)
        q_ref[:, _blk(2 * h + 1)] = (_rope_block(q[:, _blk(2 * h + 1)], cos_t, sin_t, half) * q_scale).astype(q_ref.dtype)
        k_ref[:, _blk(2 * h)] = kv[:, _blk(h)].astype(k_ref.dtype)
        k_ref[:, _blk(2 * h + 1)] = k_rope
        v_ref[:, _blk(2 * h)] = kv[:, _blk(n_heads + h)].astype(v_ref.dtype)
        v_ref[:, _blk(2 * h + 1)] = e0


def _mla_prep(pd, tables, q_norm_g, q_up, kv_norm_g, kv_up, n_heads, d_rope, q_scale):
    nb, s_len, wp = pd.shape
    q_lora, kv_lora = q_norm_g.shape[0], kv_norm_g.shape[0]
    d_qk = q_up.shape[1] // n_heads
    d_nope = d_qk - d_rope
    assert d_nope == LANES and kv_up.shape[1] == n_heads * 2 * LANES and q_lora + kv_lora + LANES <= wp
    qw = jnp.pad(q_up.reshape(q_lora, n_heads, d_qk), ((0, 0), (0, 0), (0, 2 * LANES - d_qk)))
    qw = qw.reshape(q_lora, n_heads * 2 * LANES).astype(BF16)
    kvw = kv_up.reshape(kv_lora, n_heads, 2, LANES).transpose(0, 2, 1, 3).reshape(kv_lora, 2 * n_heads * LANES)
    tile = lambda w_: pl.BlockSpec((None, ROW_TILE, w_), lambda b, i: (b, i, 0))
    tab = pl.BlockSpec((ROW_TILE, LANES), lambda b, i: (i, 0))
    full = lambda shape: pl.BlockSpec(shape, lambda b, i: (0,) * len(shape))
    wide = n_heads * 2 * LANES
    out = jax.ShapeDtypeStruct((nb, s_len, wide), BF16)
    return pl.pallas_call(
        functools.partial(_mla_prep_kernel, q_lora=q_lora, kv_lora=kv_lora, n_heads=n_heads, half=d_rope // 2,
                          q_scale=q_scale),
        grid=(nb, s_len // ROW_TILE),
        in_specs=[tile(wp), tab, tab, full((1, q_lora)), full((1, kv_lora)), full((q_lora, wide)),
                  full((kv_lora, wide))],
        out_specs=[tile(wide), tile(wide), tile(wide)],
        out_shape=[out, out, out],
        compiler_params=_cparams(("parallel", "parallel")),
        name="mla_prep",
    )(pd, *tables, q_norm_g.reshape(1, q_lora), kv_norm_g.reshape(1, kv_lora), qw, kvw.astype(BF16))


def _moe_gather_kernel(src_ref, used_ref, tok_ref, o_ref, buf_ref, sem_ref):
    base = pl.program_id(0) * MOE_ROWS

    def copy(r):
        return pltpu.make_async_copy(tok_ref.at[pl.ds(src_ref[base + r], 1), :], buf_ref.at[pl.ds(r, 1), :],
                                     sem_ref.at[0])

    def start(r, carry):
        copy(r).start()
        return carry

    @pl.when(pl.program_id(0) < used_ref[0])
    def _():
        lax.fori_loop(0, MOE_ROWS, start, 0, unroll=16)
        pltpu.make_async_copy(tok_ref.at[pl.ds(0, MOE_ROWS), :], buf_ref, sem_ref.at[0]).wait()
        o_ref[...] = buf_ref[...].astype(o_ref.dtype)

    @pl.when(pl.program_id(0) >= used_ref[0])
    def _():
        o_ref[...] = jnp.zeros_like(o_ref)


def _used_block(i, used):
    return jnp.minimum(i, used[0] - 1)


def _moe_gather(tok, src, used, n_blocks):
    d = tok.shape[1]
    return pl.pallas_call(
        _moe_gather_kernel,
        grid_spec=pltpu.PrefetchScalarGridSpec(
            num_scalar_prefetch=2,
            grid=(n_blocks,),
            in_specs=[pl.BlockSpec(memory_space=pl.ANY)],
            out_specs=pl.BlockSpec((MOE_ROWS, d), lambda i, src, used: (i, 0)),
            scratch_shapes=[pltpu.VMEM((MOE_ROWS, d), F32), pltpu.SemaphoreType.DMA((1,))],
        ),
        out_shape=jax.ShapeDtypeStruct((n_blocks * MOE_ROWS, d), BF16),
        compiler_params=_cparams(("arbitrary",)),
        name="moe_gather",
    )(src, used, tok)


def _expert_changed(be_ref, i):
    return jnp.logical_or(i == 0, be_ref[i] != be_ref[jnp.maximum(i - 1, 0)])


def _moe_up_kernel(be_ref, used_ref, a_ref, wg_ref, wu_ref, o_ref, wg_bf, wu_bf):
    i = pl.program_id(1)

    @pl.when(i < used_ref[0])
    def _():
        @pl.when(_expert_changed(be_ref, i))
        def _():
            wg_bf[...] = wg_ref[...].astype(BF16)
            wu_bf[...] = wu_ref[...].astype(BF16)

        a = a_ref[...]
        g = jnp.dot(a, wg_bf[...], preferred_element_type=F32)
        u = jnp.dot(a, wu_bf[...], preferred_element_type=F32)
        o_ref[...] = (g * jax.nn.sigmoid(g) * u).astype(o_ref.dtype)

    @pl.when(i >= used_ref[0])
    def _():
        o_ref[...] = jnp.zeros_like(o_ref)


def _moe_block_maps():
    rows = lambda j, i, be, used: (_used_block(i, used), 0)
    weight = lambda j, i, be, used: (be[_used_block(i, used)], 0, j)
    out = lambda j, i, be, used: (i, j)
    return rows, weight, out


def _moe_up(buf, block_expert, used, wg, wu, tn_pref=512):
    rows, d = buf.shape
    f = wg.shape[2]
    tn = _pick(f, tn_pref)
    n_blocks = rows // MOE_ROWS
    row_map, w_map, out_map = _moe_block_maps()
    return pl.pallas_call(
        _moe_up_kernel,
        grid_spec=pltpu.PrefetchScalarGridSpec(
            num_scalar_prefetch=2,
            grid=(f // tn, n_blocks),
            in_specs=[
                pl.BlockSpec((MOE_ROWS, d), row_map),
                pl.BlockSpec((None, d, tn), w_map),
                pl.BlockSpec((None, d, tn), w_map),
            ],
            out_specs=pl.BlockSpec((MOE_ROWS, tn), out_map),
            scratch_shapes=[pltpu.VMEM((d, tn), BF16), pltpu.VMEM((d, tn), BF16)],
        ),
        out_shape=jax.ShapeDtypeStruct((rows, f), BF16),
        compiler_params=_cparams(("arbitrary", "arbitrary")),
        name="moe_up",
    )(block_expert, used, buf, wg, wu)


def _moe_down_kernel(be_ref, used_ref, h_ref, w_ref, o_ref, w_bf):
    i = pl.program_id(1)

    @pl.when(i < used_ref[0])
    def _():
        @pl.when(_expert_changed(be_ref, i))
        def _():
            w_bf[...] = w_ref[...].astype(BF16)

        o_ref[...] = jnp.dot(h_ref[...], w_bf[...], preferred_element_type=F32)

    @pl.when(i >= used_ref[0])
    def _():
        o_ref[...] = jnp.zeros_like(o_ref)


def _moe_down(hid, block_expert, used, wd, tn_pref=512):
    rows, f = hid.shape
    d = wd.shape[2]
    tn = _pick(d, tn_pref)
    n_blocks = rows // MOE_ROWS
    row_map, w_map, out_map = _moe_block_maps()
    return pl.pallas_call(
        _moe_down_kernel,
        grid_spec=pltpu.PrefetchScalarGridSpec(
            num_scalar_prefetch=2,
            grid=(d // tn, n_blocks),
            in_specs=[
                pl.BlockSpec((MOE_ROWS, f), row_map),
                pl.BlockSpec((None, f, tn), w_map),
            ],
            out_specs=pl.BlockSpec((MOE_ROWS, tn), out_map),
            scratch_shapes=[pltpu.VMEM((f, tn), BF16)],
        ),
        out_shape=jax.ShapeDtypeStruct((rows, d), F32),
        compiler_params=_cparams(("arbitrary", "arbitrary")),
        name="moe_down",
    )(block_expert, used, hid, wd)


def _moe_combine_kernel(dest_ref, y_ref, x_ref, gate_ref, g0_ref, g1_ref, o_ref, b0_ref, b1_ref, sem_ref):
    base = (pl.program_id(0) * pl.num_programs(1) + pl.program_id(1)) * ROW_TILE

    def copies(r):
        t = (base + r) * TOP_K
        return (pltpu.make_async_copy(y_ref.at[pl.ds(dest_ref[t], 1), :], b0_ref.at[pl.ds(r, 1), :], sem_ref.at[0]),
                pltpu.make_async_copy(y_ref.at[pl.ds(dest_ref[t + 1], 1), :], b1_ref.at[pl.ds(r, 1), :],
                                      sem_ref.at[1]))

    def start(r, carry):
        c0, c1 = copies(r)
        c0.start()
        c1.start()
        return carry

    lax.fori_loop(0, ROW_TILE, start, 0, unroll=16)
    pltpu.make_async_copy(y_ref.at[pl.ds(0, ROW_TILE), :], b0_ref, sem_ref.at[0]).wait()
    pltpu.make_async_copy(y_ref.at[pl.ds(0, ROW_TILE), :], b1_ref, sem_ref.at[1]).wait()
    o_ref[...] = x_ref[...] + gate_ref[...] * (g0_ref[...] * b0_ref[...] + g1_ref[...] * b1_ref[...])


def _moe_combine(y, dest, x, gate, g0, g1):
    nb, _, d = x.shape
    t = g0.shape[1]
    return pl.pallas_call(
        _moe_combine_kernel,
        grid_spec=pltpu.PrefetchScalarGridSpec(
            num_scalar_prefetch=1,
            grid=(nb, t // ROW_TILE),
            in_specs=[
                pl.BlockSpec(memory_space=pl.ANY),
                pl.BlockSpec((None, ROW_TILE, d), lambda b, i, dest: (b, i, 0)),
                pl.BlockSpec((None, 1, d), lambda b, i, dest: (b, 0, 0)),
                pl.BlockSpec((None, ROW_TILE, 1), lambda b, i, dest: (b, i, 0)),
                pl.BlockSpec((None, ROW_TILE, 1), lambda b, i, dest: (b, i, 0)),
            ],
            out_specs=pl.BlockSpec((None, ROW_TILE, d), lambda b, i, dest: (b, i, 0)),
            scratch_shapes=[pltpu.VMEM((ROW_TILE, d), F32), pltpu.VMEM((ROW_TILE, d), F32),
                            pltpu.SemaphoreType.DMA((2,))],
        ),
        out_shape=jax.ShapeDtypeStruct((nb, t, d), F32),
        compiler_params=_cparams(("arbitrary", "arbitrary")),
        name="moe_combine",
    )(dest, y, x, gate, g0, g1)


def _moe_layer(a_tok, logits, x, gate, wg, wu, wd):
    nb, t, d = a_tok.shape
    n_exp = wg.shape[0]
    n_tok = nb * t
    n_assign = n_tok * TOP_K
    top_logit, top_idx = lax.top_k(logits.reshape(n_tok, -1)[:, :n_exp], TOP_K)
    gates = jax.nn.softmax(top_logit, axis=-1)
    flat_e = top_idx.reshape(-1)
    onehot = (flat_e[:, None] == jnp.arange(n_exp)[None, :]).astype(jnp.int32)
    rank = jnp.take_along_axis(jnp.cumsum(onehot, axis=0) - onehot, flat_e[:, None], axis=1)[:, 0]
    counts = jnp.sum(onehot, axis=0)
    padded = (counts + MOE_ROWS - 1) // MOE_ROWS * MOE_ROWS
    pad_end = jnp.cumsum(padded)
    pad_start = pad_end - padded
    dest = (pad_start[flat_e] + rank).astype(jnp.int32)
    n_blocks = -(-n_assign // MOE_ROWS) + n_exp
    src = jnp.zeros((n_blocks * MOE_ROWS,), jnp.int32).at[dest].set(jnp.arange(n_assign, dtype=jnp.int32) // TOP_K)
    block_expert = jnp.minimum(
        jnp.searchsorted(pad_end, jnp.arange(n_blocks) * MOE_ROWS, side='right'), n_exp - 1).astype(jnp.int32)
    used = (pad_end[-1:] // MOE_ROWS).astype(jnp.int32)
    buf = _moe_gather(a_tok.reshape(n_tok, d), src, used, n_blocks)
    hid = _moe_up(buf, block_expert, used, wg, wu)
    y = _moe_down(hid, block_expert, used, wd)
    return _moe_combine(y, dest, x, gate, gates[:, 0].reshape(nb, t, 1), gates[:, 1].reshape(nb, t, 1))


def _rope_tables(n_lat, n_ctx, rot_dim):
    rows = n_lat // GRID_W
    row = jnp.repeat(jnp.arange(rows), GRID_W).astype(F32)
    col = jnp.tile(jnp.arange(GRID_W), rows).astype(F32)
    n_freq = rot_dim // 4
    inv = ROPE_THETA ** (-jnp.arange(n_freq, dtype=F32) / n_freq)
    ang = jnp.concatenate([row[:, None] * inv, col[:, None] * inv], axis=-1)
    ang = jnp.concatenate([ang, jnp.zeros((n_ctx, rot_dim // 2), F32)], axis=0)
    return jnp.cos(ang), jnp.sin(ang)


def _rwkv_mixer(pa, n_lat, mu, w0, w_up, a0, a_up, g_up, k_k, k_a, r_k, lnx_w, lnx_b):
    r, v, kk, wl, kd, bb, bonus, g = _rwkv_prep(pa, n_lat // ROW_TILE, mu, w0, w_up, a0, a_up, g_up, k_k, k_a,
                                                 r_k.reshape(-1))
    yf, yr = _rwkv_scan(r, v, kk, wl, kd, bb, n_lat)
    return _rwkv_out(yf, yr, bonus, g, lnx_w, lnx_b)


def _diff_mixer(pb, n_lat, lq1, lk1, lq2, lk2, subln_g, lambda_init, tables):
    dh = subln_g.shape[0] // 2
    nh = pb.shape[2] // 3 // (2 * dh)
    lam = jnp.exp(jnp.sum(lq1 * lk1)) - jnp.exp(jnp.sum(lq2 * lk2)) + lambda_init
    q, k, v1 = _diff_prep(pb, tables, dh // 2, dh ** -0.5 * LOG2E)
    scale = 1.0 - lambda_init
    tq = ATTN_ROWS // 2
    return _diff_attention(q, k, v1, lam, subln_g, nh, scale, tq, n_lat)


def _gqa_mixer(pc, n_lat, q_norm_g, k_norm_g, n_heads, n_kv, tables):
    dh = q_norm_g.shape[0]
    assert dh == LANES
    q, k, v1 = _gqa_prep(pc, tables, q_norm_g, k_norm_g, n_heads, n_kv, dh ** -0.5 * LOG2E)
    group = n_heads // n_kv
    tq = ATTN_ROWS // group
    return _attention(q, k, v1, n_kv, group, dh, dh, tq, n_lat)


def _mla_mixer(pd, n_lat, q_norm_g, q_up, kv_norm_g, kv_up, n_heads, d_rope, tables):
    d_qk = q_up.shape[1] // n_heads
    q, k, v1 = _mla_prep(pd, tables, q_norm_g, q_up, kv_norm_g, kv_up, n_heads, d_rope, d_qk ** -0.5 * LOG2E)
    dk, d_v = 2 * LANES, LANES
    return _attention(q, k, v1, n_heads, 1, dk, d_v, ATTN_ROWS, n_lat, MLA_KEYS)


def _pad_cols(w, mult):
    return jnp.pad(w, ((0, 0), (0, -w.shape[1] % mult)))


def kernel(x, c, ctx, c_ctx, ada_w, ada_b, norm_mix_g, norm_ffn_g, w_in, rwkv_mu, rwkv_w0, rwkv_w_up, rwkv_a0, rwkv_a_up, rwkv_g_up, rwkv_k_k, rwkv_k_a, rwkv_r_k, rwkv_lnx_w, rwkv_lnx_b, diff_lq1, diff_lk1, diff_lq2, diff_lk2, diff_subln_g, gqa_q_norm_g, gqa_k_norm_g, mla_q_norm_g, mla_q_up, mla_kv_norm_g, mla_kv_up, w_out, ffn_w_gate, ffn_w_up, ffn_w_down, moe_router, moe_w_gate, moe_w_up, moe_w_down, final_norm_g):
    nb, n_lat, d = x.shape
    n_ctx = ctx.shape[1]
    depth = ada_w.shape[0]
    assert n_ctx == ROW_TILE and n_lat % ROW_TILE == 0 and n_lat % GRID_W == 0
    n_lat_tiles = n_lat // ROW_TILE
    n_all_tiles = n_lat_tiles + 1

    gw = rwkv_k_k.shape[1]
    cols_a = 3 * gw + rwkv_w_up.shape[2] + rwkv_a_up.shape[2] + rwkv_g_up.shape[1]
    cols_b = 3 * gw
    gqa_dh = gqa_q_norm_g.shape[1]
    mla_rope = 64
    mla_heads = 8
    gqa_heads = gw // gqa_dh
    cols_d = mla_q_norm_g.shape[1] + mla_kv_norm_g.shape[1] + mla_rope
    cols_c = w_in.shape[2] - cols_a - cols_b - cols_d
    gqa_kv = (cols_c - gw) // (2 * gqa_dh)
    offs = (0, cols_a, cols_a + cols_b, cols_a + cols_b + cols_c, w_in.shape[2])

    rope_diff = _lane_tables(_rope_tables(n_lat, n_ctx, diff_subln_g.shape[1] // 2))
    rope_gqa = _lane_tables(_rope_tables(n_lat, n_ctx, gqa_dh))
    rope_mla = _lane_tables(_rope_tables(n_lat, n_ctx, mla_rope))

    cvec = jnp.concatenate([c, c_ctx[None, :], jnp.zeros((8 - nb - 1, d), F32)], axis=0)
    mods = _ada(cvec, ada_w, ada_b)[:, :nb + 1].reshape(depth, nb + 1, 6, 1, d)

    h = jnp.concatenate([x, ctx], axis=1)
    s_len = n_lat + n_ctx
    flat = lambda z: z.reshape(nb * s_len, z.shape[-1])
    for l in range(depth):
        need_ctx = l < depth - 1
        m = [mods[l, :, i] for i in range(6)]

        a = flat(_norm_mod(h, norm_mix_g[l], m[1], m[0], n_all_tiles, n_lat_tiles))
        groups = [_matmul(a, _pad_cols(w_in[l][:, offs[i]:offs[i + 1]], 768).astype(BF16), F32, 1536)
                  .reshape(nb, s_len, -1) for i in range(4)]
        oa = _rwkv_mixer(groups[0], n_lat, rwkv_mu[l], rwkv_w0[l], rwkv_w_up[l],
                         rwkv_a0[l], rwkv_a_up[l], rwkv_g_up[l], rwkv_k_k[l], rwkv_k_a[l], rwkv_r_k[l],
                         rwkv_lnx_w[l], rwkv_lnx_b[l])
        lambda_init = 0.8 - 0.6 * math.exp(-0.3 * l)
        ob = _diff_mixer(groups[1], n_lat, diff_lq1[l], diff_lk1[l], diff_lq2[l], diff_lk2[l],
                         diff_subln_g[l], lambda_init, rope_diff)
        oc = _gqa_mixer(groups[2], n_lat, gqa_q_norm_g[l], gqa_k_norm_g[l], gqa_heads, gqa_kv, rope_gqa)
        od = _mla_mixer(groups[3], n_lat, mla_q_norm_g[l], mla_q_up[l], mla_kv_norm_g[l], mla_kv_up[l],
                        mla_heads, mla_rope, rope_mla)
        h = _matmul_resid([flat(o) for o in (oa, ob, oc, od)], w_out[l].astype(BF16), flat(h), m[2],
                          n_all_tiles, tn_pref=1024).reshape(nb, s_len, d)

        j = l // 2
        if l % 2 == 0:
            d_ff = ffn_w_gate.shape[2]
            pad_f = -d_ff % 512
            w_g = jnp.pad(ffn_w_gate[j], ((0, 0), (0, pad_f))).astype(BF16)
            w_u = jnp.pad(ffn_w_up[j], ((0, 0), (0, pad_f))).astype(BF16)
            w_d = jnp.pad(ffn_w_down[j], ((0, pad_f), (0, 0))).astype(BF16)
            a = flat(_norm_mod(h, norm_ffn_g[l], m[4], m[3], n_all_tiles, n_lat_tiles))
            hid = _matmul_swiglu(a, w_g, w_u)
            h = _matmul_resid([hid], w_d, flat(h), m[5], n_all_tiles, tm=ROW_TILE).reshape(nb, s_len, d)
        else:
            assert not need_ctx
            a, logits = _norm_router(h, norm_ffn_g[l], m[4], m[3], moe_router[j], n_lat_tiles, n_lat_tiles)
            h = _moe_layer(a, logits, h, m[5][:nb], moe_w_gate[j], moe_w_up[j], moe_w_down[j])
    return _final_norm(h, final_norm_g, n_lat_tiles)
```

```python
import functools
import itertools
import math

import jax
import jax.numpy as jnp
from jax import lax
from jax.experimental import pallas as pl
from jax.experimental.pallas import tpu as pltpu

F32 = jnp.float32
BF16 = jnp.bfloat16

GRID_W = 64
ROPE_THETA = 10000.0
NORM_EPS = 1e-6
SUBLN_EPS = 1e-5
GN_EPS = 64e-5
TOP_K = 2

LANES = 128
MXU_WIDTH = 256
ROW_TILE = 256
RWKV_HEAD = 64
RWKV_CHUNK = 64
RWKV_QUAD = 4
RWKV_STEP_QUADS = 4
MOE_ROWS = 512
MM_ROWS = 512
ATTN_KEYS = 1408
MLA_KEYS = 768
ATTN_ROWS = 1024
LOG2E = 1.4426950408889634
VMEM_LIMIT = 56 * 1024 * 1024


def _cparams(sem):
    return pltpu.CompilerParams(dimension_semantics=sem, vmem_limit_bytes=VMEM_LIMIT)


def _pick(n, pref):
    if n <= pref:
        return n
    t = pref - pref % LANES
    while t >= LANES:
        if n % t == 0:
            return t
        t -= LANES
    return n


def _ada_kernel(c_ref, w_ref, b_ref, o_ref):
    c = c_ref[...]
    s = (c * jax.nn.sigmoid(c)).astype(BF16)
    o_ref[...] = jnp.dot(s, w_ref[...].astype(BF16), preferred_element_type=F32) + b_ref[...]


def _ada(cvec, ada_w, ada_b):
    n_layers, d, n = ada_w.shape
    tn = _pick(n, 1024)
    return pl.pallas_call(
        _ada_kernel,
        grid=(n_layers, n // tn),
        in_specs=[
            pl.BlockSpec((8, d), lambda l, j: (0, 0)),
            pl.BlockSpec((None, d, tn), lambda l, j: (l, 0, j)),
            pl.BlockSpec((None, 1, tn), lambda l, j: (l, 0, j)),
        ],
        out_specs=pl.BlockSpec((None, 8, tn), lambda l, j: (l, 0, j)),
        out_shape=jax.ShapeDtypeStruct((n_layers, 8, n), F32),
        compiler_params=_cparams(("parallel", "parallel")),
        name="ada",
    )(cvec, ada_w, ada_b.reshape(n_layers, 1, n))


def _mod_sel(n_lat_tiles, n_batch):
    return lambda b, i: (jnp.where(i >= n_lat_tiles, n_batch, b), 0, 0)


def _norm_mod_kernel(x_ref, g_ref, sc_ref, sh_ref, o_ref):
    x = x_ref[...]
    y = x * lax.rsqrt(jnp.mean(x * x, axis=-1, keepdims=True) + NORM_EPS) * g_ref[...]
    o_ref[...] = (y * (1.0 + sc_ref[...]) + sh_ref[...]).astype(o_ref.dtype)


def _norm_mod(x, g, scale, shift, n_tiles, n_lat_tiles, out_dtype=BF16):
    nb, _, d = x.shape
    sel = _mod_sel(n_lat_tiles, nb)
    return pl.pallas_call(
        _norm_mod_kernel,
        grid=(nb, n_tiles),
        in_specs=[
            pl.BlockSpec((None, ROW_TILE, d), lambda b, i: (b, i, 0)),
            pl.BlockSpec((1, d), lambda b, i: (0, 0)),
            pl.BlockSpec((None, 1, d), sel),
            pl.BlockSpec((None, 1, d), sel),
        ],
        out_specs=pl.BlockSpec((None, ROW_TILE, d), lambda b, i: (b, i, 0)),
        out_shape=jax.ShapeDtypeStruct((nb, n_tiles * ROW_TILE, d), out_dtype),
        compiler_params=_cparams(("parallel", "parallel")),
        name="norm_mod",
    )(x, g.reshape(1, d), scale, shift)


def _norm_router_kernel(x_ref, g_ref, sc_ref, sh_ref, rh_ref, rl_ref, o_ref, lg_ref):
    x = x_ref[...]
    y = x * lax.rsqrt(jnp.mean(x * x, axis=-1, keepdims=True) + NORM_EPS) * g_ref[...]
    a = y * (1.0 + sc_ref[...]) + sh_ref[...]
    o_ref[...] = a
    ah = a.astype(BF16)
    al = (a - ah.astype(F32)).astype(BF16)
    rh = rh_ref[...]
    lg_ref[...] = (jnp.dot(ah, rh, preferred_element_type=F32)
                   + jnp.dot(al, rh, preferred_element_type=F32)
                   + jnp.dot(ah, rl_ref[...], preferred_element_type=F32))


def _norm_router(x, g, scale, shift, router, n_tiles, n_lat_tiles):
    nb, _, d = x.shape
    n_exp = router.shape[1]
    r_pad = jnp.pad(router, ((0, 0), (0, LANES - n_exp)))
    r_hi = r_pad.astype(BF16)
    r_lo = (r_pad - r_hi.astype(F32)).astype(BF16)
    sel = _mod_sel(n_lat_tiles, nb)
    rows = n_tiles * ROW_TILE
    return pl.pallas_call(
        _norm_router_kernel,
        grid=(nb, n_tiles),
        in_specs=[
            pl.BlockSpec((None, ROW_TILE, d), lambda b, i: (b, i, 0)),
            pl.BlockSpec((1, d), lambda b, i: (0, 0)),
            pl.BlockSpec((None, 1, d), sel),
            pl.BlockSpec((None, 1, d), sel),
            pl.BlockSpec((d, LANES), lambda b, i: (0, 0)),
            pl.BlockSpec((d, LANES), lambda b, i: (0, 0)),
        ],
        out_specs=[
            pl.BlockSpec((None, ROW_TILE, d), lambda b, i: (b, i, 0)),
            pl.BlockSpec((None, ROW_TILE, LANES), lambda b, i: (b, i, 0)),
        ],
        out_shape=[
            jax.ShapeDtypeStruct((nb, rows, d), F32),
            jax.ShapeDtypeStruct((nb, rows, LANES), F32),
        ],
        compiler_params=_cparams(("parallel", "parallel")),
        name="norm_router",
    )(x, g.reshape(1, d), scale, shift, r_hi, r_lo)


def _final_norm_kernel(x_ref, g_ref, o_ref):
    x = x_ref[...]
    o_ref[...] = x * lax.rsqrt(jnp.mean(x * x, axis=-1, keepdims=True) + NORM_EPS) * g_ref[...]


def _final_norm(x, g, n_tiles):
    nb, _, d = x.shape
    rows = n_tiles * ROW_TILE
    return pl.pallas_call(
        _final_norm_kernel,
        grid=(nb, n_tiles),
        in_specs=[
            pl.BlockSpec((None, ROW_TILE, d), lambda b, i: (b, i, 0)),
            pl.BlockSpec((1, d), lambda b, i: (0, 0)),
        ],
        out_specs=pl.BlockSpec((None, ROW_TILE, d), lambda b, i: (b, i, 0)),
        out_shape=jax.ShapeDtypeStruct((nb, rows, d), F32),
        compiler_params=_cparams(("parallel", "parallel")),
        name="final_norm",
    )(x, g.reshape(1, d))


def _row_tile(rows):
    return MM_ROWS if rows % MM_ROWS == 0 else ROW_TILE


def _mm_kernel(a_ref, w_ref, o_ref):
    o_ref[...] = jnp.dot(a_ref[...], w_ref[...], preferred_element_type=F32).astype(o_ref.dtype)


def _matmul(a, w, out_dtype=F32, tn_pref=1024):
    rows, k = a.shape
    n = w.shape[1]
    tn = _pick(n, tn_pref)
    tm = _row_tile(rows)
    return pl.pallas_call(
        _mm_kernel,
        grid=(n // tn, rows // tm),
        in_specs=[
            pl.BlockSpec((tm, k), lambda j, i: (i, 0)),
            pl.BlockSpec((k, tn), lambda j, i: (0, j)),
        ],
        out_specs=pl.BlockSpec((tm, tn), lambda j, i: (i, j)),
        out_shape=jax.ShapeDtypeStruct((rows, n), out_dtype),
        compiler_params=_cparams(("parallel", "parallel")),
        name="matmul",
    )(a, w)


def _mm_swiglu_kernel(a_ref, wg_ref, wu_ref, o_ref):
    a = a_ref[...]
    g = jnp.dot(a, wg_ref[...], preferred_element_type=F32)
    u = jnp.dot(a, wu_ref[...], preferred_element_type=F32)
    o_ref[...] = (g * jax.nn.sigmoid(g) * u).astype(o_ref.dtype)


def _matmul_swiglu(a, wg, wu, tn_pref=1024):
    rows, k = a.shape
    n = wg.shape[1]
    tn = _pick(n, tn_pref)
    tm = _row_tile(rows)
    return pl.pallas_call(
        _mm_swiglu_kernel,
        grid=(n // tn, rows // tm),
        in_specs=[
            pl.BlockSpec((tm, k), lambda j, i: (i, 0)),
            pl.BlockSpec((k, tn), lambda j, i: (0, j)),
            pl.BlockSpec((k, tn), lambda j, i: (0, j)),
        ],
        out_specs=pl.BlockSpec((tm, tn), lambda j, i: (i, j)),
        out_shape=jax.ShapeDtypeStruct((rows, n), BF16),
        compiler_params=_cparams(("parallel", "parallel")),
        name="matmul_swiglu",
    )(a, wg, wu)


def _mm_resid_kernel(*refs, n_parts, n_gates):
    a_refs, w_refs = refs[:n_parts], refs[n_parts:2 * n_parts]
    res_ref = refs[2 * n_parts]
    gate_refs = refs[2 * n_parts + 1:2 * n_parts + 1 + n_gates]
    o_ref = refs[-1]
    acc = jnp.dot(a_refs[0][...], w_refs[0][...], preferred_element_type=F32)
    for a_ref, w_ref in zip(a_refs[1:], w_refs[1:]):
        acc = acc + jnp.dot(a_ref[...], w_ref[...], preferred_element_type=F32)
    rows = acc.shape[0] // n_gates
    for t, gate_ref in enumerate(gate_refs):
        sl = slice(t * rows, (t + 1) * rows)
        o_ref[sl, :] = res_ref[sl, :] + gate_ref[...] * acc[sl, :]


def _matmul_resid(parts, w, res, gate, tiles_per_batch, tm=None, tn_pref=512):
    rows, kg = parts[0].shape
    n_parts = len(parts)
    n = w.shape[1]
    nb = gate.shape[0] - 1
    tn = _pick(n, tn_pref)
    tm = tm or _row_tile(rows)
    n_gates = tm // ROW_TILE

    def gate_spec(t):
        def index(j, i):
            tile = i * n_gates + t
            pos = tile % tiles_per_batch
            return (jnp.where(pos == tiles_per_batch - 1, nb, tile // tiles_per_batch), 0, j)
        return pl.BlockSpec((None, 1, tn), index)

    return pl.pallas_call(
        functools.partial(_mm_resid_kernel, n_parts=n_parts, n_gates=n_gates),
        grid=(n // tn, rows // tm),
        in_specs=([pl.BlockSpec((tm, kg), lambda j, i: (i, 0))] * n_parts
                  + [pl.BlockSpec((kg, tn), lambda j, i, g=g: (g, j)) for g in range(n_parts)]
                  + [pl.BlockSpec((tm, tn), lambda j, i: (i, j))]
                  + [gate_spec(t) for t in range(n_gates)]),
        out_specs=pl.BlockSpec((tm, tn), lambda j, i: (i, j)),
        out_shape=jax.ShapeDtypeStruct((rows, n), F32),
        compiler_params=_cparams(("parallel", "parallel")),
        name="matmul_resid",
    )(*parts, *([w] * n_parts), res, *([gate] * n_gates))


_NT = (((1,), (1,)), ((), ()))


def _online_softmax(qs, k_ref, v_ref, ck):
    n_rows = qs.shape[0]
    m = jnp.full((n_rows, 1), -jnp.inf, F32)
    acc = jnp.zeros((n_rows, v_ref.shape[1]), F32)
    for c in range(k_ref.shape[0] // ck):
        keys = slice(c * ck, (c + 1) * ck)
        s = lax.dot_general(qs, k_ref[keys, :], _NT, preferred_element_type=F32)
        m_new = jnp.maximum(m, jnp.max(s, axis=-1, keepdims=True))
        p = jnp.exp2(s - m_new).astype(BF16)
        acc = jnp.exp2(m - m_new) * acc + jnp.dot(p, v_ref[keys, :], preferred_element_type=F32)
        m = m_new
    return acc


def _attend_rows(body, pre_args, pre_specs, q, k, v1, n_heads, qw, dk, ow, tq, n_lat, name, keys=ATTN_KEYS):
    nb, s_len, _ = q.shape
    n_ctx = s_len - n_lat
    dvp = v1.shape[2] // n_heads
    out_shape = jax.ShapeDtypeStruct((nb, s_len, n_heads * ow), BF16)

    def run(tq, n_q, q_off, k_rows, k_off, prev):
        kern = functools.partial(body, ck=_pick(k_rows, keys))
        specs = pre_specs + [
            pl.BlockSpec((None, tq, qw), lambda b, h, i: (b, i + q_off, h)),
            pl.BlockSpec((None, k_rows, dk), lambda b, h, i: (b, k_off, h)),
            pl.BlockSpec((None, k_rows, dvp), lambda b, h, i: (b, k_off, h)),
        ]
        args = pre_args + [q, k, v1, prev]
        return pl.pallas_call(
            lambda *refs: kern(*refs[:-2], refs[-1]),
            grid=(nb, n_heads, n_q),
            in_specs=specs + [pl.BlockSpec(memory_space=pl.ANY)],
            out_specs=pl.BlockSpec((None, tq, ow), lambda b, h, i: (b, i + q_off, h)),
            out_shape=out_shape,
            input_output_aliases={len(args) - 1: 0},
            compiler_params=_cparams(("parallel", "parallel", "parallel")),
            name=name,
        )(*args)

    tq = math.gcd(tq, n_lat)
    out = run(tq, n_lat // tq, 0, s_len, 0, jnp.zeros(out_shape.shape, out_shape.dtype))
    return run(n_ctx, 1, n_lat // n_ctx, n_ctx, n_lat // n_ctx, out)


def _attn_kernel(q_ref, k_ref, v_ref, o_ref, *, pieces, dv, ck):
    q = q_ref[...]
    tq, dk = q.shape[0], q.shape[1] // pieces
    qs = q if pieces == 1 else jnp.concatenate([q[:, g * dk:(g + 1) * dk] for g in range(pieces)], axis=0)
    acc = _online_softmax(qs, k_ref, v_ref, ck)
    o = acc[:, :dv] / acc[:, dv:dv + 1]
    for g in range(pieces):
        o_ref[:, g * dv:(g + 1) * dv] = o[g * tq:(g + 1) * tq].astype(o_ref.dtype)


def _attention(q, k, v1, n_kv_heads, group, dk, dv, tq, n_lat, keys=ATTN_KEYS):
    return _attend_rows(functools.partial(_attn_kernel, pieces=group, dv=dv), [], [], q, k, v1, n_kv_heads,
                        group * dk, dk, group * dv, tq, n_lat, "attention", keys)


def _diff_attn_kernel(lam_ref, g_ref, q_ref, k_ref, v_ref, o_ref, *, dv, ck, out_scale):
    q = q_ref[...]
    tq, width = q.shape
    first = lax.broadcasted_iota(jnp.int32, (tq, width), 1) < width // 2
    zero = jnp.zeros_like(q)
    qs = jnp.concatenate([jnp.where(first, q, zero), jnp.where(first, zero, q)], axis=0)
    acc = _online_softmax(qs, k_ref, v_ref, ck)
    o = acc[:, :dv] / acc[:, dv:dv + 1]
    o = o[:tq] - lam_ref[0] * o[tq:]
    o = o * lax.rsqrt(jnp.mean(o * o, axis=-1, keepdims=True) + SUBLN_EPS) * g_ref[...]
    o_ref[...] = (o * out_scale).astype(o_ref.dtype)


def _diff_attention(q, k, v1, lam, subln_g, n_heads, out_scale, tq, n_lat):
    dh = q.shape[2] // n_heads
    return _attend_rows(functools.partial(_diff_attn_kernel, dv=dh, out_scale=out_scale),
                        [lam.reshape(1).astype(F32), subln_g.reshape(1, dh)],
                        [pl.BlockSpec(memory_space=pltpu.SMEM), pl.BlockSpec((1, dh), lambda b, h, i: (0, 0))],
                        q, k, v1, n_heads, dh, dh, dh, tq, n_lat, "diff_attention")


def _rwkv_chunk(r, v, kk, wl, kd, bb, h_state, rev, out):
    c, w = r.shape
    nh = w // RWKV_HEAD
    assert c == RWKV_HEAD
    tr = lax.broadcasted_iota(jnp.int32, (c, w), 0)
    cum = wl
    step = 1
    while step < c:
        shifted = pltpu.roll(cum, c - step if rev else step, axis=0)
        cum = cum + jnp.where((tr < c - step) if rev else (tr >= step), shifted, 0.0)
        step *= 2
    cum_prev = cum - wl
    last = 0 if rev else c - 1
    total = cum[last:last + 1, :]
    mid = cum[c // 2:c // 2 + 1, :]
    a = -kk
    e_mc = jnp.exp(mid - cum)
    e_end = jnp.exp(total - cum)

    lane = lax.broadcasted_iota(jnp.int32, (c, w), 1)
    lane_head, ts = lane // c, lane % c
    strict = (ts > tr) if rev else (ts < tr)
    incl = (ts >= tr) if rev else (ts <= tr)

    def stack(x):
        return jnp.concatenate([jnp.where(lane_head == h, x, 0.0) for h in range(nh)], axis=0).astype(BF16)

    def dot_nt(x, y):
        return lax.dot_general(x.astype(BF16), y.astype(BF16), _NT, preferred_element_type=F32)

    def dot_tn(x, y):
        return lax.dot_general(x.astype(BF16), y.astype(BF16), (((0,), (0,)), ((), ())), preferred_element_type=F32)

    def dot(x, y):
        return jnp.dot(x.astype(BF16), y.astype(BF16), preferred_element_type=F32)

    lhs = jnp.concatenate([a * jnp.exp(cum_prev - mid), r * jnp.exp(cum - mid)], axis=0)
    g_b = dot_nt(lhs, stack(bb * e_mc))
    g_k = dot_nt(lhs, stack(kd * e_mc))
    n_ab = jnp.where(strict, g_b[:c], 0.0)
    a_rb = jnp.where(incl, g_b[c:], 0.0)
    a_ak = jnp.where(strict, g_k[:c], 0.0)
    a_rk = jnp.where(incl, g_k[c:], 0.0)
    yield

    t_inv = jnp.where(ts == tr, 1.0, 0.0) + n_ab
    pw = n_ab
    for _ in range(int(math.log2(c)) - 1):
        pw = dot(pw, stack(pw))
        yield
        t_inv = t_inv + dot(pw, stack(t_inv))
    yield

    v_st = stack(v)
    x = dot(a * jnp.exp(cum_prev), h_state) + dot(a_ak, v_st)
    yield
    u = dot(t_inv, stack(x))
    yield
    y = dot(r * jnp.exp(cum), h_state) + dot(a_rb, stack(u)) + dot(a_rk, v_st)
    rw = lax.broadcasted_iota(jnp.int32, (w, w), 0)
    cw = lax.broadcasted_iota(jnp.int32, (w, w), 1)
    decay_end = jnp.where(rw == cw, jnp.broadcast_to(jnp.exp(total), (w, w)), 0.0)
    update = dot_tn(jnp.concatenate([bb * e_end, kd * e_end], axis=0), jnp.concatenate([u, v], axis=0))
    out.append((y, dot(decay_end, h_state) + jnp.where(rw // RWKV_HEAD == cw // RWKV_HEAD, update, 0.0)))


def _rwkv_scan_kernel(rf_ref, vf_ref, kkf_ref, wlf_ref, kdf_ref, bbf_ref,
                      rr_ref, vr_ref, kkr_ref, wlr_ref, kdr_ref, bbr_ref,
                      yf_ref, yr_ref, hf_ref, hr_ref):
    @pl.when(pl.program_id(2) == 0)
    def _():
        hf_ref[...] = jnp.zeros_like(hf_ref)
        hr_ref[...] = jnp.zeros_like(hr_ref)

    qw = RWKV_QUAD * RWKV_HEAD
    chains = []
    for i in range(hf_ref.shape[0]):
        sl = slice(i * qw, (i + 1) * qw)
        for refs, y_ref, h_ref, rev in (((rf_ref, vf_ref, kkf_ref, wlf_ref, kdf_ref, bbf_ref), yf_ref, hf_ref, False),
                                        ((rr_ref, vr_ref, kkr_ref, wlr_ref, kdr_ref, bbr_ref), yr_ref, hr_ref, True)):
            out = []
            chains.append((_rwkv_chunk(*(ref[:, sl] for ref in refs), h_ref[i], rev, out), out, y_ref, h_ref, i, sl))
    for _ in itertools.zip_longest(*(chain[0] for chain in chains)):
        pass
    for _, out, y_ref, h_ref, i, sl in chains:
        y_ref[:, sl], h_ref[i] = out[0]


def _rwkv_scan(r, v, kk, wl, kd, bb, n_lat):
    nb, s_len, w = r.shape
    c = RWKV_CHUNK
    qw = RWKV_QUAD * RWKV_HEAD
    nq = RWKV_STEP_QUADS if w % (RWKV_STEP_QUADS * qw) == 0 else 1
    bw = nq * qw
    n_l, n_all = n_lat // c, s_len // c
    n_c = n_all - n_l

    def fwd(s):
        return jnp.where(s < n_c, n_l + s, s - n_c)

    def rev(s):
        return n_all - 1 - s

    def shared(order):
        return pl.BlockSpec((None, c, bw), lambda b, q, s: (b, order(s), q))

    def per_dir(d, order):
        return pl.BlockSpec((None, None, c, bw), lambda b, q, s: (d, b, order(s), q))

    yf, yr = pl.pallas_call(
        _rwkv_scan_kernel,
        grid=(nb, w // bw, n_all),
        in_specs=[shared(fwd), shared(fwd), shared(fwd), per_dir(0, fwd), per_dir(0, fwd), per_dir(0, fwd),
                  shared(rev), shared(rev), shared(rev), per_dir(1, rev), per_dir(1, rev), per_dir(1, rev)],
        out_specs=[shared(fwd), shared(rev)],
        out_shape=[jax.ShapeDtypeStruct((nb, s_len, w), F32)] * 2,
        scratch_shapes=[pltpu.VMEM((nq, qw, qw), F32), pltpu.VMEM((nq, qw, qw), F32)],
        compiler_params=_cparams(("parallel", "parallel", "arbitrary")),
        name="rwkv_scan",
    )(r, v, kk, wl, kd, bb, r, v, kk, wl, kd, bb)
    return yf, yr


def _group_sum(x, ones_bd):
    hi = x.astype(BF16)
    lo = (x - hi.astype(F32)).astype(BF16)
    wb = ones_bd.shape[0]
    return jnp.concatenate(
        [jnp.dot(hi[:, b:b + wb], ones_bd, preferred_element_type=F32)
         + jnp.dot(lo[:, b:b + wb], ones_bd, preferred_element_type=F32) for b in range(0, x.shape[1], wb)], axis=1)


def _rwkv_prep_kernel(x_ref, prev_ref, next_ref, mu_ref, kkg_ref, ka_ref, rk_ref, w0_ref, a0_ref, wup_ref, aup_ref,
                      gup_ref, ones_ref, r_ref, v_ref, kk_ref, wl_ref, kd_ref, bb_ref, bonus_ref, g_ref,
                      *, n_lat_tiles, gw, lora_w, lora_a):
    i = pl.program_id(1)
    x = x_ref[...]
    rows = x.shape[0]
    row = lax.broadcasted_iota(jnp.int32, x.shape, 0)
    has_prev = jnp.logical_and(i > 0, i < n_lat_tiles)
    has_next = i < n_lat_tiles - 1
    prev_row = jnp.where(has_prev, prev_ref[7:8, :], 0.0)
    next_row = jnp.where(has_next, next_ref[0:1, :], 0.0)
    prev = jnp.where(row == 0, prev_row, pltpu.roll(x, 1, axis=0))
    nxt = jnp.where(row == rows - 1, next_row, pltpu.roll(x, rows - 1, axis=0))
    ps = x + mu_ref[...] * (0.5 * (prev + nxt) - x)
    r, k, v = ps[:, :gw], ps[:, gw:2 * gw], ps[:, 2 * gw:3 * gw]
    o = 3 * gw
    wd, ad, gd = ps[:, o:o + lora_w], ps[:, o + lora_w:o + lora_w + lora_a], ps[:, o + lora_w + lora_a:]
    ones_bd = ones_ref[...]
    kk = k * kkg_ref[...]
    kk = kk * lax.rsqrt(jnp.maximum(_group_sum(kk * kk, ones_bd), 1e-24))
    tanh_wd = jnp.tanh(wd).astype(BF16)
    ad_b = ad.astype(BF16)
    kd_sum = jnp.zeros_like(k)
    for d in range(2):
        z = w0_ref[d] + jnp.dot(tanh_wd, wup_ref[d], preferred_element_type=F32)
        softplus_neg = jnp.maximum(-z, 0.0) + jnp.log1p(jnp.exp(-jnp.abs(z)))
        wl_ref[d] = -jnp.exp(-softplus_neg - 0.5)
        a_sig = jax.nn.sigmoid(a0_ref[d] + jnp.dot(ad_b, aup_ref[d], preferred_element_type=F32))
        kd = k * (1.0 + (a_sig - 1.0) * ka_ref[...])
        kd_ref[d] = kd
        bb_ref[d] = kk * a_sig
        kd_sum = kd_sum + kd
    r_ref[...] = r
    v_ref[...] = v
    kk_ref[...] = kk
    bonus_ref[...] = _group_sum(r * kd_sum * rk_ref[...], ones_bd) * v
    g_ref[...] = jnp.dot(jax.nn.sigmoid(gd).astype(BF16), gup_ref[...], preferred_element_type=F32)


def _head_ones(width, head):
    idx = jnp.arange(width) // head
    return (idx[:, None] == idx[None, :]).astype(BF16)


def _rwkv_prep(pa, n_lat_tiles, mu, w0, w_up, a0, a_up, g_up, k_k, k_a, r_k):
    nb, s_len, wp = pa.shape
    gw = k_k.shape[0]
    lora_w, lora_a = w_up.shape[1], a_up.shape[1]
    lora_g = wp - 3 * gw - lora_w - lora_a
    n_tiles = s_len // ROW_TILE
    sub = ROW_TILE // 8
    row = lambda z: z.reshape(1, -1)
    full = lambda shape: pl.BlockSpec(shape, lambda b, i: (0,) * len(shape))
    tile = lambda w: pl.BlockSpec((None, ROW_TILE, w), lambda b, i: (b, i, 0))
    tile2 = pl.BlockSpec((2, None, ROW_TILE, gw), lambda b, i: (0, b, i, 0))
    one = jax.ShapeDtypeStruct((nb, s_len, gw), F32)
    two = jax.ShapeDtypeStruct((2, nb, s_len, gw), F32)
    return pl.pallas_call(
        functools.partial(_rwkv_prep_kernel, n_lat_tiles=n_lat_tiles, gw=gw, lora_w=lora_w, lora_a=lora_a),
        grid=(nb, n_tiles),
        in_specs=[
            tile(wp),
            pl.BlockSpec((None, 8, wp), lambda b, i: (b, jnp.maximum(i * sub - 1, 0), 0)),
            pl.BlockSpec((None, 8, wp), lambda b, i: (b, jnp.minimum((i + 1) * sub, s_len // 8 - 1), 0)),
            full((1, wp)), full((1, gw)), full((1, gw)), full((1, gw)),
            full((2, 1, gw)), full((2, 1, gw)),
            full((2, lora_w, gw)), full((2, lora_a, gw)), full((lora_g, gw)), full((MXU_WIDTH, MXU_WIDTH)),
        ],
        out_specs=[tile(gw), tile(gw), tile(gw), tile2, tile2, tile2, tile(gw), tile(gw)],
        out_shape=[one, one, one, two, two, two, one, one],
        compiler_params=_cparams(("parallel", "parallel")),
        name="rwkv_prep",
    )(pa, pa, pa, row(jnp.pad(mu, (0, wp - mu.shape[0]))), row(k_k), row(k_a), row(r_k),
      w0.reshape(2, 1, gw), a0.reshape(2, 1, gw), w_up.astype(BF16), a_up.astype(BF16),
      jnp.pad(g_up, ((0, lora_g - g_up.shape[0]), (0, 0))).astype(BF16), _head_ones(MXU_WIDTH, RWKV_HEAD))


def _rwkv_out_kernel(yf_ref, yr_ref, bonus_ref, g_ref, lw_ref, lb_ref, ones_ref, o_ref):
    ones_bd = ones_ref[...]
    y = yf_ref[...] + yr_ref[...]
    inv_n = 1.0 / RWKV_HEAD
    dev = y - _group_sum(y, ones_bd) * inv_n
    var = _group_sum(dev * dev, ones_bd) * inv_n
    yn = dev * lax.rsqrt(var + GN_EPS) * lw_ref[...] + lb_ref[...]
    o_ref[...] = ((yn + bonus_ref[...]) * g_ref[...]).astype(o_ref.dtype)


def _rwkv_out(yf, yr, bonus, g, lnx_w, lnx_b):
    nb, s_len, gw = yf.shape
    tile = pl.BlockSpec((None, ROW_TILE, gw), lambda b, i: (b, i, 0))
    vec = pl.BlockSpec((1, gw), lambda b, i: (0, 0))
    return pl.pallas_call(
        _rwkv_out_kernel,
        grid=(nb, s_len // ROW_TILE),
        in_specs=[tile, tile, tile, tile, vec, vec, pl.BlockSpec((MXU_WIDTH, MXU_WIDTH), lambda b, i: (0, 0))],
        out_specs=tile,
        out_shape=jax.ShapeDtypeStruct((nb, s_len, gw), BF16),
        compiler_params=_cparams(("parallel", "parallel")),
        name="rwkv_out",
    )(yf, yr, bonus, g, lnx_w.reshape(1, gw), lnx_b.reshape(1, gw), _head_ones(MXU_WIDTH, RWKV_HEAD))


def _rope_block(xb, cos_t, sin_t, half):
    if 2 * half == LANES:
        partner = pltpu.roll(xb, half, axis=1)
    else:
        lane = lax.broadcasted_iota(jnp.int32, xb.shape, 1)
        partner = jnp.where(lane % (2 * half) < half, pltpu.roll(xb, LANES - half, axis=1),
                            pltpu.roll(xb, half, axis=1))
    return xb * cos_t + partner * sin_t


def _lane_tables(cs):
    cos, sin = cs
    reps = LANES // (2 * cos.shape[1])
    return (jnp.tile(jnp.concatenate([cos, cos], axis=1), (1, reps)),
            jnp.tile(jnp.concatenate([-sin, sin], axis=1), (1, reps)))


def _ones_column(rows, dtype):
    return jnp.where(lax.broadcasted_iota(jnp.int32, (rows, LANES), 1) == 0, 1.0, 0.0).astype(dtype)


def _blk(b):
    return slice(b * LANES, (b + 1) * LANES)


def _diff_prep_kernel(x_ref, cos_ref, sin_ref, q_ref, k_ref, v_ref, *, gw, half, q_scale):
    cos_t, sin_t = cos_ref[...], sin_ref[...]
    e0 = _ones_column(x_ref.shape[0], v_ref.dtype)
    nblk = gw // LANES
    for b in range(nblk):
        q_ref[:, _blk(b)] = (_rope_block(x_ref[:, _blk(b)], cos_t, sin_t, half) * q_scale).astype(q_ref.dtype)
        k_ref[:, _blk(b)] = _rope_block(x_ref[:, _blk(nblk + b)], cos_t, sin_t, half).astype(k_ref.dtype)
        v_ref[:, _blk(2 * b)] = x_ref[:, _blk(2 * nblk + b)].astype(v_ref.dtype)
        v_ref[:, _blk(2 * b + 1)] = e0


def _diff_prep(pb, tables, half, q_scale):
    nb, s_len, w3 = pb.shape
    gw = w3 // 3
    tile = lambda w: pl.BlockSpec((None, ROW_TILE, w), lambda b, i: (b, i, 0))
    tab = pl.BlockSpec((ROW_TILE, LANES), lambda b, i: (i, 0))
    return pl.pallas_call(
        functools.partial(_diff_prep_kernel, gw=gw, half=half, q_scale=q_scale),
        grid=(nb, s_len // ROW_TILE),
        in_specs=[tile(w3), tab, tab],
        out_specs=[tile(gw), tile(gw), tile(2 * gw)],
        out_shape=[jax.ShapeDtypeStruct((nb, s_len, gw), BF16), jax.ShapeDtypeStruct((nb, s_len, gw), BF16),
                   jax.ShapeDtypeStruct((nb, s_len, 2 * gw), BF16)],
        compiler_params=_cparams(("parallel", "parallel")),
        name="diff_prep",
    )(pb, *tables)


def _gqa_prep_kernel(x_ref, cos_ref, sin_ref, qg_ref, kg_ref, q_ref, k_ref, v_ref, *, n_heads, n_kv, q_scale):
    cos_t, sin_t = cos_ref[...], sin_ref[...]
    e0 = _ones_column(x_ref.shape[0], v_ref.dtype)

    def normed_rope(xb, g):
        xb = xb * lax.rsqrt(jnp.mean(xb * xb, axis=-1, keepdims=True) + NORM_EPS) * g
        return _rope_block(xb, cos_t, sin_t, LANES // 2)

    for b in range(n_heads):
        q_ref[:, _blk(b)] = (normed_rope(x_ref[:, _blk(b)], qg_ref[...]) * q_scale).astype(q_ref.dtype)
    for b in range(n_kv):
        k_ref[:, _blk(b)] = normed_rope(x_ref[:, _blk(n_heads + b)], kg_ref[...]).astype(k_ref.dtype)
        v_ref[:, _blk(2 * b)] = x_ref[:, _blk(n_heads + n_kv + b)].astype(v_ref.dtype)
        v_ref[:, _blk(2 * b + 1)] = e0


def _gqa_prep(pc, tables, q_norm_g, k_norm_g, n_heads, n_kv, q_scale):
    nb, s_len, w = pc.shape
    tile = lambda w_: pl.BlockSpec((None, ROW_TILE, w_), lambda b, i: (b, i, 0))
    tab = pl.BlockSpec((ROW_TILE, LANES), lambda b, i: (i, 0))
    vec = pl.BlockSpec((1, LANES), lambda b, i: (0, 0))
    return pl.pallas_call(
        functools.partial(_gqa_prep_kernel, n_heads=n_heads, n_kv=n_kv, q_scale=q_scale),
        grid=(nb, s_len // ROW_TILE),
        in_specs=[tile(w), tab, tab, vec, vec],
        out_specs=[tile(n_heads * LANES), tile(n_kv * LANES), tile(2 * n_kv * LANES)],
        out_shape=[jax.ShapeDtypeStruct((nb, s_len, n_heads * LANES), BF16),
                   jax.ShapeDtypeStruct((nb, s_len, n_kv * LANES), BF16),
                   jax.ShapeDtypeStruct((nb, s_len, 2 * n_kv * LANES), BF16)],
        compiler_params=_cparams(("parallel", "parallel")),
        name="gqa_prep",
    )(pc, *tables, q_norm_g.reshape(1, LANES), k_norm_g.reshape(1, LANES))


def _mla_prep_kernel(x_ref, cos_ref, sin_ref, qg_ref, kvg_ref, qup_ref, kvup_ref, q_ref, k_ref, v_ref,
                     *, q_lora, kv_lora, n_heads, half, q_scale):
    cos_t, sin_t = cos_ref[...], sin_ref[...]
    e0 = _ones_column(x_ref.shape[0], v_ref.dtype)

    def normed(z, g):
        return (z * lax.rsqrt(jnp.mean(z * z, axis=-1, keepdims=True) + NORM_EPS) * g).astype(BF16)

    q = jnp.dot(normed(x_ref[:, :q_lora], qg_ref[...]), qup_ref[...], preferred_element_type=F32)
    kv = jnp.dot(normed(x_ref[:, q_lora:q_lora + kv_lora], kvg_ref[...]), kvup_ref[...], preferred_element_type=F32)
    k_rope = _rope_block(x_ref[:, q_lora + kv_lora:q_lora + kv_lora + LANES], cos_t, sin_t, half).astype(k_ref.dtype)
    for h in range(n_heads):
        q_ref[:, _blk(2 * h)] = (q[:, _blk(2 * h)] * q_scale).astype(q_ref.dtype)
        q_ref[:, _blk(2 * h + 1)] = (_rope_block(q[:, _blk(2 * h + 1)], cos_t, sin_t, half) * q_scale).astype(q_ref.dtype)
        k_ref[:, _blk(2 * h)] = kv[:, _blk(h)].astype(k_ref.dtype)
        k_ref[:, _blk(2 * h + 1)] = k_rope
        v_ref[:, _blk(2 * h)] = kv[:, _blk(n_heads + h)].astype(v_ref.dtype)
        v_ref[:, _blk(2 * h + 1)] = e0


def _mla_prep(pd, tables, q_norm_g, q_up, kv_norm_g, kv_up, n_heads, d_rope, q_scale):
    nb, s_len, wp = pd.shape
    q_lora, kv_lora = q_norm_g.shape[0], kv_norm_g.shape[0]
    d_qk = q_up.shape[1] // n_heads
    d_nope = d_qk - d_rope
    assert d_nope == LANES and kv_up.shape[1] == n_heads * 2 * LANES and q_lora + kv_lora + LANES <= wp
    qw = jnp.pad(q_up.reshape(q_lora, n_heads, d_qk), ((0, 0), (0, 0), (0, 2 * LANES - d_qk)))
    qw = qw.reshape(q_lora, n_heads * 2 * LANES).astype(BF16)
    kvw = kv_up.reshape(kv_lora, n_heads, 2, LANES).transpose(0, 2, 1, 3).reshape(kv_lora, 2 * n_heads * LANES)
    tile = lambda w_: pl.BlockSpec((None, ROW_TILE, w_), lambda b, i: (b, i, 0))
    tab = pl.BlockSpec((ROW_TILE, LANES), lambda b, i: (i, 0))
    full = lambda shape: pl.BlockSpec(shape, lambda b, i: (0,) * len(shape))
    wide = n_heads * 2 * LANES
    out = jax.ShapeDtypeStruct((nb, s_len, wide), BF16)
    return pl.pallas_call(
        functools.partial(_mla_prep_kernel, q_lora=q_lora, kv_lora=kv_lora, n_heads=n_heads, half=d_rope // 2,
                          q_scale=q_scale),
        grid=(nb, s_len // ROW_TILE),
        in_specs=[tile(wp), tab, tab, full((1, q_lora)), full((1, kv_lora)), full((q_lora, wide)),
                  full((kv_lora, wide))],
        out_specs=[tile(wide), tile(wide), tile(wide)],
        out_shape=[out, out, out],
        compiler_params=_cparams(("parallel", "parallel")),
        name="mla_prep",
    )(pd, *tables, q_norm_g.reshape(1, q_lora), kv_norm_g.reshape(1, kv_lora), qw, kvw.astype(BF16))


def _moe_gather_kernel(src_ref, used_ref, tok_ref, o_ref, buf_ref, sem_ref):
    base = pl.program_id(0) * MOE_ROWS

    def copy(r):
        return pltpu.make_async_copy(tok_ref.at[pl.ds(src_ref[base + r], 1), :], buf_ref.at[pl.ds(r, 1), :],
                                     sem_ref.at[0])

    def start(r, carry):
        copy(r).start()
        return carry

    @pl.when(pl.program_id(0) < used_ref[0])
    def _():
        lax.fori_loop(0, MOE_ROWS, start, 0, unroll=16)
        pltpu.make_async_copy(tok_ref.at[pl.ds(0, MOE_ROWS), :], buf_ref, sem_ref.at[0]).wait()
        o_ref[...] = buf_ref[...].astype(o_ref.dtype)

    @pl.when(pl.program_id(0) >= used_ref[0])
    def _():
        o_ref[...] = jnp.zeros_like(o_ref)


def _used_block(i, used):
    return jnp.minimum(i, used[0] - 1)


def _moe_gather(tok, src, used, n_blocks):
    d = tok.shape[1]
    return pl.pallas_call(
        _moe_gather_kernel,
        grid_spec=pltpu.PrefetchScalarGridSpec(
            num_scalar_prefetch=2,
            grid=(n_blocks,),
            in_specs=[pl.BlockSpec(memory_space=pl.ANY)],
            out_specs=pl.BlockSpec((MOE_ROWS, d), lambda i, src, used: (i, 0)),
            scratch_shapes=[pltpu.VMEM((MOE_ROWS, d), F32), pltpu.SemaphoreType.DMA((1,))],
        ),
        out_shape=jax.ShapeDtypeStruct((n_blocks * MOE_ROWS, d), BF16),
        compiler_params=_cparams(("arbitrary",)),
        name="moe_gather",
    )(src, used, tok)


def _expert_changed(be_ref, i):
    return jnp.logical_or(i == 0, be_ref[i] != be_ref[jnp.maximum(i - 1, 0)])


def _moe_up_kernel(be_ref, used_ref, a_ref, wg_ref, wu_ref, o_ref, wg_bf, wu_bf):
    i = pl.program_id(1)

    @pl.when(i < used_ref[0])
    def _():
        @pl.when(_expert_changed(be_ref, i))
        def _():
            wg_bf[...] = wg_ref[...].astype(BF16)
            wu_bf[...] = wu_ref[...].astype(BF16)

        a = a_ref[...]
        g = jnp.dot(a, wg_bf[...], preferred_element_type=F32)
        u = jnp.dot(a, wu_bf[...], preferred_element_type=F32)
        o_ref[...] = (g * jax.nn.sigmoid(g) * u).astype(o_ref.dtype)

    @pl.when(i >= used_ref[0])
    def _():
        o_ref[...] = jnp.zeros_like(o_ref)


def _moe_block_maps():
    rows = lambda j, i, be, used: (_used_block(i, used), 0)
    weight = lambda j, i, be, used: (be[_used_block(i, used)], 0, j)
    out = lambda j, i, be, used: (i, j)
    return rows, weight, out


def _moe_up(buf, block_expert, used, wg, wu, tn_pref=512):
    rows, d = buf.shape
    f = wg.shape[2]
    tn = _pick(f, tn_pref)
    n_blocks = rows // MOE_ROWS
    row_map, w_map, out_map = _moe_block_maps()
    return pl.pallas_call(
        _moe_up_kernel,
        grid_spec=pltpu.PrefetchScalarGridSpec(
            num_scalar_prefetch=2,
            grid=(f // tn, n_blocks),
            in_specs=[
                pl.BlockSpec((MOE_ROWS, d), row_map),
                pl.BlockSpec((None, d, tn), w_map),
                pl.BlockSpec((None, d, tn), w_map),
            ],
            out_specs=pl.BlockSpec((MOE_ROWS, tn), out_map),
            scratch_shapes=[pltpu.VMEM((d, tn), BF16), pltpu.VMEM((d, tn), BF16)],
        ),
        out_shape=jax.ShapeDtypeStruct((rows, f), BF16),
        compiler_params=_cparams(("arbitrary", "arbitrary")),
        name="moe_up",
    )(block_expert, used, buf, wg, wu)


def _moe_down_kernel(be_ref, used_ref, h_ref, w_ref, o_ref, w_bf):
    i = pl.program_id(1)

    @pl.when(i < used_ref[0])
    def _():
        @pl.when(_expert_changed(be_ref, i))
        def _():
            w_bf[...] = w_ref[...].astype(BF16)

        o_ref[...] = jnp.dot(h_ref[...], w_bf[...], preferred_element_type=F32)

    @pl.when(i >= used_ref[0])
    def _():
        o_ref[...] = jnp.zeros_like(o_ref)


def _moe_down(hid, block_expert, used, wd, tn_pref=512):
    rows, f = hid.shape
    d = wd.shape[2]
    tn = _pick(d, tn_pref)
    n_blocks = rows // MOE_ROWS
    row_map, w_map, out_map = _moe_block_maps()
    return pl.pallas_call(
        _moe_down_kernel,
        grid_spec=pltpu.PrefetchScalarGridSpec(
            num_scalar_prefetch=2,
            grid=(d // tn, n_blocks),
            in_specs=[
                pl.BlockSpec((MOE_ROWS, f), row_map),
                pl.BlockSpec((None, f, tn), w_map),
            ],
            out_specs=pl.BlockSpec((MOE_ROWS, tn), out_map),
            scratch_shapes=[pltpu.VMEM((f, tn), BF16)],
        ),
        out_shape=jax.ShapeDtypeStruct((rows, d), F32),
        compiler_params=_cparams(("arbitrary", "arbitrary")),
        name="moe_down",
    )(block_expert, used, hid, wd)


def _moe_combine_kernel(dest_ref, y_ref, x_ref, gate_ref, g0_ref, g1_ref, o_ref, b0_ref, b1_ref, sem_ref):
    base = (pl.program_id(0) * pl.num_programs(1) + pl.program_id(1)) * ROW_TILE

    def copies(r):
        t = (base + r) * TOP_K
        return (pltpu.make_async_copy(y_ref.at[pl.ds(dest_ref[t], 1), :], b0_ref.at[pl.ds(r, 1), :], sem_ref.at[0]),
                pltpu.make_async_copy(y_ref.at[pl.ds(dest_ref[t + 1], 1), :], b1_ref.at[pl.ds(r, 1), :],
                                      sem_ref.at[1]))

    def start(r, carry):
        c0, c1 = copies(r)
        c0.start()
        c1.start()
        return carry

    lax.fori_loop(0, ROW_TILE, start, 0, unroll=16)
    pltpu.make_async_copy(y_ref.at[pl.ds(0, ROW_TILE), :], b0_ref, sem_ref.at[0]).wait()
    pltpu.make_async_copy(y_ref.at[pl.ds(0, ROW_TILE), :], b1_ref, sem_ref.at[1]).wait()
    o_ref[...] = x_ref[...] + gate_ref[...] * (g0_ref[...] * b0_ref[...] + g1_ref[...] * b1_ref[...])


def _moe_combine(y, dest, x, gate, g0, g1):
    nb, _, d = x.shape
    t = g0.shape[1]
    return pl.pallas_call(
        _moe_combine_kernel,
        grid_spec=pltpu.PrefetchScalarGridSpec(
            num_scalar_prefetch=1,
            grid=(nb, t // ROW_TILE),
            in_specs=[
                pl.BlockSpec(memory_space=pl.ANY),
                pl.BlockSpec((None, ROW_TILE, d), lambda b, i, dest: (b, i, 0)),
                pl.BlockSpec((None, 1, d), lambda b, i, dest: (b, 0, 0)),
                pl.BlockSpec((None, ROW_TILE, 1), lambda b, i, dest: (b, i, 0)),
                pl.BlockSpec((None, ROW_TILE, 1), lambda b, i, dest: (b, i, 0)),
            ],
            out_specs=pl.BlockSpec((None, ROW_TILE, d), lambda b, i, dest: (b, i, 0)),
            scratch_shapes=[pltpu.VMEM((ROW_TILE, d), F32), pltpu.VMEM((ROW_TILE, d), F32),
                            pltpu.SemaphoreType.DMA((2,))],
        ),
        out_shape=jax.ShapeDtypeStruct((nb, t, d), F32),
        compiler_params=_cparams(("arbitrary", "arbitrary")),
        name="moe_combine",
    )(dest, y, x, gate, g0, g1)


def _moe_layer(a_tok, logits, x, gate, wg, wu, wd):
    nb, t, d = a_tok.shape
    n_exp = wg.shape[0]
    n_tok = nb * t
    n_assign = n_tok * TOP_K
    top_logit, top_idx = lax.top_k(logits.reshape(n_tok, -1)[:, :n_exp], TOP_K)
    gates = jax.nn.softmax(top_logit, axis=-1)
    flat_e = top_idx.reshape(-1)
    onehot = (flat_e[:, None] == jnp.arange(n_exp)[None, :]).astype(jnp.int32)
    rank = jnp.take_along_axis(jnp.cumsum(onehot, axis=0) - onehot, flat_e[:, None], axis=1)[:, 0]
    counts = jnp.sum(onehot, axis=0)
    padded = (counts + MOE_ROWS - 1) // MOE_ROWS * MOE_ROWS
    pad_end = jnp.cumsum(padded)
    pad_start = pad_end - padded
    dest = (pad_start[flat_e] + rank).astype(jnp.int32)
    n_blocks = -(-n_assign // MOE_ROWS) + n_exp
    src = jnp.zeros((n_blocks * MOE_ROWS,), jnp.int32).at[dest].set(jnp.arange(n_assign, dtype=jnp.int32) // TOP_K)
    block_expert = jnp.minimum(
        jnp.searchsorted(pad_end, jnp.arange(n_blocks) * MOE_ROWS, side='right'), n_exp - 1).astype(jnp.int32)
    used = (pad_end[-1:] // MOE_ROWS).astype(jnp.int32)
    buf = _moe_gather(a_tok.reshape(n_tok, d), src, used, n_blocks)
    hid = _moe_up(buf, block_expert, used, wg, wu)
    y = _moe_down(hid, block_expert, used, wd)
    return _moe_combine(y, dest, x, gate, gates[:, 0].reshape(nb, t, 1), gates[:, 1].reshape(nb, t, 1))


def _rope_tables(n_lat, n_ctx, rot_dim):
    rows = n_lat // GRID_W
    row = jnp.repeat(jnp.arange(rows), GRID_W).astype(F32)
    col = jnp.tile(jnp.arange(GRID_W), rows).astype(F32)
    n_freq = rot_dim // 4
    inv = ROPE_THETA ** (-jnp.arange(n_freq, dtype=F32) / n_freq)
    ang = jnp.concatenate([row[:, None] * inv, col[:, None] * inv], axis=-1)
    ang = jnp.concatenate([ang, jnp.zeros((n_ctx, rot_dim // 2), F32)], axis=0)
    return jnp.cos(ang), jnp.sin(ang)


def _rwkv_mixer(pa, n_lat, mu, w0, w_up, a0, a_up, g_up, k_k, k_a, r_k, lnx_w, lnx_b):
    r, v, kk, wl, kd, bb, bonus, g = _rwkv_prep(pa, n_lat // ROW_TILE, mu, w0, w_up, a0, a_up, g_up, k_k, k_a,
                                                 r_k.reshape(-1))
    yf, yr = _rwkv_scan(r, v, kk, wl, kd, bb, n_lat)
    return _rwkv_out(yf, yr, bonus, g, lnx_w, lnx_b)


def _diff_mixer(pb, n_lat, lq1, lk1, lq2, lk2, subln_g, lambda_init, tables):
    dh = subln_g.shape[0] // 2
    nh = pb.shape[2] // 3 // (2 * dh)
    lam = jnp.exp(jnp.sum(lq1 * lk1)) - jnp.exp(jnp.sum(lq2 * lk2)) + lambda_init
    q, k, v1 = _diff_prep(pb, tables, dh // 2, dh ** -0.5 * LOG2E)
    scale = 1.0 - lambda_init
    tq = ATTN_ROWS // 2
    return _diff_attention(q, k, v1, lam, subln_g, nh, scale, tq, n_lat)


def _gqa_mixer(pc, n_lat, q_norm_g, k_norm_g, n_heads, n_kv, tables):
    dh = q_norm_g.shape[0]
    assert dh == LANES
    q, k, v1 = _gqa_prep(pc, tables, q_norm_g, k_norm_g, n_heads, n_kv, dh ** -0.5 * LOG2E)
    group = n_heads // n_kv
    tq = ATTN_ROWS // group
    return _attention(q, k, v1, n_kv, group, dh, dh, tq, n_lat)


def _mla_mixer(pd, n_lat, q_norm_g, q_up, kv_norm_g, kv_up, n_heads, d_rope, tables):
    d_qk = q_up.shape[1] // n_heads
    q, k, v1 = _mla_prep(pd, tables, q_norm_g, q_up, kv_norm_g, kv_up, n_heads, d_rope, d_qk ** -0.5 * LOG2E)
    dk, d_v = 2 * LANES, LANES
    return _attention(q, k, v1, n_heads, 1, dk, d_v, ATTN_ROWS, n_lat, MLA_KEYS)


def _pad_cols(w, mult):
    return jnp.pad(w, ((0, 0), (0, -w.shape[1] % mult)))


def kernel(x, c, ctx, c_ctx, ada_w, ada_b, norm_mix_g, norm_ffn_g, w_in, rwkv_mu, rwkv_w0, rwkv_w_up, rwkv_a0, rwkv_a_up, rwkv_g_up, rwkv_k_k, rwkv_k_a, rwkv_r_k, rwkv_lnx_w, rwkv_lnx_b, diff_lq1, diff_lk1, diff_lq2, diff_lk2, diff_subln_g, gqa_q_norm_g, gqa_k_norm_g, mla_q_norm_g, mla_q_up, mla_kv_norm_g, mla_kv_up, w_out, ffn_w_gate, ffn_w_up, ffn_w_down, moe_router, moe_w_gate, moe_w_up, moe_w_down, final_norm_g):
    nb, n_lat, d = x.shape
    n_ctx = ctx.shape[1]
    depth = ada_w.shape[0]
    assert n_ctx == ROW_TILE and n_lat % ROW_TILE == 0 and n_lat % GRID_W == 0
    n_lat_tiles = n_lat // ROW_TILE
    n_all_tiles = n_lat_tiles + 1

    gw = rwkv_k_k.shape[1]
    cols_a = 3 * gw + rwkv_w_up.shape[2] + rwkv_a_up.shape[2] + rwkv_g_up.shape[1]
    cols_b = 3 * gw
    gqa_dh = gqa_q_norm_g.shape[1]
    mla_rope = 64
    mla_heads = 8
    gqa_heads = gw // gqa_dh
    cols_d = mla_q_norm_g.shape[1] + mla_kv_norm_g.shape[1] + mla_rope
    cols_c = w_in.shape[2] - cols_a - cols_b - cols_d
    gqa_kv = (cols_c - gw) // (2 * gqa_dh)
    offs = (0, cols_a, cols_a + cols_b, cols_a + cols_b + cols_c, w_in.shape[2])

    rope_diff = _lane_tables(_rope_tables(n_lat, n_ctx, diff_subln_g.shape[1] // 2))
    rope_gqa = _lane_tables(_rope_tables(n_lat, n_ctx, gqa_dh))
    rope_mla = _lane_tables(_rope_tables(n_lat, n_ctx, mla_rope))

    cvec = jnp.concatenate([c, c_ctx[None, :], jnp.zeros((8 - nb - 1, d), F32)], axis=0)
    mods = _ada(cvec, ada_w, ada_b)[:, :nb + 1].reshape(depth, nb + 1, 6, 1, d)

    h = jnp.concatenate([x, ctx], axis=1)
    s_len = n_lat + n_ctx
    flat = lambda z: z.reshape(nb * s_len, z.shape[-1])
    for l in range(depth):
        need_ctx = l < depth - 1
        m = [mods[l, :, i] for i in range(6)]

        a = flat(_norm_mod(h, norm_mix_g[l], m[1], m[0], n_all_tiles, n_lat_tiles))
        groups = [_matmul(a, _pad_cols(w_in[l][:, offs[i]:offs[i + 1]], 768).astype(BF16), F32, 1536)
                  .reshape(nb, s_len, -1) for i in range(4)]
        oa = _rwkv_mixer(groups[0], n_lat, rwkv_mu[l], rwkv_w0[l], rwkv_w_up[l],
                         rwkv_a0[l], rwkv_a_up[l], rwkv_g_up[l], rwkv_k_k[l], rwkv_k_a[l], rwkv_r_k[l],
                         rwkv_lnx_w[l], rwkv_lnx_b[l])
        lambda_init = 0.8 - 0.6 * math.exp(-0.3 * l)
        ob = _diff_mixer(groups[1], n_lat, diff_lq1[l], diff_lk1[l], diff_lq2[l], diff_lk2[l],
                         diff_subln_g[l], lambda_init, rope_diff)
        oc = _gqa_mixer(groups[2], n_lat, gqa_q_norm_g[l], gqa_k_norm_g[l], gqa_heads, gqa_kv, rope_gqa)
        od = _mla_mixer(groups[3], n_lat, mla_q_norm_g[l], mla_q_up[l], mla_kv_norm_g[l], mla_kv_up[l],
                        mla_heads, mla_rope, rope_mla)
        h = _matmul_resid([flat(o) for o in (oa, ob, oc, od)], w_out[l].astype(BF16), flat(h), m[2],
                          n_all_tiles, tn_pref=1024).reshape(nb, s_len, d)

        j = l // 2
        if l % 2 == 0:
            d_ff = ffn_w_gate.shape[2]
            pad_f = -d_ff % 512
            w_g = jnp.pad(ffn_w_gate[j], ((0, 0), (0, pad_f))).astype(BF16)
            w_u = jnp.pad(ffn_w_up[j], ((0, 0), (0, pad_f))).astype(BF16)
            w_d = jnp.pad(ffn_w_down[j], ((0, pad_f), (0, 0))).astype(BF16)
            a = flat(_norm_mod(h, norm_ffn_g[l], m[4], m[3], n_all_tiles, n_lat_tiles))
            hid = _matmul_swiglu(a, w_g, w_u)
            h = _matmul_resid([hid], w_d, flat(h), m[5], n_all_tiles).reshape(nb, s_len, d)
        else:
            assert not need_ctx
            a, logits = _norm_router(h, norm_ffn_g[l], m[4], m[3], moe_router[j], n_lat_tiles, n_lat_tiles)
            h = _moe_layer(a, logits, h, m[5][:nb], moe_w_gate[j], moe_w_up[j], moe_w_down[j])
    return _final_norm(h, final_norm_g, n_lat_tiles)
```

```python
import functools
import itertools
import math

import jax
import jax.numpy as jnp
from jax import lax
from jax.experimental import pallas as pl
from jax.experimental.pallas import tpu as pltpu

F32 = jnp.float32
BF16 = jnp.bfloat16

GRID_W = 64
ROPE_THETA = 10000.0
NORM_EPS = 1e-6
SUBLN_EPS = 1e-5
GN_EPS = 64e-5
TOP_K = 2

LANES = 128
MXU_WIDTH = 256
ROW_TILE = 256
RWKV_HEAD = 64
RWKV_CHUNK = 64
RWKV_QUAD = 4
RWKV_STEP_QUADS = 4
MOE_ROWS = 512
MM_ROWS = 512
ATTN_KEYS = 1408
MLA_KEYS = 768
ATTN_ROWS = 1024
LOG2E = 1.4426950408889634
VMEM_LIMIT = 56 * 1024 * 1024


def _cparams(sem):
    return pltpu.CompilerParams(dimension_semantics=sem, vmem_limit_bytes=VMEM_LIMIT)


def _pick(n, pref):
    if n <= pref:
        return n
    t = pref - pref % LANES
    while t >= LANES:
        if n % t == 0:
            return t
        t -= LANES
    return n


def _ada_kernel(c_ref, w_ref, b_ref, o_ref):
    c = c_ref[...]
    s = (c * jax.nn.sigmoid(c)).astype(BF16)
    o_ref[...] = jnp.dot(s, w_ref[...].astype(BF16), preferred_element_type=F32) + b_ref[...]


def _ada(cvec, ada_w, ada_b):
    n_layers, d, n = ada_w.shape
    tn = _pick(n, 1024)
    return pl.pallas_call(
        _ada_kernel,
        grid=(n_layers, n // tn),
        in_specs=[
            pl.BlockSpec((8, d), lambda l, j: (0, 0)),
            pl.BlockSpec((None, d, tn), lambda l, j: (l, 0, j)),
            pl.BlockSpec((None, 1, tn), lambda l, j: (l, 0, j)),
        ],
        out_specs=pl.BlockSpec((None, 8, tn), lambda l, j: (l, 0, j)),
        out_shape=jax.ShapeDtypeStruct((n_layers, 8, n), F32),
        compiler_params=_cparams(("parallel", "parallel")),
        name="ada",
    )(cvec, ada_w, ada_b.reshape(n_layers, 1, n))


def _mod_sel(n_lat_tiles, n_batch):
    return lambda b, i: (jnp.where(i >= n_lat_tiles, n_batch, b), 0, 0)


def _norm_mod_kernel(x_ref, g_ref, sc_ref, sh_ref, o_ref):
    x = x_ref[...]
    y = x * lax.rsqrt(jnp.mean(x * x, axis=-1, keepdims=True) + NORM_EPS) * g_ref[...]
    o_ref[...] = (y * (1.0 + sc_ref[...]) + sh_ref[...]).astype(o_ref.dtype)


def _norm_mod(x, g, scale, shift, n_tiles, n_lat_tiles, out_dtype=BF16):
    nb, _, d = x.shape
    sel = _mod_sel(n_lat_tiles, nb)
    return pl.pallas_call(
        _norm_mod_kernel,
        grid=(nb, n_tiles),
        in_specs=[
            pl.BlockSpec((None, ROW_TILE, d), lambda b, i: (b, i, 0)),
            pl.BlockSpec((1, d), lambda b, i: (0, 0)),
            pl.BlockSpec((None, 1, d), sel),
            pl.BlockSpec((None, 1, d), sel),
        ],
        out_specs=pl.BlockSpec((None, ROW_TILE, d), lambda b, i: (b, i, 0)),
        out_shape=jax.ShapeDtypeStruct((nb, n_tiles * ROW_TILE, d), out_dtype),
        compiler_params=_cparams(("parallel", "parallel")),
        name="norm_mod",
    )(x, g.reshape(1, d), scale, shift)


def _norm_router_kernel(x_ref, g_ref, sc_ref, sh_ref, rh_ref, rl_ref, o_ref, lg_ref):
    x = x_ref[...]
    y = x * lax.rsqrt(jnp.mean(x * x, axis=-1, keepdims=True) + NORM_EPS) * g_ref[...]
    a = y * (1.0 + sc_ref[...]) + sh_ref[...]
    o_ref[...] = a
    ah = a.astype(BF16)
    al = (a - ah.astype(F32)).astype(BF16)
    rh = rh_ref[...]
    lg_ref[...] = (jnp.dot(ah, rh, preferred_element_type=F32)
                   + jnp.dot(al, rh, preferred_element_type=F32)
                   + jnp.dot(ah, rl_ref[...], preferred_element_type=F32))


def _norm_router(x, g, scale, shift, router, n_tiles, n_lat_tiles):
    nb, _, d = x.shape
    n_exp = router.shape[1]
    r_pad = jnp.pad(router, ((0, 0), (0, LANES - n_exp)))
    r_hi = r_pad.astype(BF16)
    r_lo = (r_pad - r_hi.astype(F32)).astype(BF16)
    sel = _mod_sel(n_lat_tiles, nb)
    rows = n_tiles * ROW_TILE
    return pl.pallas_call(
        _norm_router_kernel,
        grid=(nb, n_tiles),
        in_specs=[
            pl.BlockSpec((None, ROW_TILE, d), lambda b, i: (b, i, 0)),
            pl.BlockSpec((1, d), lambda b, i: (0, 0)),
            pl.BlockSpec((None, 1, d), sel),
            pl.BlockSpec((None, 1, d), sel),
            pl.BlockSpec((d, LANES), lambda b, i: (0, 0)),
            pl.BlockSpec((d, LANES), lambda b, i: (0, 0)),
        ],
        out_specs=[
            pl.BlockSpec((None, ROW_TILE, d), lambda b, i: (b, i, 0)),
            pl.BlockSpec((None, ROW_TILE, LANES), lambda b, i: (b, i, 0)),
        ],
        out_shape=[
            jax.ShapeDtypeStruct((nb, rows, d), F32),
            jax.ShapeDtypeStruct((nb, rows, LANES), F32),
        ],
        compiler_params=_cparams(("parallel", "parallel")),
        name="norm_router",
    )(x, g.reshape(1, d), scale, shift, r_hi, r_lo)


def _final_norm_kernel(x_ref, g_ref, o_ref):
    x = x_ref[...]
    o_ref[...] = x * lax.rsqrt(jnp.mean(x * x, axis=-1, keepdims=True) + NORM_EPS) * g_ref[...]


def _final_norm(x, g, n_tiles):
    nb, _, d = x.shape
    rows = n_tiles * ROW_TILE
    return pl.pallas_call(
        _final_norm_kernel,
        grid=(nb, n_tiles),
        in_specs=[
            pl.BlockSpec((None, ROW_TILE, d), lambda b, i: (b, i, 0)),
            pl.BlockSpec((1, d), lambda b, i: (0, 0)),
        ],
        out_specs=pl.BlockSpec((None, ROW_TILE, d), lambda b, i: (b, i, 0)),
        out_shape=jax.ShapeDtypeStruct((nb, rows, d), F32),
        compiler_params=_cparams(("parallel", "parallel")),
        name="final_norm",
    )(x, g.reshape(1, d))


def _row_tile(rows):
    return MM_ROWS if rows % MM_ROWS == 0 else ROW_TILE


def _mm_kernel(a_ref, w_ref, o_ref):
    o_ref[...] = jnp.dot(a_ref[...], w_ref[...], preferred_element_type=F32).astype(o_ref.dtype)


def _matmul(a, w, out_dtype=F32, tn_pref=1024):
    rows, k = a.shape
    n = w.shape[1]
    tn = _pick(n, tn_pref)
    tm = _row_tile(rows)
    return pl.pallas_call(
        _mm_kernel,
        grid=(n // tn, rows // tm),
        in_specs=[
            pl.BlockSpec((tm, k), lambda j, i: (i, 0)),
            pl.BlockSpec((k, tn), lambda j, i: (0, j)),
        ],
        out_specs=pl.BlockSpec((tm, tn), lambda j, i: (i, j)),
        out_shape=jax.ShapeDtypeStruct((rows, n), out_dtype),
        compiler_params=_cparams(("parallel", "parallel")),
        name="matmul",
    )(a, w)


def _mm_swiglu_kernel(a_ref, wg_ref, wu_ref, o_ref):
    a = a_ref[...]
    g = jnp.dot(a, wg_ref[...], preferred_element_type=F32)
    u = jnp.dot(a, wu_ref[...], preferred_element_type=F32)
    o_ref[...] = (g * jax.nn.sigmoid(g) * u).astype(o_ref.dtype)


def _matmul_swiglu(a, wg, wu, tn_pref=1024):
    rows, k = a.shape
    n = wg.shape[1]
    tn = _pick(n, tn_pref)
    tm = _row_tile(rows)
    return pl.pallas_call(
        _mm_swiglu_kernel,
        grid=(n // tn, rows // tm),
        in_specs=[
            pl.BlockSpec((tm, k), lambda j, i: (i, 0)),
            pl.BlockSpec((k, tn), lambda j, i: (0, j)),
            pl.BlockSpec((k, tn), lambda j, i: (0, j)),
        ],
        out_specs=pl.BlockSpec((tm, tn), lambda j, i: (i, j)),
        out_shape=jax.ShapeDtypeStruct((rows, n), BF16),
        compiler_params=_cparams(("parallel", "parallel")),
        name="matmul_swiglu",
    )(a, wg, wu)


def _mm_resid_kernel(*refs, n_parts, n_gates):
    a_refs, w_refs = refs[:n_parts], refs[n_parts:2 * n_parts]
    res_ref = refs[2 * n_parts]
    gate_refs = refs[2 * n_parts + 1:2 * n_parts + 1 + n_gates]
    o_ref = refs[-1]
    acc = jnp.dot(a_refs[0][...], w_refs[0][...], preferred_element_type=F32)
    for a_ref, w_ref in zip(a_refs[1:], w_refs[1:]):
        acc = acc + jnp.dot(a_ref[...], w_ref[...], preferred_element_type=F32)
    rows = acc.shape[0] // n_gates
    for t, gate_ref in enumerate(gate_refs):
        sl = slice(t * rows, (t + 1) * rows)
        o_ref[sl, :] = res_ref[sl, :] + gate_ref[...] * acc[sl, :]


def _matmul_resid(parts, w, res, gate, tiles_per_batch, tm=None, tn_pref=512):
    rows, kg = parts[0].shape
    n_parts = len(parts)
    n = w.shape[1]
    nb = gate.shape[0] - 1
    tn = _pick(n, tn_pref)
    tm = tm or _row_tile(rows)
    n_gates = tm // ROW_TILE

    def gate_spec(t):
        def index(j, i):
            tile = i * n_gates + t
            pos = tile % tiles_per_batch
            return (jnp.where(pos == tiles_per_batch - 1, nb, tile // tiles_per_batch), 0, j)
        return pl.BlockSpec((None, 1, tn), index)

    return pl.pallas_call(
        functools.partial(_mm_resid_kernel, n_parts=n_parts, n_gates=n_gates),
        grid=(n // tn, rows // tm),
        in_specs=([pl.BlockSpec((tm, kg), lambda j, i: (i, 0))] * n_parts
                  + [pl.BlockSpec((kg, tn), lambda j, i, g=g: (g, j)) for g in range(n_parts)]
                  + [pl.BlockSpec((tm, tn), lambda j, i: (i, j))]
                  + [gate_spec(t) for t in range(n_gates)]),
        out_specs=pl.BlockSpec((tm, tn), lambda j, i: (i, j)),
        out_shape=jax.ShapeDtypeStruct((rows, n), F32),
        compiler_params=_cparams(("parallel", "parallel")),
        name="matmul_resid",
    )(*parts, *([w] * n_parts), res, *([gate] * n_gates))


_NT = (((1,), (1,)), ((), ()))


def _online_softmax(qs, k_ref, v_ref, ck):
    n_rows = qs.shape[0]
    m = jnp.full((n_rows, 1), -jnp.inf, F32)
    acc = jnp.zeros((n_rows, v_ref.shape[1]), F32)
    for c in range(k_ref.shape[0] // ck):
        keys = slice(c * ck, (c + 1) * ck)
        s = lax.dot_general(qs, k_ref[keys, :], _NT, preferred_element_type=F32)
        m_new = jnp.maximum(m, jnp.max(s, axis=-1, keepdims=True))
        p = jnp.exp2(s - m_new).astype(BF16)
        acc = jnp.exp2(m - m_new) * acc + jnp.dot(p, v_ref[keys, :], preferred_element_type=F32)
        m = m_new
    return acc


def _attend_rows(body, pre_args, pre_specs, q, k, v1, n_heads, qw, dk, ow, tq, n_lat, name, keys=ATTN_KEYS):
    nb, s_len, _ = q.shape
    n_ctx = s_len - n_lat
    dvp = v1.shape[2] // n_heads
    out_shape = jax.ShapeDtypeStruct((nb, s_len, n_heads * ow), BF16)

    def run(tq, n_q, q_off, k_rows, k_off, prev):
        kern = functools.partial(body, ck=_pick(k_rows, keys))
        specs = pre_specs + [
            pl.BlockSpec((None, tq, qw), lambda b, h, i: (b, i + q_off, h)),
            pl.BlockSpec((None, k_rows, dk), lambda b, h, i: (b, k_off, h)),
            pl.BlockSpec((None, k_rows, dvp), lambda b, h, i: (b, k_off, h)),
        ]
        args = pre_args + [q, k, v1, prev]
        return pl.pallas_call(
            lambda *refs: kern(*refs[:-2], refs[-1]),
            grid=(nb, n_heads, n_q),
            in_specs=specs + [pl.BlockSpec(memory_space=pl.ANY)],
            out_specs=pl.BlockSpec((None, tq, ow), lambda b, h, i: (b, i + q_off, h)),
            out_shape=out_shape,
            input_output_aliases={len(args) - 1: 0},
            compiler_params=_cparams(("parallel", "parallel", "parallel")),
            name=name,
        )(*args)

    tq = math.gcd(tq, n_lat)
    out = run(tq, n_lat // tq, 0, s_len, 0, jnp.zeros(out_shape.shape, out_shape.dtype))
    return run(n_ctx, 1, n_lat // n_ctx, n_ctx, n_lat // n_ctx, out)


def _attn_kernel(q_ref, k_ref, v_ref, o_ref, *, pieces, dv, ck):
    q = q_ref[...]
    tq, dk = q.shape[0], q.shape[1] // pieces
    qs = q if pieces == 1 else jnp.concatenate([q[:, g * dk:(g + 1) * dk] for g in range(pieces)], axis=0)
    acc = _online_softmax(qs, k_ref, v_ref, ck)
    o = acc[:, :dv] / acc[:, dv:dv + 1]
    for g in range(pieces):
        o_ref[:, g * dv:(g + 1) * dv] = o[g * tq:(g + 1) * tq].astype(o_ref.dtype)


def _attention(q, k, v1, n_kv_heads, group, dk, dv, tq, n_lat, keys=ATTN_KEYS):
    return _attend_rows(functools.partial(_attn_kernel, pieces=group, dv=dv), [], [], q, k, v1, n_kv_heads,
                        group * dk, dk, group * dv, tq, n_lat, "attention", keys)


def _diff_attn_kernel(lam_ref, g_ref, q_ref, k_ref, v_ref, o_ref, *, dv, ck, out_scale):
    q = q_ref[...]
    tq, width = q.shape
    first = lax.broadcasted_iota(jnp.int32, (tq, width), 1) < width // 2
    zero = jnp.zeros_like(q)
    qs = jnp.concatenate([jnp.where(first, q, zero), jnp.where(first, zero, q)], axis=0)
    acc = _online_softmax(qs, k_ref, v_ref, ck)
    o = acc[:, :dv] / acc[:, dv:dv + 1]
    o = o[:tq] - lam_ref[0] * o[tq:]
    o = o * lax.rsqrt(jnp.mean(o * o, axis=-1, keepdims=True) + SUBLN_EPS) * g_ref[...]
    o_ref[...] = (o * out_scale).astype(o_ref.dtype)


def _diff_attention(q, k, v1, lam, subln_g, n_heads, out_scale, tq, n_lat):
    dh = q.shape[2] // n_heads
    return _attend_rows(functools.partial(_diff_attn_kernel, dv=dh, out_scale=out_scale),
                        [lam.reshape(1).astype(F32), subln_g.reshape(1, dh)],
                        [pl.BlockSpec(memory_space=pltpu.SMEM), pl.BlockSpec((1, dh), lambda b, h, i: (0, 0))],
                        q, k, v1, n_heads, dh, dh, dh, tq, n_lat, "diff_attention")


def _rwkv_chunk(r, v, kk, wl, kd, bb, h_state, rev, out):
    c, w = r.shape
    nh = w // RWKV_HEAD
    assert c == RWKV_HEAD
    tr = lax.broadcasted_iota(jnp.int32, (c, w), 0)
    cum = wl
    step = 1
    while step < c:
        shifted = pltpu.roll(cum, c - step if rev else step, axis=0)
        cum = cum + jnp.where((tr < c - step) if rev else (tr >= step), shifted, 0.0)
        step *= 2
    cum_prev = cum - wl
    last = 0 if rev else c - 1
    total = cum[last:last + 1, :]
    mid = cum[c // 2:c // 2 + 1, :]
    a = -kk
    e_mc = jnp.exp(mid - cum)
    e_end = jnp.exp(total - cum)

    lane = lax.broadcasted_iota(jnp.int32, (c, w), 1)
    lane_head, ts = lane // c, lane % c
    strict = (ts > tr) if rev else (ts < tr)
    incl = (ts >= tr) if rev else (ts <= tr)

    def stack(x):
        return jnp.concatenate([jnp.where(lane_head == h, x, 0.0) for h in range(nh)], axis=0).astype(BF16)

    def dot_nt(x, y):
        return lax.dot_general(x.astype(BF16), y.astype(BF16), _NT, preferred_element_type=F32)

    def dot_tn(x, y):
        return lax.dot_general(x.astype(BF16), y.astype(BF16), (((0,), (0,)), ((), ())), preferred_element_type=F32)

    def dot(x, y):
        return jnp.dot(x.astype(BF16), y.astype(BF16), preferred_element_type=F32)

    lhs = jnp.concatenate([a * jnp.exp(cum_prev - mid), r * jnp.exp(cum - mid)], axis=0)
    g_b = dot_nt(lhs, stack(bb * e_mc))
    g_k = dot_nt(lhs, stack(kd * e_mc))
    n_ab = jnp.where(strict, g_b[:c], 0.0)
    a_rb = jnp.where(incl, g_b[c:], 0.0)
    a_ak = jnp.where(strict, g_k[:c], 0.0)
    a_rk = jnp.where(incl, g_k[c:], 0.0)
    yield

    t_inv = jnp.where(ts == tr, 1.0, 0.0) + n_ab
    pw = n_ab
    for _ in range(int(math.log2(c)) - 1):
        pw = dot(pw, stack(pw))
        yield
        t_inv = t_inv + dot(pw, stack(t_inv))
    yield

    v_st = stack(v)
    x = dot(a * jnp.exp(cum_prev), h_state) + dot(a_ak, v_st)
    yield
    u = dot(t_inv, stack(x))
    yield
    y = dot(r * jnp.exp(cum), h_state) + dot(a_rb, stack(u)) + dot(a_rk, v_st)
    rw = lax.broadcasted_iota(jnp.int32, (w, w), 0)
    cw = lax.broadcasted_iota(jnp.int32, (w, w), 1)
    decay_end = jnp.where(rw == cw, jnp.broadcast_to(jnp.exp(total), (w, w)), 0.0)
    update = dot_tn(jnp.concatenate([bb * e_end, kd * e_end], axis=0), jnp.concatenate([u, v], axis=0))
    out.append((y, dot(decay_end, h_state) + jnp.where(rw // RWKV_HEAD == cw // RWKV_HEAD, update, 0.0)))


def _rwkv_scan_kernel(rf_ref, vf_ref, kkf_ref, wlf_ref, kdf_ref, bbf_ref,
                      rr_ref, vr_ref, kkr_ref, wlr_ref, kdr_ref, bbr_ref,
                      yf_ref, yr_ref, hf_ref, hr_ref):
    @pl.when(pl.program_id(2) == 0)
    def _():
        hf_ref[...] = jnp.zeros_like(hf_ref)
        hr_ref[...] = jnp.zeros_like(hr_ref)

    qw = RWKV_QUAD * RWKV_HEAD
    chains = []
    for i in range(hf_ref.shape[0]):
        sl = slice(i * qw, (i + 1) * qw)
        for refs, y_ref, h_ref, rev in (((rf_ref, vf_ref, kkf_ref, wlf_ref, kdf_ref, bbf_ref), yf_ref, hf_ref, False),
                                        ((rr_ref, vr_ref, kkr_ref, wlr_ref, kdr_ref, bbr_ref), yr_ref, hr_ref, True)):
            out = []
            chains.append((_rwkv_chunk(*(ref[:, sl] for ref in refs), h_ref[i], rev, out), out, y_ref, h_ref, i, sl))
    for _ in itertools.zip_longest(*(chain[0] for chain in chains)):
        pass
    for _, out, y_ref, h_ref, i, sl in chains:
        y_ref[:, sl], h_ref[i] = out[0]


def _rwkv_scan(r, v, kk, wl, kd, bb, n_lat):
    nb, s_len, w = r.shape
    c = RWKV_CHUNK
    qw = RWKV_QUAD * RWKV_HEAD
    nq = RWKV_STEP_QUADS if w % (RWKV_STEP_QUADS * qw) == 0 else 1
    bw = nq * qw
    n_l, n_all = n_lat // c, s_len // c
    n_c = n_all - n_l

    def fwd(s):
        return jnp.where(s < n_c, n_l + s, s - n_c)

    def rev(s):
        return n_all - 1 - s

    def shared(order):
        return pl.BlockSpec((None, c, bw), lambda b, q, s: (b, order(s), q))

    def per_dir(d, order):
        return pl.BlockSpec((None, None, c, bw), lambda b, q, s: (d, b, order(s), q))

    yf, yr = pl.pallas_call(
        _rwkv_scan_kernel,
        grid=(nb, w // bw, n_all),
        in_specs=[shared(fwd), shared(fwd), shared(fwd), per_dir(0, fwd), per_dir(0, fwd), per_dir(0, fwd),
                  shared(rev), shared(rev), shared(rev), per_dir(1, rev), per_dir(1, rev), per_dir(1, rev)],
        out_specs=[shared(fwd), shared(rev)],
        out_shape=[jax.ShapeDtypeStruct((nb, s_len, w), F32)] * 2,
        scratch_shapes=[pltpu.VMEM((nq, qw, qw), F32), pltpu.VMEM((nq, qw, qw), F32)],
        compiler_params=_cparams(("parallel", "parallel", "arbitrary")),
        name="rwkv_scan",
    )(r, v, kk, wl, kd, bb, r, v, kk, wl, kd, bb)
    return yf, yr


def _group_sum(x, ones_bd):
    hi = x.astype(BF16)
    lo = (x - hi.astype(F32)).astype(BF16)
    wb = ones_bd.shape[0]
    return jnp.concatenate(
        [jnp.dot(hi[:, b:b + wb], ones_bd, preferred_element_type=F32)
         + jnp.dot(lo[:, b:b + wb], ones_bd, preferred_element_type=F32) for b in range(0, x.shape[1], wb)], axis=1)


def _rwkv_prep_kernel(x_ref, prev_ref, next_ref, mu_ref, kkg_ref, ka_ref, rk_ref, w0_ref, a0_ref, wup_ref, aup_ref,
                      gup_ref, ones_ref, r_ref, v_ref, kk_ref, wl_ref, kd_ref, bb_ref, bonus_ref, g_ref,
                      *, n_lat_tiles, gw, lora_w, lora_a):
    i = pl.program_id(1)
    x = x_ref[...]
    rows = x.shape[0]
    row = lax.broadcasted_iota(jnp.int32, x.shape, 0)
    has_prev = jnp.logical_and(i > 0, i < n_lat_tiles)
    has_next = i < n_lat_tiles - 1
    prev_row = jnp.where(has_prev, prev_ref[7:8, :], 0.0)
    next_row = jnp.where(has_next, next_ref[0:1, :], 0.0)
    prev = jnp.where(row == 0, prev_row, pltpu.roll(x, 1, axis=0))
    nxt = jnp.where(row == rows - 1, next_row, pltpu.roll(x, rows - 1, axis=0))
    ps = x + mu_ref[...] * (0.5 * (prev + nxt) - x)
    r, k, v = ps[:, :gw], ps[:, gw:2 * gw], ps[:, 2 * gw:3 * gw]
    o = 3 * gw
    wd, ad, gd = ps[:, o:o + lora_w], ps[:, o + lora_w:o + lora_w + lora_a], ps[:, o + lora_w + lora_a:]
    ones_bd = ones_ref[...]
    kk = k * kkg_ref[...]
    kk = kk * lax.rsqrt(jnp.maximum(_group_sum(kk * kk, ones_bd), 1e-24))
    tanh_wd = jnp.tanh(wd).astype(BF16)
    ad_b = ad.astype(BF16)
    kd_sum = jnp.zeros_like(k)
    for d in range(2):
        z = w0_ref[d] + jnp.dot(tanh_wd, wup_ref[d], preferred_element_type=F32)
        softplus_neg = jnp.maximum(-z, 0.0) + jnp.log1p(jnp.exp(-jnp.abs(z)))
        wl_ref[d] = -jnp.exp(-softplus_neg - 0.5)
        a_sig = jax.nn.sigmoid(a0_ref[d] + jnp.dot(ad_b, aup_ref[d], preferred_element_type=F32))
        kd = k * (1.0 + (a_sig - 1.0) * ka_ref[...])
        kd_ref[d] = kd
        bb_ref[d] = kk * a_sig
        kd_sum = kd_sum + kd
    r_ref[...] = r
    v_ref[...] = v
    kk_ref[...] = kk
    bonus_ref[...] = _group_sum(r * kd_sum * rk_ref[...], ones_bd) * v
    g_ref[...] = jnp.dot(jax.nn.sigmoid(gd).astype(BF16), gup_ref[...], preferred_element_type=F32)


def _head_ones(width, head):
    idx = jnp.arange(width) // head
    return (idx[:, None] == idx[None, :]).astype(BF16)


def _rwkv_prep(pa, n_lat_tiles, mu, w0, w_up, a0, a_up, g_up, k_k, k_a, r_k):
    nb, s_len, wp = pa.shape
    gw = k_k.shape[0]
    lora_w, lora_a = w_up.shape[1], a_up.shape[1]
    lora_g = wp - 3 * gw - lora_w - lora_a
    n_tiles = s_len // ROW_TILE
    sub = ROW_TILE // 8
    row = lambda z: z.reshape(1, -1)
    full = lambda shape: pl.BlockSpec(shape, lambda b, i: (0,) * len(shape))
    tile = lambda w: pl.BlockSpec((None, ROW_TILE, w), lambda b, i: (b, i, 0))
    tile2 = pl.BlockSpec((2, None, ROW_TILE, gw), lambda b, i: (0, b, i, 0))
    one = jax.ShapeDtypeStruct((nb, s_len, gw), F32)
    two = jax.ShapeDtypeStruct((2, nb, s_len, gw), F32)
    return pl.pallas_call(
        functools.partial(_rwkv_prep_kernel, n_lat_tiles=n_lat_tiles, gw=gw, lora_w=lora_w, lora_a=lora_a),
        grid=(nb, n_tiles),
        in_specs=[
            tile(wp),
            pl.BlockSpec((None, 8, wp), lambda b, i: (b, jnp.maximum(i * sub - 1, 0), 0)),
            pl.BlockSpec((None, 8, wp), lambda b, i: (b, jnp.minimum((i + 1) * sub, s_len // 8 - 1), 0)),
            full((1, wp)), full((1, gw)), full((1, gw)), full((1, gw)),
            full((2, 1, gw)), full((2, 1, gw)),
            full((2, lora_w, gw)), full((2, lora_a, gw)), full((lora_g, gw)), full((MXU_WIDTH, MXU_WIDTH)),
        ],
        out_specs=[tile(gw), tile(gw), tile(gw), tile2, tile2, tile2, tile(gw), tile(gw)],
        out_shape=[one, one, one, two, two, two, one, one],
        compiler_params=_cparams(("parallel", "parallel")),
        name="rwkv_prep",
    )(pa, pa, pa, row(jnp.pad(mu, (0, wp - mu.shape[0]))), row(k_k), row(k_a), row(r_k),
      w0.reshape(2, 1, gw), a0.reshape(2, 1, gw), w_up.astype(BF16), a_up.astype(BF16),
      jnp.pad(g_up, ((0, lora_g - g_up.shape[0]), (0, 0))).astype(BF16), _head_ones(MXU_WIDTH, RWKV_HEAD))


def _rwkv_out_kernel(yf_ref, yr_ref, bonus_ref, g_ref, lw_ref, lb_ref, ones_ref, o_ref):
    ones_bd = ones_ref[...]
    y = yf_ref[...] + yr_ref[...]
    inv_n = 1.0 / RWKV_HEAD
    dev = y - _group_sum(y, ones_bd) * inv_n
    var = _group_sum(dev * dev, ones_bd) * inv_n
    yn = dev * lax.rsqrt(var + GN_EPS) * lw_ref[...] + lb_ref[...]
    o_ref[...] = ((yn + bonus_ref[...]) * g_ref[...]).astype(o_ref.dtype)


def _rwkv_out(yf, yr, bonus, g, lnx_w, lnx_b):
    nb, s_len, gw = yf.shape
    tile = pl.BlockSpec((None, ROW_TILE, gw), lambda b, i: (b, i, 0))
    vec = pl.BlockSpec((1, gw), lambda b, i: (0, 0))
    return pl.pallas_call(
        _rwkv_out_kernel,
        grid=(nb, s_len // ROW_TILE),
        in_specs=[tile, tile, tile, tile, vec, vec, pl.BlockSpec((MXU_WIDTH, MXU_WIDTH), lambda b, i: (0, 0))],
        out_specs=tile,
        out_shape=jax.ShapeDtypeStruct((nb, s_len, gw), BF16),
        compiler_params=_cparams(("parallel", "parallel")),
        name="rwkv_out",
    )(yf, yr, bonus, g, lnx_w.reshape(1, gw), lnx_b.reshape(1, gw), _head_ones(MXU_WIDTH, RWKV_HEAD))


def _rope_block(xb, cos_t, sin_t, half):
    if 2 * half == LANES:
        partner = pltpu.roll(xb, half, axis=1)
    else:
        lane = lax.broadcasted_iota(jnp.int32, xb.shape, 1)
        partner = jnp.where(lane % (2 * half) < half, pltpu.roll(xb, LANES - half, axis=1),
                            pltpu.roll(xb, half, axis=1))
    return xb * cos_t + partner * sin_t


def _lane_tables(cs):
    cos, sin = cs
    reps = LANES // (2 * cos.shape[1])
    return (jnp.tile(jnp.concatenate([cos, cos], axis=1), (1, reps)),
            jnp.tile(jnp.concatenate([-sin, sin], axis=1), (1, reps)))


def _ones_column(rows, dtype):
    return jnp.where(lax.broadcasted_iota(jnp.int32, (rows, LANES), 1) == 0, 1.0, 0.0).astype(dtype)


def _blk(b):
    return slice(b * LANES, (b + 1) * LANES)


def _diff_prep_kernel(x_ref, cos_ref, sin_ref, q_ref, k_ref, v_ref, *, gw, half, q_scale):
    cos_t, sin_t = cos_ref[...], sin_ref[...]
    e0 = _ones_column(x_ref.shape[0], v_ref.dtype)
    nblk = gw // LANES
    for b in range(nblk):
        q_ref[:, _blk(b)] = (_rope_block(x_ref[:, _blk(b)], cos_t, sin_t, half) * q_scale).astype(q_ref.dtype)
        k_ref[:, _blk(b)] = _rope_block(x_ref[:, _blk(nblk + b)], cos_t, sin_t, half).astype(k_ref.dtype)
        v_ref[:, _blk(2 * b)] = x_ref[:, _blk(2 * nblk + b)].astype(v_ref.dtype)
        v_ref[:, _blk(2 * b + 1)] = e0


def _diff_prep(pb, tables, half, q_scale):
    nb, s_len, w3 = pb.shape
    gw = w3 // 3
    tile = lambda w: pl.BlockSpec((None, ROW_TILE, w), lambda b, i: (b, i, 0))
    tab = pl.BlockSpec((ROW_TILE, LANES), lambda b, i: (i, 0))
    return pl.pallas_call(
        functools.partial(_diff_prep_kernel, gw=gw, half=half, q_scale=q_scale),
        grid=(nb, s_len // ROW_TILE),
        in_specs=[tile(w3), tab, tab],
        out_specs=[tile(gw), tile(gw), tile(2 * gw)],
        out_shape=[jax.ShapeDtypeStruct((nb, s_len, gw), BF16), jax.ShapeDtypeStruct((nb, s_len, gw), BF16),
                   jax.ShapeDtypeStruct((nb, s_len, 2 * gw), BF16)],
        compiler_params=_cparams(("parallel", "parallel")),
        name="diff_prep",
    )(pb, *tables)


def _gqa_prep_kernel(x_ref, cos_ref, sin_ref, qg_ref, kg_ref, q_ref, k_ref, v_ref, *, n_heads, n_kv, q_scale):
    cos_t, sin_t = cos_ref[...], sin_ref[...]
    e0 = _ones_column(x_ref.shape[0], v_ref.dtype)

    def normed_rope(xb, g):
        xb = xb * lax.rsqrt(jnp.mean(xb * xb, axis=-1, keepdims=True) + NORM_EPS) * g
        return _rope_block(xb, cos_t, sin_t, LANES // 2)

    for b in range(n_heads):
        q_ref[:, _blk(b)] = (normed_rope(x_ref[:, _blk(b)], qg_ref[...]) * q_scale).astype(q_ref.dtype)
    for b in range(n_kv):
        k_ref[:, _blk(b)] = normed_rope(x_ref[:, _blk(n_heads + b)], kg_ref[...]).astype(k_ref.dtype)
        v_ref[:, _blk(2 * b)] = x_ref[:, _blk(n_heads + n_kv + b)].astype(v_ref.dtype)
        v_ref[:, _blk(2 * b + 1)] = e0


def _gqa_prep(pc, tables, q_norm_g, k_norm_g, n_heads, n_kv, q_scale):
    nb, s_len, w = pc.shape
    tile = lambda w_: pl.BlockSpec((None, ROW_TILE, w_), lambda b, i: (b, i, 0))
    tab = pl.BlockSpec((ROW_TILE, LANES), lambda b, i: (i, 0))
    vec = pl.BlockSpec((1, LANES), lambda b, i: (0, 0))
    return pl.pallas_call(
        functools.partial(_gqa_prep_kernel, n_heads=n_heads, n_kv=n_kv, q_scale=q_scale),
        grid=(nb, s_len // ROW_TILE),
        in_specs=[tile(w), tab, tab, vec, vec],
        out_specs=[tile(n_heads * LANES), tile(n_kv * LANES), tile(2 * n_kv * LANES)],
        out_shape=[jax.ShapeDtypeStruct((nb, s_len, n_heads * LANES), BF16),
                   jax.ShapeDtypeStruct((nb, s_len, n_kv * LANES), BF16),
                   jax.ShapeDtypeStruct((nb, s_len, 2 * n_kv * LANES), BF16)],
        compiler_params=_cparams(("parallel", "parallel")),
        name="gqa_prep",
    )(pc, *tables, q_norm_g.reshape(1, LANES), k_norm_g.reshape(1, LANES))


def _mla_prep_kernel(x_ref, cos_ref, sin_ref, qg_ref, kvg_ref, qup_ref, kvup_ref, q_ref, k_ref, v_ref,
                     *, q_lora, kv_lora, n_heads, half, q_scale):
    cos_t, sin_t = cos_ref[...], sin_ref[...]
    e0 = _ones_column(x_ref.shape[0], v_ref.dtype)

    def normed(z, g):
        return (z * lax.rsqrt(jnp.mean(z * z, axis=-1, keepdims=True) + NORM_EPS) * g).astype(BF16)

    q = jnp.dot(normed(x_ref[:, :q_lora], qg_ref[...]), qup_ref[...], preferred_element_type=F32)
    kv = jnp.dot(normed(x_ref[:, q_lora:q_lora + kv_lora], kvg_ref[...]), kvup_ref[...], preferred_element_type=F32)
    k_rope = _rope_block(x_ref[:, q_lora + kv_lora:q_lora + kv_lora + LANES], cos_t, sin_t, half).astype(k_ref.dtype)
    for h in range(n_heads):
        q_ref[:, _blk(2 * h)] = (q[:, _blk(2 * h)] * q_scale).astype(q_ref.dtype)
        q_ref[:, _blk(2 * h + 1)] = (_rope_block(q[:, _blk(2 * h + 1)], cos_t, sin_t, half) * q_scale).astype(q_ref.dtype)
        k_ref[:, _blk(2 * h)] = kv[:, _blk(h)].astype(k_ref.dtype)
        k_ref[:, _blk(2 * h + 1)] = k_rope
        v_ref[:, _blk(2 * h)] = kv[:, _blk(n_heads + h)].astype(v_ref.dtype)
        v_ref[:, _blk(2 * h + 1)] = e0


def _mla_prep(pd, tables, q_norm_g, q_up, kv_norm_g, kv_up, n_heads, d_rope, q_scale):
    nb, s_len, wp = pd.shape
    q_lora, kv_lora = q_norm_g.shape[0], kv_norm_g.shape[0]
    d_qk = q_up.shape[1] // n_heads
    d_nope = d_qk - d_rope
    assert d_nope == LANES and kv_up.shape[1] == n_heads * 2 * LANES and q_lora + kv_lora + LANES <= wp
    qw = jnp.pad(q_up.reshape(q_lora, n_heads, d_qk), ((0, 0), (0, 0), (0, 2 * LANES - d_qk)))
    qw = qw.reshape(q_lora, n_heads * 2 * LANES).astype(BF16)
    kvw = kv_up.reshape(kv_lora, n_heads, 2, LANES).transpose(0, 2, 1, 3).reshape(kv_lora, 2 * n_heads * LANES)
    tile = lambda w_: pl.BlockSpec((None, ROW_TILE, w_), lambda b, i: (b, i, 0))
    tab = pl.BlockSpec((ROW_TILE, LANES), lambda b, i: (i, 0))
    full = lambda shape: pl.BlockSpec(shape, lambda b, i: (0,) * len(shape))
    wide = n_heads * 2 * LANES
    out = jax.ShapeDtypeStruct((nb, s_len, wide), BF16)
    return pl.pallas_call(
        functools.partial(_mla_prep_kernel, q_lora=q_lora, kv_lora=kv_lora, n_heads=n_heads, half=d_rope // 2,
                          q_scale=q_scale),
        grid=(nb, s_len // ROW_TILE),
        in_specs=[tile(wp), tab, tab, full((1, q_lora)), full((1, kv_lora)), full((q_lora, wide)),
                  full((kv_lora, wide))],
        out_specs=[tile(wide), tile(wide), tile(wide)],
        out_shape=[out, out, out],
        compiler_params=_cparams(("parallel", "parallel")),
        name="mla_prep",
    )(pd, *tables, q_norm_g.reshape(1, q_lora), kv_norm_g.reshape(1, kv_lora), qw, kvw.astype(BF16))


def _moe_gather_kernel(src_ref, used_ref, tok_ref, o_ref, buf_ref, sem_ref):
    base = pl.program_id(0) * MOE_ROWS

    def copy(r):
        return pltpu.make_async_copy(tok_ref.at[pl.ds(src_ref[base + r], 1), :], buf_ref.at[pl.ds(r, 1), :],
                                     sem_ref.at[0])

    def start(r, carry):
        copy(2 * r).start(priority=0)
        copy(2 * r + 1).start(priority=1)
        return carry

    @pl.when(pl.program_id(0) < used_ref[0])
    def _():
        lax.fori_loop(0, MOE_ROWS // 2, start, 0, unroll=8)
        pltpu.make_async_copy(tok_ref.at[pl.ds(0, MOE_ROWS), :], buf_ref, sem_ref.at[0]).wait()
        o_ref[...] = buf_ref[...].astype(o_ref.dtype)

    @pl.when(pl.program_id(0) >= used_ref[0])
    def _():
        o_ref[...] = jnp.zeros_like(o_ref)


def _used_block(i, used):
    return jnp.minimum(i, used[0] - 1)


def _moe_gather(tok, src, used, n_blocks):
    d = tok.shape[1]
    return pl.pallas_call(
        _moe_gather_kernel,
        grid_spec=pltpu.PrefetchScalarGridSpec(
            num_scalar_prefetch=2,
            grid=(n_blocks,),
            in_specs=[pl.BlockSpec(memory_space=pl.ANY)],
            out_specs=pl.BlockSpec((MOE_ROWS, d), lambda i, src, used: (i, 0)),
            scratch_shapes=[pltpu.VMEM((MOE_ROWS, d), F32), pltpu.SemaphoreType.DMA((1,))],
        ),
        out_shape=jax.ShapeDtypeStruct((n_blocks * MOE_ROWS, d), BF16),
        compiler_params=_cparams(("arbitrary",)),
        name="moe_gather",
    )(src, used, tok)


def _expert_changed(be_ref, i):
    return jnp.logical_or(i == 0, be_ref[i] != be_ref[jnp.maximum(i - 1, 0)])


def _moe_up_kernel(be_ref, used_ref, a_ref, wg_ref, wu_ref, o_ref, wg_bf, wu_bf):
    i = pl.program_id(1)

    @pl.when(i < used_ref[0])
    def _():
        @pl.when(_expert_changed(be_ref, i))
        def _():
            wg_bf[...] = wg_ref[...].astype(BF16)
            wu_bf[...] = wu_ref[...].astype(BF16)

        a = a_ref[...]
        g = jnp.dot(a, wg_bf[...], preferred_element_type=F32)
        u = jnp.dot(a, wu_bf[...], preferred_element_type=F32)
        o_ref[...] = (g * jax.nn.sigmoid(g) * u).astype(o_ref.dtype)

    @pl.when(i >= used_ref[0])
    def _():
        o_ref[...] = jnp.zeros_like(o_ref)


def _moe_block_maps():
    rows = lambda j, i, be, used: (_used_block(i, used), 0)
    weight = lambda j, i, be, used: (be[_used_block(i, used)], 0, j)
    out = lambda j, i, be, used: (i, j)
    return rows, weight, out


def _moe_up(buf, block_expert, used, wg, wu, tn_pref=512):
    rows, d = buf.shape
    f = wg.shape[2]
    tn = _pick(f, tn_pref)
    n_blocks = rows // MOE_ROWS
    row_map, w_map, out_map = _moe_block_maps()
    return pl.pallas_call(
        _moe_up_kernel,
        grid_spec=pltpu.PrefetchScalarGridSpec(
            num_scalar_prefetch=2,
            grid=(f // tn, n_blocks),
            in_specs=[
                pl.BlockSpec((MOE_ROWS, d), row_map),
                pl.BlockSpec((None, d, tn), w_map),
                pl.BlockSpec((None, d, tn), w_map),
            ],
            out_specs=pl.BlockSpec((MOE_ROWS, tn), out_map),
            scratch_shapes=[pltpu.VMEM((d, tn), BF16), pltpu.VMEM((d, tn), BF16)],
        ),
        out_shape=jax.ShapeDtypeStruct((rows, f), BF16),
        compiler_params=_cparams(("arbitrary", "arbitrary")),
        name="moe_up",
    )(block_expert, used, buf, wg, wu)


def _moe_down_kernel(be_ref, used_ref, h_ref, w_ref, o_ref, w_bf):
    i = pl.program_id(1)

    @pl.when(i < used_ref[0])
    def _():
        @pl.when(_expert_changed(be_ref, i))
        def _():
            w_bf[...] = w_ref[...].astype(BF16)

        o_ref[...] = jnp.dot(h_ref[...], w_bf[...], preferred_element_type=F32)

    @pl.when(i >= used_ref[0])
    def _():
        o_ref[...] = jnp.zeros_like(o_ref)


def _moe_down(hid, block_expert, used, wd, tn_pref=512):
    rows, f = hid.shape
    d = wd.shape[2]
    tn = _pick(d, tn_pref)
    n_blocks = rows // MOE_ROWS
    row_map, w_map, out_map = _moe_block_maps()
    return pl.pallas_call(
        _moe_down_kernel,
        grid_spec=pltpu.PrefetchScalarGridSpec(
            num_scalar_prefetch=2,
            grid=(d // tn, n_blocks),
            in_specs=[
                pl.BlockSpec((MOE_ROWS, f), row_map),
                pl.BlockSpec((None, f, tn), w_map),
            ],
            out_specs=pl.BlockSpec((MOE_ROWS, tn), out_map),
            scratch_shapes=[pltpu.VMEM((f, tn), BF16)],
        ),
        out_shape=jax.ShapeDtypeStruct((rows, d), F32),
        compiler_params=_cparams(("arbitrary", "arbitrary")),
        name="moe_down",
    )(block_expert, used, hid, wd)


def _moe_combine_kernel(dest_ref, y_ref, x_ref, gate_ref, g0_ref, g1_ref, o_ref, b0_ref, b1_ref, sem_ref):
    base = (pl.program_id(0) * pl.num_programs(1) + pl.program_id(1)) * ROW_TILE

    def copies(r):
        t = (base + r) * TOP_K
        return (pltpu.make_async_copy(y_ref.at[pl.ds(dest_ref[t], 1), :], b0_ref.at[pl.ds(r, 1), :], sem_ref.at[0]),
                pltpu.make_async_copy(y_ref.at[pl.ds(dest_ref[t + 1], 1), :], b1_ref.at[pl.ds(r, 1), :],
                                      sem_ref.at[1]))

    def start(r, carry):
        c0, c1 = copies(r)
        c0.start(priority=0)
        c1.start(priority=1)
        return carry

    lax.fori_loop(0, ROW_TILE, start, 0, unroll=16)
    pltpu.make_async_copy(y_ref.at[pl.ds(0, ROW_TILE), :], b0_ref, sem_ref.at[0]).wait()
    pltpu.make_async_copy(y_ref.at[pl.ds(0, ROW_TILE), :], b1_ref, sem_ref.at[1]).wait()
    o_ref[...] = x_ref[...] + gate_ref[...] * (g0_ref[...] * b0_ref[...] + g1_ref[...] * b1_ref[...])


def _moe_combine(y, dest, x, gate, g0, g1):
    nb, _, d = x.shape
    t = g0.shape[1]
    return pl.pallas_call(
        _moe_combine_kernel,
        grid_spec=pltpu.PrefetchScalarGridSpec(
            num_scalar_prefetch=1,
            grid=(nb, t // ROW_TILE),
            in_specs=[
                pl.BlockSpec(memory_space=pl.ANY),
                pl.BlockSpec((None, ROW_TILE, d), lambda b, i, dest: (b, i, 0)),
                pl.BlockSpec((None, 1, d), lambda b, i, dest: (b, 0, 0)),
                pl.BlockSpec((None, ROW_TILE, 1), lambda b, i, dest: (b, i, 0)),
                pl.BlockSpec((None, ROW_TILE, 1), lambda b, i, dest: (b, i, 0)),
            ],
            out_specs=pl.BlockSpec((None, ROW_TILE, d), lambda b, i, dest: (b, i, 0)),
            scratch_shapes=[pltpu.VMEM((ROW_TILE, d), F32), pltpu.VMEM((ROW_TILE, d), F32),
                            pltpu.SemaphoreType.DMA((2,))],
        ),
        out_shape=jax.ShapeDtypeStruct((nb, t, d), F32),
        compiler_params=_cparams(("arbitrary", "arbitrary")),
        name="moe_combine",
    )(dest, y, x, gate, g0, g1)


def _moe_layer(a_tok, logits, x, gate, wg, wu, wd):
    nb, t, d = a_tok.shape
    n_exp = wg.shape[0]
    n_tok = nb * t
    n_assign = n_tok * TOP_K
    top_logit, top_idx = lax.top_k(logits.reshape(n_tok, -1)[:, :n_exp], TOP_K)
    gates = jax.nn.softmax(top_logit, axis=-1)
    flat_e = top_idx.reshape(-1)
    onehot = (flat_e[:, None] == jnp.arange(n_exp)[None, :]).astype(jnp.int32)
    rank = jnp.take_along_axis(jnp.cumsum(onehot, axis=0) - onehot, flat_e[:, None], axis=1)[:, 0]
    counts = jnp.sum(onehot, axis=0)
    padded = (counts + MOE_ROWS - 1) // MOE_ROWS * MOE_ROWS
    pad_end = jnp.cumsum(padded)
    pad_start = pad_end - padded
    dest = (pad_start[flat_e] + rank).astype(jnp.int32)
    n_blocks = -(-n_assign // MOE_ROWS) + n_exp
    src = jnp.zeros((n_blocks * MOE_ROWS,), jnp.int32).at[dest].set(jnp.arange(n_assign, dtype=jnp.int32) // TOP_K)
    block_expert = jnp.minimum(
        jnp.searchsorted(pad_end, jnp.arange(n_blocks) * MOE_ROWS, side='right'), n_exp - 1).astype(jnp.int32)
    used = (pad_end[-1:] // MOE_ROWS).astype(jnp.int32)
    buf = _moe_gather(a_tok.reshape(n_tok, d), src, used, n_blocks)
    hid = _moe_up(buf, block_expert, used, wg, wu)
    y = _moe_down(hid, block_expert, used, wd)
    return _moe_combine(y, dest, x, gate, gates[:, 0].reshape(nb, t, 1), gates[:, 1].reshape(nb, t, 1))


def _rope_tables(n_lat, n_ctx, rot_dim):
    rows = n_lat // GRID_W
    row = jnp.repeat(jnp.arange(rows), GRID_W).astype(F32)
    col = jnp.tile(jnp.arange(GRID_W), rows).astype(F32)
    n_freq = rot_dim // 4
    inv = ROPE_THETA ** (-jnp.arange(n_freq, dtype=F32) / n_freq)
    ang = jnp.concatenate([row[:, None] * inv, col[:, None] * inv], axis=-1)
    ang = jnp.concatenate([ang, jnp.zeros((n_ctx, rot_dim // 2), F32)], axis=0)
    return jnp.cos(ang), jnp.sin(ang)


def _rwkv_mixer(pa, n_lat, mu, w0, w_up, a0, a_up, g_up, k_k, k_a, r_k, lnx_w, lnx_b):
    r, v, kk, wl, kd, bb, bonus, g = _rwkv_prep(pa, n_lat // ROW_TILE, mu, w0, w_up, a0, a_up, g_up, k_k, k_a,
                                                 r_k.reshape(-1))
    yf, yr = _rwkv_scan(r, v, kk, wl, kd, bb, n_lat)
    return _rwkv_out(yf, yr, bonus, g, lnx_w, lnx_b)


def _diff_mixer(pb, n_lat, lq1, lk1, lq2, lk2, subln_g, lambda_init, tables):
    dh = subln_g.shape[0] // 2
    nh = pb.shape[2] // 3 // (2 * dh)
    lam = jnp.exp(jnp.sum(lq1 * lk1)) - jnp.exp(jnp.sum(lq2 * lk2)) + lambda_init
    q, k, v1 = _diff_prep(pb, tables, dh // 2, dh ** -0.5 * LOG2E)
    scale = 1.0 - lambda_init
    tq = ATTN_ROWS // 2
    return _diff_attention(q, k, v1, lam, subln_g, nh, scale, tq, n_lat)


def _gqa_mixer(pc, n_lat, q_norm_g, k_norm_g, n_heads, n_kv, tables):
    dh = q_norm_g.shape[0]
    assert dh == LANES
    q, k, v1 = _gqa_prep(pc, tables, q_norm_g, k_norm_g, n_heads, n_kv, dh ** -0.5 * LOG2E)
    group = n_heads // n_kv
    tq = ATTN_ROWS // group
    return _attention(q, k, v1, n_kv, group, dh, dh, tq, n_lat)


def _mla_mixer(pd, n_lat, q_norm_g, q_up, kv_norm_g, kv_up, n_heads, d_rope, tables):
    d_qk = q_up.shape[1] // n_heads
    q, k, v1 = _mla_prep(pd, tables, q_norm_g, q_up, kv_norm_g, kv_up, n_heads, d_rope, d_qk ** -0.5 * LOG2E)
    dk, d_v = 2 * LANES, LANES
    return _attention(q, k, v1, n_heads, 1, dk, d_v, ATTN_ROWS, n_lat, MLA_KEYS)


def _pad_cols(w, mult):
    return jnp.pad(w, ((0, 0), (0, -w.shape[1] % mult)))


def kernel(x, c, ctx, c_ctx, ada_w, ada_b, norm_mix_g, norm_ffn_g, w_in, rwkv_mu, rwkv_w0, rwkv_w_up, rwkv_a0, rwkv_a_up, rwkv_g_up, rwkv_k_k, rwkv_k_a, rwkv_r_k, rwkv_lnx_w, rwkv_lnx_b, diff_lq1, diff_lk1, diff_lq2, diff_lk2, diff_subln_g, gqa_q_norm_g, gqa_k_norm_g, mla_q_norm_g, mla_q_up, mla_kv_norm_g, mla_kv_up, w_out, ffn_w_gate, ffn_w_up, ffn_w_down, moe_router, moe_w_gate, moe_w_up, moe_w_down, final_norm_g):
    nb, n_lat, d = x.shape
    n_ctx = ctx.shape[1]
    depth = ada_w.shape[0]
    assert n_ctx == ROW_TILE and n_lat % ROW_TILE == 0 and n_lat % GRID_W == 0
    n_lat_tiles = n_lat // ROW_TILE
    n_all_tiles = n_lat_tiles + 1

    gw = rwkv_k_k.shape[1]
    cols_a = 3 * gw + rwkv_w_up.shape[2] + rwkv_a_up.shape[2] + rwkv_g_up.shape[1]
    cols_b = 3 * gw
    gqa_dh = gqa_q_norm_g.shape[1]
    mla_rope = 64
    mla_heads = 8
    gqa_heads = gw // gqa_dh
    cols_d = mla_q_norm_g.shape[1] + mla_kv_norm_g.shape[1] + mla_rope
    cols_c = w_in.shape[2] - cols_a - cols_b - cols_d
    gqa_kv = (cols_c - gw) // (2 * gqa_dh)
    offs = (0, cols_a, cols_a + cols_b, cols_a + cols_b + cols_c, w_in.shape[2])

    rope_diff = _lane_tables(_rope_tables(n_lat, n_ctx, diff_subln_g.shape[1] // 2))
    rope_gqa = _lane_tables(_rope_tables(n_lat, n_ctx, gqa_dh))
    rope_mla = _lane_tables(_rope_tables(n_lat, n_ctx, mla_rope))

    cvec = jnp.concatenate([c, c_ctx[None, :], jnp.zeros((8 - nb - 1, d), F32)], axis=0)
    mods = _ada(cvec, ada_w, ada_b)[:, :nb + 1].reshape(depth, nb + 1, 6, 1, d)

    h = jnp.concatenate([x, ctx], axis=1)
    s_len = n_lat + n_ctx
    flat = lambda z: z.reshape(nb * s_len, z.shape[-1])
    for l in range(depth):
        need_ctx = l < depth - 1
        m = [mods[l, :, i] for i in range(6)]

        a = flat(_norm_mod(h, norm_mix_g[l], m[1], m[0], n_all_tiles, n_lat_tiles))
        groups = [_matmul(a, _pad_cols(w_in[l][:, offs[i]:offs[i + 1]], 768).astype(BF16), F32, 1536)
                  .reshape(nb, s_len, -1) for i in range(4)]
        oa = _rwkv_mixer(groups[0], n_lat, rwkv_mu[l], rwkv_w0[l], rwkv_w_up[l],
                         rwkv_a0[l], rwkv_a_up[l], rwkv_g_up[l], rwkv_k_k[l], rwkv_k_a[l], rwkv_r_k[l],
                         rwkv_lnx_w[l], rwkv_lnx_b[l])
        lambda_init = 0.8 - 0.6 * math.exp(-0.3 * l)
        ob = _diff_mixer(groups[1], n_lat, diff_lq1[l], diff_lk1[l], diff_lq2[l], diff_lk2[l],
                         diff_subln_g[l], lambda_init, rope_diff)
        oc = _gqa_mixer(groups[2], n_lat, gqa_q_norm_g[l], gqa_k_norm_g[l], gqa_heads, gqa_kv, rope_gqa)
        od = _mla_mixer(groups[3], n_lat, mla_q_norm_g[l], mla_q_up[l], mla_kv_norm_g[l], mla_kv_up[l],
                        mla_heads, mla_rope, rope_mla)
        h = _matmul_resid([flat(o) for o in (oa, ob, oc, od)], w_out[l].astype(BF16), flat(h), m[2],
                          n_all_tiles, tn_pref=1024).reshape(nb, s_len, d)

        j = l // 2
        if l % 2 == 0:
            d_ff = ffn_w_gate.shape[2]
            pad_f = -d_ff % 512
            w_g = jnp.pad(ffn_w_gate[j], ((0, 0), (0, pad_f))).astype(BF16)
            w_u = jnp.pad(ffn_w_up[j], ((0, 0), (0, pad_f))).astype(BF16)
            w_d = jnp.pad(ffn_w_down[j], ((0, pad_f), (0, 0))).astype(BF16)
            a = flat(_norm_mod(h, norm_ffn_g[l], m[4], m[3], n_all_tiles, n_lat_tiles))
            hid = _matmul_swiglu(a, w_g, w_u)
            h = _matmul_resid([hid], w_d, flat(h), m[5], n_all_tiles).reshape(nb, s_len, d)
        else:
            assert not need_ctx
            a, logits = _norm_router(h, norm_ffn_g[l], m[4], m[3], moe_router[j], n_lat_tiles, n_lat_tiles)
            h = _moe_layer(a, logits, h, m[5][:nb], moe_w_gate[j], moe_w_up[j], moe_w_down[j])
    return _final_norm(h, final_norm_g, n_lat_tiles)
```
